```python
import math
import jax, jax.numpy as jnp
from jax import lax
import numpy as np


D_MODEL = 1024
BATCH = 8
SEQ = 4096
DEPTH = 1

PLE_DIM = 256
RMS_EPS = 1e-6

RET_HEADS = 8
RET_HEAD_DIM = 64
RET_WIDTH = RET_HEADS * RET_HEAD_DIM
RET_CHUNK = 128
RET_GN_EPS = 1e-5
ROPE_BASE = 10000.0

RWKV_HEADS = 8
RWKV_HEAD_DIM = 64
RWKV_WIDTH = RWKV_HEADS * RWKV_HEAD_DIM
DECAY_LORA = 64
AAA_LORA = 64
GATE_LORA = 128
RWKV_GN_EPS = 64e-5
L2_EPS = 1e-12

PEER_HEADS = 8
PEER_N_KEYS = 128
PEER_N_EXPERTS = PEER_N_KEYS * PEER_N_KEYS
PEER_QUERY_DIM = 256
PEER_HALF = PEER_QUERY_DIM // 2
PEER_TOPK = 16
PEER_TOKEN_BLOCK = 128

RET_SIZES = (RET_WIDTH, RET_WIDTH, RET_WIDTH, RET_WIDTH)
RWKV_SIZES = (RWKV_WIDTH, RWKV_WIDTH, RWKV_WIDTH, DECAY_LORA, AAA_LORA, GATE_LORA)
GATE_SIZES = (D_MODEL, D_MODEL)
RET_COLS = sum(RET_SIZES)
RWKV_COLS = sum(RWKV_SIZES)
GATE_COLS = sum(GATE_SIZES)
IN_COLS = RET_COLS + RWKV_COLS + GATE_COLS

kernel_name = 'hybrid_retention_rwkv7_peer_block'


def _split(z, sizes):
    offs = [sum(sizes[:i + 1]) for i in range(len(sizes) - 1)]
    return jnp.split(z, offs, axis=-1)


def _rmsnorm(x, g):
    xf = x.astype(jnp.float32)
    y = xf * lax.rsqrt(jnp.mean(xf * xf, axis=-1, keepdims=True) + RMS_EPS)
    return (y * g.astype(jnp.float32)).astype(x.dtype)


def _head_norm(o, eps):
    mu = jnp.mean(o, axis=-1, keepdims=True)
    oc = o - mu
    var = jnp.mean(oc * oc, axis=-1, keepdims=True)
    y = oc * lax.rsqrt(var + eps)
    return y.reshape(o.shape[0], o.shape[1], -1)


def _rotary(x):
    S, d = x.shape[1], x.shape[-1]
    half = d // 2
    inv_freq = ROPE_BASE ** (-jnp.arange(half, dtype=jnp.float32) * 2.0 / d)
    ang = jnp.arange(S, dtype=jnp.float32)[:, None] * inv_freq[None, :]
    cos = jnp.cos(ang)[None, :, None, :]
    sin = jnp.sin(ang)[None, :, None, :]
    x1, x2 = x[..., :half], x[..., half:]
    return jnp.concatenate([x1 * cos - x2 * sin, x1 * sin + x2 * cos], axis=-1)


def _retention_chunkwise(q, k, v):
    B, S, H, d = q.shape
    C = RET_CHUNK
    NC = S // C
    log_gamma = jnp.log1p(-(2.0 ** (-5.0 - jnp.arange(H, dtype=jnp.float32))))
    idx = jnp.arange(C, dtype=jnp.float32)
    diff = idx[:, None] - idx[None, :]
    causal = diff >= 0
    intra_decay = jnp.where(causal[None], jnp.exp(log_gamma[:, None, None] * jnp.where(causal, diff, 0.0)[None]), 0.0)
    zeta = jnp.exp(log_gamma[:, None] * (C - 1.0 - idx)[None, :])
    xi = jnp.exp(log_gamma[:, None] * (idx + 1.0)[None, :])
    chunk_decay = jnp.exp(log_gamma * C)

    qc = q.reshape(B, NC, C, H, d)
    kc = k.reshape(B, NC, C, H, d)
    vc = v.reshape(B, NC, C, H, d)
    scores = jnp.einsum('bnihd,bnjhd->bnhij', qc, kc) * intra_decay[None, None]
    intra = jnp.einsum('bnhij,bnjhd->bnihd', scores, vc)
    chunk_kv = jnp.einsum('bnjhd,hj,bnjhe->bnhde', kc, zeta, vc)

    def step(R, kv):
        return R * chunk_decay[None, :, None, None] + kv, R

    R0 = jnp.zeros((B, H, d, d), chunk_kv.dtype)
    _, R_prev = lax.scan(step, R0, jnp.moveaxis(chunk_kv, 1, 0))
    R_prev = jnp.moveaxis(R_prev, 0, 1)
    cross = jnp.einsum('bnihd,bnhde,hi->bnihe', qc, R_prev, xi)
    return (intra + cross).reshape(B, S, H, d)


def _wkv7(r, w, k, v, kk, a):
    B, S, H, d = r.shape

    def step(state, inp):
        r_t, w_t, k_t, v_t, kk_t, a_t = inp
        sa = jnp.einsum('bhvk,bhk->bhv', state, -kk_t)
        state = (state * w_t[:, :, None, :]
                 + sa[..., None] * (kk_t * a_t)[:, :, None, :]
                 + v_t[..., None] * k_t[:, :, None, :])
        y = jnp.einsum('bhvk,bhk->bhv', state, r_t)
        return state, y

    xs = (jnp.moveaxis(r, 1, 0), jnp.moveaxis(w, 1, 0), jnp.moveaxis(k, 1, 0),
          jnp.moveaxis(v, 1, 0), jnp.moveaxis(kk, 1, 0), jnp.moveaxis(a, 1, 0))
    _, ys = lax.scan(step, jnp.zeros((B, H, d, d), jnp.float32), xs)
    return jnp.moveaxis(ys, 0, 1)


def _token_shift(z, mu):
    z_prev = jnp.concatenate([jnp.zeros_like(z[:, :1]), z[:, :-1]], axis=1)
    return z + (z_prev - z) * mu


def _token_mixers(h, w_in, ret_gn_g, rwkv_mu, rwkv_w0, rwkv_w_up, rwkv_a0, rwkv_a_up,
                  rwkv_g_up, rwkv_k_k, rwkv_k_a, rwkv_r_k, rwkv_gn_g, rwkv_gn_b,
                  w_ret_br, w_rwkv_br, w_o):
    B, S, _ = h.shape
    f32 = jnp.float32
    z = h @ w_in
    z_ret = z[..., :RET_COLS]
    z_rwkv = z[..., RET_COLS:RET_COLS + RWKV_COLS]
    z_gate = z[..., RET_COLS + RWKV_COLS:]

    q, k, v, gr = _split(z_ret, RET_SIZES)
    rh = lambda t: t.astype(f32).reshape(B, S, RET_HEADS, RET_HEAD_DIM)
    q = _rotary(rh(q))
    k = _rotary(rh(k)) * (RET_HEAD_DIM ** -0.5)
    o_ret = _retention_chunkwise(q, k, rh(v))
    y_ret = jax.nn.silu(gr.astype(f32)) * (_head_norm(o_ret, RET_GN_EPS) * ret_gn_g)
    branch_ret = (y_ret @ w_ret_br.astype(f32)).astype(h.dtype)

    zs = _token_shift(z_rwkv, rwkv_mu).astype(f32)
    r, kr, vr, wl, al, gl = _split(zs, RWKV_SIZES)
    w_log = -jax.nn.softplus(-(rwkv_w0 + jnp.tanh(wl) @ rwkv_w_up)) - 0.5
    decay = jnp.exp(-jnp.exp(w_log))
    a = jax.nn.sigmoid(rwkv_a0 + al @ rwkv_a_up)
    g = jax.nn.sigmoid(gl) @ rwkv_g_up
    hh = lambda t: t.reshape(B, S, RWKV_HEADS, RWKV_HEAD_DIM)
    kk = hh(kr * rwkv_k_k)
    kk = kk / jnp.maximum(jnp.sqrt(jnp.sum(kk * kk, axis=-1, keepdims=True)), L2_EPS)
    kr = kr * (1.0 + (a - 1.0) * rwkv_k_a)
    r_h, k_h, v_h = hh(r), hh(kr), hh(vr)
    o = _wkv7(r_h, hh(decay), k_h, v_h, kk, hh(a))
    bonus = jnp.sum(r_h * k_h * rwkv_r_k, axis=-1, keepdims=True) * v_h
    y_rwkv = (_head_norm(o, RWKV_GN_EPS) * rwkv_gn_g + rwkv_gn_b + bonus.reshape(B, S, -1)) * g
    branch_rwkv = (y_rwkv @ w_rwkv_br.astype(f32)).astype(h.dtype)

    gate_ret, gate_rwkv = _split(z_gate, GATE_SIZES)
    merged = jax.nn.sigmoid(gate_ret) * branch_ret + jax.nn.sigmoid(gate_rwkv) * branch_rwkv
    return merged @ w_o


def _peer(h, w_pq, sub_keys, expert_u, expert_v):
    B, S, D = h.shape
    K = PEER_TOPK
    q = (h @ w_pq).reshape(B, S, PEER_HEADS, 2, PEER_HALF)
    s = jnp.einsum('bshcd,hckd->bshck', q, sub_keys)
    top_s, top_i = lax.top_k(s, K)
    cand_s = (top_s[..., 0, :, None] + top_s[..., 1, None, :]).reshape(B, S, PEER_HEADS, K * K)
    cand_i = (top_i[..., 0, :, None] * PEER_N_KEYS + top_i[..., 1, None, :]).reshape(B, S, PEER_HEADS, K * K)
    best_s, best_pos = lax.top_k(cand_s, K)
    ids = jnp.take_along_axis(cand_i, best_pos, axis=-1)
    gates = jax.nn.softmax(best_s.astype(jnp.float32), axis=-1).astype(h.dtype)

    T = B * S
    NB = T // PEER_TOKEN_BLOCK
    hb = h.reshape(NB, PEER_TOKEN_BLOCK, D)
    ib = ids.reshape(NB, PEER_TOKEN_BLOCK, PEER_HEADS * K)
    gb = gates.reshape(NB, PEER_TOKEN_BLOCK, PEER_HEADS * K)

    def block(args):
        hx, ix, gx = args
        u = expert_u[ix]
        act = jax.nn.gelu(jnp.einsum('td,tkd->tk', hx, u))
        vv = expert_v[ix]
        return jnp.einsum('tk,tkd->td', gx * act, vv)

    out = lax.map(block, (hb, ib, gb))
    return out.reshape(B, S, D)


def setup_inputs(seed: int = 0) -> dict:
    key = jax.random.key(seed)
    ks = iter(jax.random.split(key, 32))
    nrm = lambda shape, scale: jax.random.normal(next(ks), shape, jnp.float32) * scale
    uni = lambda shape, lo, hi: jax.random.uniform(next(ks), shape, jnp.float32, lo, hi)
    L = DEPTH
    return {
        'x': nrm((BATCH, SEQ, D_MODEL), 1.0),
        'p': nrm((DEPTH, BATCH, SEQ, PLE_DIM), 1.0),
        'g_mix': 1.0 + nrm((L, D_MODEL), 0.02),
        'w_in': nrm((L, D_MODEL, IN_COLS), D_MODEL ** -0.5),
        'ret_gn_g': 1.0 + nrm((L, RET_WIDTH), 0.02),
        'rwkv_mu': uni((L, RWKV_COLS), 0.0, 1.0),
        'rwkv_w0': uni((L, RWKV_WIDTH), -6.0, -1.0),
        'rwkv_w_up': nrm((L, DECAY_LORA, RWKV_WIDTH), 0.5 * DECAY_LORA ** -0.5),
        'rwkv_a0': nrm((L, RWKV_WIDTH), 0.1),
        'rwkv_a_up': nrm((L, AAA_LORA, RWKV_WIDTH), AAA_LORA ** -0.5),
        'rwkv_g_up': nrm((L, GATE_LORA, RWKV_WIDTH), GATE_LORA ** -0.5),
        'rwkv_k_k': 0.85 + nrm((L, RWKV_WIDTH), 0.05),
        'rwkv_k_a': 1.0 + nrm((L, RWKV_WIDTH), 0.05),
        'rwkv_r_k': nrm((L, RWKV_HEADS, RWKV_HEAD_DIM), 0.1),
        'rwkv_gn_g': 1.0 + nrm((L, RWKV_WIDTH), 0.02),
        'rwkv_gn_b': nrm((L, RWKV_WIDTH), 0.02),
        'w_ret_br': nrm((L, RET_WIDTH, D_MODEL), RET_WIDTH ** -0.5),
        'w_rwkv_br': nrm((L, RWKV_WIDTH, D_MODEL), RWKV_WIDTH ** -0.5),
        'w_o': nrm((L, D_MODEL, D_MODEL), D_MODEL ** -0.5),
        'g_ffn': 1.0 + nrm((L, D_MODEL), 0.02),
        'w_pq': nrm((L, D_MODEL, PEER_HEADS * PEER_QUERY_DIM), D_MODEL ** -0.5),
        'peer_sub_keys': nrm((L, PEER_HEADS, 2, PEER_N_KEYS, PEER_HALF), PEER_HALF ** -0.5),
        'peer_u': nrm((L, PEER_N_EXPERTS, D_MODEL), D_MODEL ** -0.5),
        'peer_v': nrm((L, PEER_N_EXPERTS, D_MODEL), PEER_HEADS ** -0.5),
        'g_ple': 1.0 + nrm((L, D_MODEL), 0.02),
        'w_ple_gate': nrm((L, D_MODEL, D_MODEL), D_MODEL ** -0.5),
        'w_ple_up': nrm((L, PLE_DIM, D_MODEL), PLE_DIM ** -0.5),
        'g_final': 1.0 + nrm((D_MODEL,), 0.02),
    }


def reference(x, p, g_mix, w_in, ret_gn_g, rwkv_mu, rwkv_w0, rwkv_w_up, rwkv_a0, rwkv_a_up,
              rwkv_g_up, rwkv_k_k, rwkv_k_a, rwkv_r_k, rwkv_gn_g, rwkv_gn_b, w_ret_br,
              w_rwkv_br, w_o, g_ffn, w_pq, peer_sub_keys, peer_u, peer_v, g_ple,
              w_ple_gate, w_ple_up, g_final):
    for i in range(DEPTH):
        h = _rmsnorm(x, g_mix[i])
        x = x + _token_mixers(h, w_in[i], ret_gn_g[i], rwkv_mu[i], rwkv_w0[i], rwkv_w_up[i],
                              rwkv_a0[i], rwkv_a_up[i], rwkv_g_up[i], rwkv_k_k[i], rwkv_k_a[i],
                              rwkv_r_k[i], rwkv_gn_g[i], rwkv_gn_b[i], w_ret_br[i],
                              w_rwkv_br[i], w_o[i]).astype(x.dtype)
        h2 = _rmsnorm(x, g_ffn[i])
        x = x + _peer(h2, w_pq[i], peer_sub_keys[i], peer_u[i], peer_v[i]).astype(x.dtype)
        ple_gate = jax.nn.sigmoid(_rmsnorm(x, g_ple[i]) @ w_ple_gate[i])
        x = x + (ple_gate * (p[i] @ w_ple_up[i])).astype(x.dtype)
    return _rmsnorm(x, g_final)
```

```python
import functools
import math

import jax
import jax.numpy as jnp
from jax import lax
from jax.experimental import pallas as pl
from jax.experimental.pallas import tpu as pltpu

F32 = jnp.float32
BF16 = jnp.bfloat16

RMS_EPS = 1e-6
HEAD_DIM = 64
N_HEADS = 8
WIDTH = N_HEADS * HEAD_DIM
RET_CHUNK = 128
RET_GN_EPS = 1e-5
ROPE_BASE = 10000.0
RWKV_GN_EPS = 64e-5
L2_EPS = 1e-12
DECAY_LORA = 64
AAA_LORA = 64
GATE_LORA = 128
RWKV_COLS = 3 * WIDTH + DECAY_LORA + AAA_LORA + GATE_LORA
WKV_CHUNK = 64

PEER_HEADS = 8
PEER_N_KEYS = 128
PEER_HALF = 128
PEER_TOPK = 16
PEER_GROUPS = 2 * PEER_HEADS
PEER_SEL = PEER_HEADS * PEER_TOPK
PEER_TOK = 8

ROW_TILE = 256
VMEM_LIMIT = 48 * 1024 * 1024

_NT = (((1,), (1,)), ((), ()))
_TN = (((0,), (0,)), ((), ()))


def _bdot(a, b):
    return jnp.dot(a.astype(BF16), b.astype(BF16), preferred_element_type=F32)


def _bdot_nt(a, b):
    return lax.dot_general(a.astype(BF16), b.astype(BF16), _NT, preferred_element_type=F32)


def _bdot_tn(a, b):
    return lax.dot_general(a.astype(BF16), b.astype(BF16), _TN, preferred_element_type=F32)


def _split2(a):
    hi = a.astype(BF16)
    lo = (a - hi.astype(F32)).astype(BF16)
    return hi, lo


def _seg_dot(a, m):
    hi, lo = _split2(a)
    return (jnp.dot(hi, m, preferred_element_type=F32)
            + jnp.dot(lo, m, preferred_element_type=F32))


def _rms(x, g):
    return x * lax.rsqrt(jnp.mean(x * x, axis=-1, keepdims=True) + RMS_EPS) * g


def _sigmoid(x):
    return 1.0 / (1.0 + jnp.exp(-x))


def _head_norm(o, mavg, eps):
    mu = _seg_dot(o, mavg)
    oc = o - mu
    var = _seg_dot(oc * oc, mavg)
    return oc * lax.rsqrt(var + eps)


def _cparams(sem, vmem=VMEM_LIMIT):
    return pltpu.CompilerParams(dimension_semantics=sem, vmem_limit_bytes=vmem)


def _full(shape):
    nd = len(shape)
    return pl.BlockSpec(shape, lambda *_: (0,) * nd)


def _in_proj_kernel(x_ref, g_ref, w1_ref, w2_ref, w3_ref, o1_ref, o2_ref, o3_ref):
    h = _rms(x_ref[...], g_ref[...]).astype(BF16)
    o1_ref[...] = jnp.dot(h, w1_ref[...], preferred_element_type=F32)
    o2_ref[...] = jnp.dot(h, w2_ref[...], preferred_element_type=F32)
    o3_ref[...] = jnp.dot(h, w3_ref[...], preferred_element_type=F32)


def _in_proj(x2, g, w_ret, w_rwkv, w_gate):
    T, D = x2.shape
    tm = ROW_TILE
    ws = (w_ret, w_rwkv, w_gate)
    return pl.pallas_call(
        _in_proj_kernel,
        grid=(T // tm,),
        in_specs=[pl.BlockSpec((tm, D), lambda i: (i, 0)), _full((1, D))]
        + [_full(w.shape) for w in ws],
        out_specs=[pl.BlockSpec((tm, w.shape[1]), lambda i: (i, 0)) for w in ws],
        out_shape=[jax.ShapeDtypeStruct((T, w.shape[1]), F32) for w in ws],
        compiler_params=_cparams(("parallel",)),
        name="in_proj",
    )(x2, g, *ws)


def _retention_kernel(z_ref, cos_ref, sin_ref, xi_ref, zeta_ref, decay_ref, cd_ref,
                      gn_ref, mavg_ref, y_ref, state_ref):
    @pl.when(pl.program_id(1) == 0)
    def _():
        state_ref[...] = jnp.zeros_like(state_ref)

    z = z_ref[0]
    q = z[:, 0:WIDTH]
    k = z[:, WIDTH:2 * WIDTH]
    v = z[:, 2 * WIDTH:3 * WIDTH]
    gr = z[:, 3 * WIDTH:4 * WIDTH]
    cos = cos_ref[...]
    sin = sin_ref[...]
    lane = lax.broadcasted_iota(jnp.int32, q.shape, 1)
    first_half = (lane % HEAD_DIM) < (HEAD_DIM // 2)

    def rot(x):
        partner = jnp.where(first_half, pltpu.roll(x, WIDTH - HEAD_DIM // 2, 1),
                            pltpu.roll(x, HEAD_DIM // 2, 1))
        return x * cos + partner * sin

    qr = rot(q)
    kr = rot(k) * (HEAD_DIM ** -0.5)
    qx = qr * xi_ref[...]
    kz = kr * zeta_ref[...]
    outs = []
    for h in range(N_HEADS):
        sl = slice(h * HEAD_DIM, (h + 1) * HEAD_DIM)
        vh = v[:, sl]
        scores = _bdot_nt(qr[:, sl], kr[:, sl]) * decay_ref[h]
        state = state_ref[h]
        outs.append(_bdot(scores, vh) + _bdot(qx[:, sl], state))
        state_ref[h] = state * cd_ref[h] + _bdot_tn(kz[:, sl], vh)
    o = jnp.concatenate(outs, axis=1)
    y = _head_norm(o, mavg_ref[...], RET_GN_EPS)
    y_ref[0] = gr * _sigmoid(gr) * (y * gn_ref[...])


def _retention(z_ret, gn_g, mavg):
    B, S, _ = z_ret.shape
    C = RET_CHUNK
    half = HEAD_DIM // 2
    inv_freq = ROPE_BASE ** (-jnp.arange(half, dtype=F32) * 2.0 / HEAD_DIM)
    ang = jnp.arange(S, dtype=F32)[:, None] * inv_freq[None, :]
    cos_h = jnp.concatenate([jnp.cos(ang), jnp.cos(ang)], axis=1)
    sin_h = jnp.concatenate([-jnp.sin(ang), jnp.sin(ang)], axis=1)
    cos = jnp.tile(cos_h, (1, N_HEADS))
    sin = jnp.tile(sin_h, (1, N_HEADS))
    log_gamma = jnp.log1p(-(2.0 ** (-5.0 - jnp.arange(N_HEADS, dtype=F32))))
    idx = jnp.arange(C, dtype=F32)
    diff = idx[:, None] - idx[None, :]
    causal = diff >= 0
    decay = jnp.where(causal[None], jnp.exp(log_gamma[:, None, None] * jnp.where(causal, diff, 0.0)[None]), 0.0)
    zeta = jnp.exp(log_gamma[:, None] * (C - 1.0 - idx)[None, :])
    xi = jnp.exp(log_gamma[:, None] * (idx + 1.0)[None, :])
    widen = lambda t: jnp.repeat(t.T, HEAD_DIM, axis=1)
    cd = jnp.broadcast_to(jnp.exp(log_gamma * C)[:, None, None], (N_HEADS, HEAD_DIM, HEAD_DIM))
    return pl.pallas_call(
        _retention_kernel,
        grid=(B, S // C),
        in_specs=[pl.BlockSpec((1, C, 4 * WIDTH), lambda b, c: (b, c, 0)),
                  pl.BlockSpec((C, WIDTH), lambda b, c: (c, 0)),
                  pl.BlockSpec((C, WIDTH), lambda b, c: (c, 0)),
                  _full((C, WIDTH)), _full((C, WIDTH)), _full((N_HEADS, C, C)),
                  _full((N_HEADS, HEAD_DIM, HEAD_DIM)), _full((1, WIDTH)), _full((WIDTH, WIDTH))],
        out_specs=pl.BlockSpec((1, C, WIDTH), lambda b, c: (b, c, 0)),
        out_shape=jax.ShapeDtypeStruct((B, S, WIDTH), F32),
        scratch_shapes=[pltpu.VMEM((N_HEADS, HEAD_DIM, HEAD_DIM), F32)],
        compiler_params=_cparams(("parallel", "arbitrary")),
        name="retention",
    )(z_ret, cos, sin, widen(xi), widen(zeta), decay, cd, gn_g, mavg)


def _rwkv_prep_kernel(z_ref, mu_ref, w0_ref, wup_ref, a0_ref, aup_ref, gup_ref, kk_ref,
                      ka_ref, rk_ref, mones_ref,
                      r_out, lw_out, k_out, v_out, kk_out, kka_out, g_out, bonus_out,
                      carry_ref):
    @pl.when(pl.program_id(1) == 0)
    def _():
        carry_ref[...] = jnp.zeros_like(carry_ref)

    z = z_ref[0]
    n = z.shape[0]
    row = lax.broadcasted_iota(jnp.int32, z.shape, 0)
    prev = jnp.where(row == 0, carry_ref[0:1, :], pltpu.roll(z, 1, 0))
    carry_ref[0:1, :] = z[n - 1:n, :]
    zs = z + (prev - z) * mu_ref[...]
    r = zs[:, 0:WIDTH]
    kr = zs[:, WIDTH:2 * WIDTH]
    vr = zs[:, 2 * WIDTH:3 * WIDTH]
    o = 3 * WIDTH
    wl = zs[:, o:o + DECAY_LORA]
    al = zs[:, o + DECAY_LORA:o + DECAY_LORA + AAA_LORA]
    gl = zs[:, o + DECAY_LORA + AAA_LORA:]
    t = -(w0_ref[...] + _bdot(jnp.tanh(wl), wup_ref[...]))
    softplus = jnp.maximum(t, 0.0) + jnp.log1p(jnp.exp(-jnp.abs(t)))
    w_log = -softplus - 0.5
    a = _sigmoid(a0_ref[...] + _bdot(al, aup_ref[...]))
    g = _bdot(_sigmoid(gl), gup_ref[...])
    mones = mones_ref[...]
    kk = kr * kk_ref[...]
    norm = jnp.sqrt(_seg_dot(kk * kk, mones))
    kk = kk / jnp.maximum(norm, L2_EPS)
    k2 = kr * (1.0 + (a - 1.0) * ka_ref[...])
    r_out[0] = r
    lw_out[0] = -jnp.exp(w_log)
    k_out[0] = k2
    v_out[0] = vr
    kk_out[0] = kk
    kka_out[0] = kk * a
    g_out[0] = g
    bonus_out[0] = _seg_dot(r * k2 * rk_ref[...], mones) * vr


def _rwkv_prep(z_rwkv, mu, w0, w_up, a0, a_up, g_up, k_k, k_a, r_k, mones):
    B, S, _ = z_rwkv.shape
    ts = ROW_TILE
    row = lambda t: t.reshape(1, -1)
    args = (row(mu), row(w0), w_up.astype(BF16), row(a0), a_up.astype(BF16),
            g_up.astype(BF16), row(k_k), row(k_a), row(r_k), mones)
    out_spec = pl.BlockSpec((1, ts, WIDTH), lambda b, s: (b, s, 0))
    return pl.pallas_call(
        _rwkv_prep_kernel,
        grid=(B, S // ts),
        in_specs=[pl.BlockSpec((1, ts, RWKV_COLS), lambda b, s: (b, s, 0))]
        + [_full(a.shape) for a in args],
        out_specs=[out_spec] * 8,
        out_shape=[jax.ShapeDtypeStruct((B, S, WIDTH), F32)] * 8,
        scratch_shapes=[pltpu.VMEM((8, RWKV_COLS), F32)],
        compiler_params=_cparams(("parallel", "arbitrary")),
        name="rwkv_prep",
    )(z_rwkv, *args)


def _wkv7_kernel(r_ref, lw_ref, k_ref, v_ref, kk_ref, kka_ref, tri_ref, o_ref, state_ref):
    @pl.when(pl.program_id(1) == 0)
    def _():
        state_ref[...] = jnp.zeros_like(state_ref)

    L = WKV_CHUNK
    lw = lw_ref[0]
    tri = tri_ref[...]
    hi = lw.astype(BF16)
    rem = lw - hi.astype(F32)
    mid = rem.astype(BF16)
    lo = (rem - mid.astype(F32)).astype(BF16)
    cum = (jnp.dot(tri, hi, preferred_element_type=F32)
           + jnp.dot(tri, mid, preferred_element_type=F32)
           + jnp.dot(tri, lo, preferred_element_type=F32))
    cum_last = cum[L - 1:L, :]
    inv_g = jnp.exp(-cum)
    to_end = jnp.exp(cum_last - cum)
    g_last = jnp.exp(cum_last)
    kk = kk_ref[0]
    kka = kka_ref[0]
    k = k_ref[0]
    v = v_ref[0]
    a_t = -kk * jnp.exp(cum - lw)
    b_t = kka * inv_g
    k_t = k * inv_g
    r_t = r_ref[0] * jnp.exp(cum)
    b_end = kka * to_end
    k_end = k * to_end
    ri = lax.broadcasted_iota(jnp.int32, (L, L), 0)
    ci = lax.broadcasted_iota(jnp.int32, (L, L), 1)
    strict = ri > ci
    incl = ri >= ci
    eye = (ri == ci).astype(F32)
    outs = []
    for h in range(N_HEADS):
        sl = slice(h * HEAD_DIM, (h + 1) * HEAD_DIM)
        ah, bh, kh, rh, vh = a_t[:, sl], b_t[:, sl], k_t[:, sl], r_t[:, sl], v[:, sl]
        a_ab = jnp.where(strict, _bdot_nt(ah, bh), 0.0)
        a_ak = jnp.where(strict, _bdot_nt(ah, kh), 0.0)
        a_rb = jnp.where(incl, _bdot_nt(rh, bh), 0.0)
        a_rk = jnp.where(incl, _bdot_nt(rh, kh), 0.0)
        inv = eye + a_ab
        p = a_ab
        for _ in range(int(math.log2(L)) - 1):
            p = _bdot(p, p)
            inv = inv + _bdot(inv, p)
        state = state_ref[h]
        u = _bdot(inv, _bdot_nt(ah, state) + _bdot(a_ak, vh))
        outs.append(_bdot_nt(rh, state) + _bdot(a_rb, u) + _bdot(a_rk, vh))
        state_ref[h] = (state * g_last[:, sl] + _bdot_tn(u, b_end[:, sl])
                        + _bdot_tn(vh, k_end[:, sl]))
    o_ref[0] = jnp.concatenate(outs, axis=1)


def _wkv7(r, lw, k, v, kk, kka):
    B, S, _ = r.shape
    L = WKV_CHUNK
    tri = jnp.tril(jnp.ones((L, L), F32)).astype(BF16)
    spec = pl.BlockSpec((1, L, WIDTH), lambda b, c: (b, c, 0))
    return pl.pallas_call(
        _wkv7_kernel,
        grid=(B, S // L),
        in_specs=[spec] * 6 + [_full((L, L))],
        out_specs=spec,
        out_shape=jax.ShapeDtypeStruct((B, S, WIDTH), F32),
        scratch_shapes=[pltpu.VMEM((N_HEADS, HEAD_DIM, HEAD_DIM), F32)],
        compiler_params=_cparams(("parallel", "arbitrary")),
        name="wkv7",
    )(r, lw, k, v, kk, kka, tri)


def _merge_kernel(x_ref, yret_ref, o_ref, bonus_ref, g_ref, zg_ref, gng_ref, gnb_ref,
                  mavg_ref, wret_ref, wrwkv_ref, wo_ref, x1_ref):
    y = _head_norm(o_ref[...], mavg_ref[...], RWKV_GN_EPS)
    y_rwkv = (y * gng_ref[...] + gnb_ref[...] + bonus_ref[...]) * g_ref[...]
    br = _bdot(yret_ref[...], wret_ref[...])
    bw = _bdot(y_rwkv, wrwkv_ref[...])
    zg = zg_ref[...]
    d = br.shape[1]
    merged = _sigmoid(zg[:, :d]) * br + _sigmoid(zg[:, d:]) * bw
    x1_ref[...] = x_ref[...] + _bdot(merged, wo_ref[...])


def _merge(x2, y_ret, o, bonus, g, z_gate, gn_g, gn_b, mavg, w_ret_br, w_rwkv_br, w_o):
    T, D = x2.shape
    tm = ROW_TILE
    rows = lambda n: pl.BlockSpec((tm, n), lambda i: (i, 0))
    consts = (gn_g, gn_b, mavg, w_ret_br, w_rwkv_br, w_o)
    return pl.pallas_call(
        _merge_kernel,
        grid=(T // tm,),
        in_specs=[rows(D), rows(WIDTH), rows(WIDTH), rows(WIDTH), rows(WIDTH), rows(2 * D)]
        + [_full(c.shape) for c in consts],
        out_specs=rows(D),
        out_shape=jax.ShapeDtypeStruct((T, D), F32),
        compiler_params=_cparams(("parallel",)),
        name="merge",
    )(x2, y_ret, o, bonus, g, z_gate, *consts)


def _peer_scores_kernel(x_ref, g_ref, wq_ref, keys_ref, h_ref, s_ref):
    h = _rms(x_ref[...], g_ref[...])
    h_ref[...] = h
    q = _bdot(h, wq_ref[...]).astype(BF16)
    for grp in range(PEER_GROUPS):
        s_ref[grp] = lax.dot_general(keys_ref[grp], q[:, grp * PEER_HALF:(grp + 1) * PEER_HALF],
                                     _NT, preferred_element_type=F32)


def _peer_scores(x1, g_ffn, w_pq, keys):
    T, D = x1.shape
    tm = ROW_TILE
    return pl.pallas_call(
        _peer_scores_kernel,
        grid=(T // tm,),
        in_specs=[pl.BlockSpec((tm, D), lambda i: (i, 0)), _full((1, D)), _full(w_pq.shape),
                  _full(keys.shape)],
        out_specs=[pl.BlockSpec((tm, D), lambda i: (i, 0)),
                   pl.BlockSpec((PEER_GROUPS, PEER_N_KEYS, tm), lambda i: (0, 0, i))],
        out_shape=[jax.ShapeDtypeStruct((T, D), F32),
                   jax.ShapeDtypeStruct((PEER_GROUPS, PEER_N_KEYS, T), F32)],
        compiler_params=_cparams(("parallel",)),
        name="peer_scores",
    )(x1, g_ffn, w_pq, keys)


def _top_rows(s, count):
    n = s.shape[0]
    rows = lax.broadcasted_iota(jnp.int32, s.shape, 0)
    vals, idxs = [], []
    for _ in range(count):
        m = jnp.max(s, axis=0, keepdims=True)
        idx = jnp.min(jnp.where(s == m, rows, n), axis=0, keepdims=True)
        s = jnp.where(rows == idx, -jnp.inf, s)
        vals.append(m)
        idxs.append(idx)
    return jnp.concatenate(vals, axis=0), jnp.concatenate(idxs, axis=0)


def _pick_rows(table, idx):
    rows = lax.broadcasted_iota(jnp.int32, table.shape, 0)
    out = []
    for j in range(idx.shape[0]):
        out.append(jnp.sum(jnp.where(rows == idx[j:j + 1, :], table, 0), axis=0, keepdims=True))
    return jnp.concatenate(out, axis=0)


def _peer_topk_kernel(s_ref, ids_ref, gates_ref):
    K = PEER_TOPK

    def head(h, carry):
        s0, i0 = _top_rows(s_ref[2 * h], K)
        s1, i1 = _top_rows(s_ref[2 * h + 1], K)
        cand = jnp.concatenate([s0[i:i + 1, :] + s1 for i in range(K)], axis=0)
        best, pos = _top_rows(cand, K)
        ids = _pick_rows(i0, pos // K) * PEER_N_KEYS + _pick_rows(i1, pos % K)
        e = jnp.exp(best - best[0:1, :])
        gates = e / jnp.sum(e, axis=0, keepdims=True)
        off = pl.multiple_of(h * K, K)
        ids_ref[pl.ds(off, K), :] = ids
        gates_ref[pl.ds(off, K), :] = gates
        return carry

    lax.fori_loop(0, PEER_HEADS, head, 0)


def _peer_topk(scores_t):
    _, _, T = scores_t.shape
    tk = 128
    return pl.pallas_call(
        _peer_topk_kernel,
        grid=(T // tk,),
        in_specs=[pl.BlockSpec((PEER_GROUPS, PEER_N_KEYS, tk), lambda i: (0, 0, i))],
        out_specs=[pl.BlockSpec((PEER_SEL, tk), lambda i: (0, i))] * 2,
        out_shape=[jax.ShapeDtypeStruct((PEER_SEL, T), jnp.int32),
                   jax.ShapeDtypeStruct((PEER_SEL, T), F32)],
        compiler_params=_cparams(("parallel",)),
        name="peer_topk",
    )(scores_t)


def _gelu_tanh(x):
    return 0.5 * x * (1.0 + jnp.tanh(math.sqrt(2.0 / math.pi) * (x + 0.044715 * x * x * x)))


def _peer_experts_kernel(ids_cur, ids_next, gates_ref, h_ref, x_ref, uv_hbm, out_ref, buf, sem):
    i = pl.program_id(0)
    n = pl.num_programs(0)
    slot = i % 2
    rows = PEER_TOK * PEER_SEL
    d = h_ref.shape[1]

    def issue(ids_ref, dst_slot):
        def body(j, carry):
            e = ids_ref[0, j // PEER_SEL, j % PEER_SEL]
            pltpu.make_async_copy(uv_hbm.at[pl.ds(e, 1)], buf.at[dst_slot, pl.ds(j, 1)],
                                  sem.at[dst_slot]).start()
            return carry
        lax.fori_loop(0, rows, body, 0, unroll=8)

    @pl.when(i == 0)
    def _():
        issue(ids_cur, 0)

    @pl.when(i + 1 < n)
    def _():
        issue(ids_next, 1 - slot)

    pltpu.make_async_copy(uv_hbm.at[pl.ds(0, rows)], buf.at[slot], sem.at[slot]).wait()

    hb = h_ref[...].astype(BF16)
    gates = gates_ref[...]
    cur = buf.at[slot]
    outs = []
    for t in range(PEER_TOK):
        u = cur[t * PEER_SEL:(t + 1) * PEER_SEL, 0:d]
        v = cur[t * PEER_SEL:(t + 1) * PEER_SEL, d:2 * d]
        act = _bdot_nt(hb, u)[t:t + 1, :]
        w = _gelu_tanh(act) * gates[t:t + 1, :]
        outs.append(_bdot(jnp.broadcast_to(w, (8, PEER_SEL)), v)[0:1, :])
    out_ref[...] = x_ref[...] + jnp.concatenate(outs, axis=0)


def _peer_experts(ids, gates, h2, x1, uv):
    T, D = x1.shape
    tb = PEER_TOK
    nb = T // tb
    ids3 = ids.reshape(nb, tb, PEER_SEL)
    smem = lambda imap: pl.BlockSpec((1, tb, PEER_SEL), imap, memory_space=pltpu.SMEM)
    return pl.pallas_call(
        _peer_experts_kernel,
        grid=(nb,),
        in_specs=[smem(lambda i: (i, 0, 0)),
                  smem(lambda i: (jnp.minimum(i + 1, nb - 1), 0, 0)),
                  pl.BlockSpec((tb, PEER_SEL), lambda i: (i, 0)),
                  pl.BlockSpec((tb, D), lambda i: (i, 0)),
                  pl.BlockSpec((tb, D), lambda i: (i, 0)),
                  pl.BlockSpec(memory_space=pl.ANY)],
        out_specs=pl.BlockSpec((tb, D), lambda i: (i, 0)),
        out_shape=jax.ShapeDtypeStruct((T, D), F32),
        scratch_shapes=[pltpu.VMEM((2, tb * PEER_SEL, 2 * D), F32),
                        pltpu.SemaphoreType.DMA((2,))],
        compiler_params=_cparams(("arbitrary",)),
        name="peer_experts",
    )(ids3, ids3, gates, h2, x1, uv)


def _ple_final_kernel(x_ref, p_ref, gple_ref, wg_ref, wu_ref, gfin_ref, out_ref):
    x = x_ref[...]
    gate = _sigmoid(_bdot(_rms(x, gple_ref[...]), wg_ref[...]))
    x = x + gate * _bdot(p_ref[...], wu_ref[...])
    out_ref[...] = _rms(x, gfin_ref[...])


def _ple_final(x2, p2, g_ple, w_gate, w_up, g_final):
    T, D = x2.shape
    tm = ROW_TILE
    consts = (g_ple, w_gate, w_up, g_final)
    return pl.pallas_call(
        _ple_final_kernel,
        grid=(T // tm,),
        in_specs=[pl.BlockSpec((tm, D), lambda i: (i, 0)),
                  pl.BlockSpec((tm, p2.shape[1]), lambda i: (i, 0)),
                  _full(g_ple.shape), _full(w_gate.shape), _full(w_up.shape), _full(g_final.shape)],
        out_specs=pl.BlockSpec((tm, D), lambda i: (i, 0)),
        out_shape=jax.ShapeDtypeStruct((T, D), F32),
        compiler_params=_cparams(("parallel",)),
        name="ple_final",
    )(x2, p2, *consts)


def kernel(x, p, g_mix, w_in, ret_gn_g, rwkv_mu, rwkv_w0, rwkv_w_up, rwkv_a0, rwkv_a_up, rwkv_g_up, rwkv_k_k, rwkv_k_a, rwkv_r_k, rwkv_gn_g, rwkv_gn_b, w_ret_br, w_rwkv_br, w_o, g_ffn, w_pq, peer_sub_keys, peer_u, peer_v, g_ple, w_ple_gate, w_ple_up, g_final):
    B, S, D = x.shape
    T = B * S
    assert w_in.shape[0] == 1, "single-layer block: the final RMSNorm is fused into its last step"
    i = 0
    row = lambda t: t.reshape(1, -1)
    head_of = jnp.arange(WIDTH) // HEAD_DIM
    same_head = head_of[:, None] == head_of[None, :]
    mones = same_head.astype(BF16)
    mavg = (same_head.astype(F32) / HEAD_DIM).astype(BF16)
    ret_cols = 4 * WIDTH
    x2 = x.reshape(T, D)
    wi = w_in[i].astype(BF16)
    z_ret, z_rwkv, z_gate = _in_proj(
        x2, row(g_mix[i]), wi[:, :ret_cols], wi[:, ret_cols:ret_cols + RWKV_COLS],
        wi[:, ret_cols + RWKV_COLS:])
    y_ret = _retention(z_ret.reshape(B, S, ret_cols), row(ret_gn_g[i]), mavg)
    r, lw, k, v, kk, kka, g, bonus = _rwkv_prep(
        z_rwkv.reshape(B, S, RWKV_COLS), rwkv_mu[i], rwkv_w0[i], rwkv_w_up[i], rwkv_a0[i],
        rwkv_a_up[i], rwkv_g_up[i], rwkv_k_k[i], rwkv_k_a[i], rwkv_r_k[i], mones)
    o = _wkv7(r, lw, k, v, kk, kka)
    flat = lambda t: t.reshape(T, WIDTH)
    x2 = _merge(x2, flat(y_ret), flat(o), flat(bonus), flat(g), z_gate,
                row(rwkv_gn_g[i]), row(rwkv_gn_b[i]), mavg, w_ret_br[i].astype(BF16),
                w_rwkv_br[i].astype(BF16), w_o[i].astype(BF16))
    keys = peer_sub_keys[i].reshape(PEER_GROUPS, PEER_N_KEYS, PEER_HALF).astype(BF16)
    h2, scores_t = _peer_scores(x2, row(g_ffn[i]), w_pq[i].astype(BF16), keys)
    ids_t, gates_t = _peer_topk(scores_t)
    uv = jnp.concatenate([peer_u[i], peer_v[i]], axis=1)
    x2 = _peer_experts(ids_t.T, gates_t.T, h2, x2, uv)
    x2 = _ple_final(x2, p[i].reshape(T, -1), row(g_ple[i]), w_ple_gate[i].astype(BF16),
                    w_ple_up[i].astype(BF16), row(g_final))
    return x2.reshape(B, S, D)
```

```python
import functools
import math

import jax
import jax.numpy as jnp
from jax import lax
from jax.experimental import pallas as pl
from jax.experimental.pallas import tpu as pltpu

F32 = jnp.float32
BF16 = jnp.bfloat16

RMS_EPS = 1e-6
HEAD_DIM = 64
N_HEADS = 8
WIDTH = N_HEADS * HEAD_DIM
RET_CHUNK = 128
RET_GN_EPS = 1e-5
ROPE_BASE = 10000.0
RWKV_GN_EPS = 64e-5
L2_EPS = 1e-12
DECAY_LORA = 64
AAA_LORA = 64
GATE_LORA = 128
RWKV_COLS = 3 * WIDTH + DECAY_LORA + AAA_LORA + GATE_LORA
WKV_CHUNK = 64

PEER_HEADS = 8
PEER_N_KEYS = 128
PEER_HALF = 128
PEER_TOPK = 16
PEER_GROUPS = 2 * PEER_HEADS
PEER_SEL = PEER_HEADS * PEER_TOPK
PEER_TOK = 8

ROW_TILE = 256
VMEM_LIMIT = 48 * 1024 * 1024

_NT = (((1,), (1,)), ((), ()))
_TN = (((0,), (0,)), ((), ()))


def _bdot(a, b):
    return jnp.dot(a.astype(BF16), b.astype(BF16), preferred_element_type=F32)


def _bdot_nt(a, b):
    return lax.dot_general(a.astype(BF16), b.astype(BF16), _NT, preferred_element_type=F32)


def _bdot_tn(a, b):
    return lax.dot_general(a.astype(BF16), b.astype(BF16), _TN, preferred_element_type=F32)


def _split2(a):
    hi = a.astype(BF16)
    lo = (a - hi.astype(F32)).astype(BF16)
    return hi, lo


def _seg_dot(a, m):
    hi, lo = _split2(a)
    return (jnp.dot(hi, m, preferred_element_type=F32)
            + jnp.dot(lo, m, preferred_element_type=F32))


def _rms(x, g):
    return x * lax.rsqrt(jnp.mean(x * x, axis=-1, keepdims=True) + RMS_EPS) * g


def _sigmoid(x):
    return 1.0 / (1.0 + jnp.exp(-x))


def _head_norm(o, mavg, eps):
    mu = _seg_dot(o, mavg)
    oc = o - mu
    var = _seg_dot(oc * oc, mavg)
    return oc * lax.rsqrt(var + eps)


def _cparams(sem, vmem=VMEM_LIMIT):
    return pltpu.CompilerParams(dimension_semantics=sem, vmem_limit_bytes=vmem)


def _full(shape):
    nd = len(shape)
    return pl.BlockSpec(shape, lambda *_: (0,) * nd)


def _in_proj_kernel(x_ref, g_ref, w1_ref, w2_ref, w3_ref, o1_ref, o2_ref, o3_ref):
    h = _rms(x_ref[...], g_ref[...]).astype(BF16)
    o1_ref[...] = jnp.dot(h, w1_ref[...], preferred_element_type=F32)
    o2_ref[...] = jnp.dot(h, w2_ref[...], preferred_element_type=F32)
    o3_ref[...] = jnp.dot(h, w3_ref[...], preferred_element_type=F32)


def _in_proj(x2, g, w_ret, w_rwkv, w_gate):
    T, D = x2.shape
    tm = ROW_TILE
    ws = (w_ret, w_rwkv, w_gate)
    return pl.pallas_call(
        _in_proj_kernel,
        grid=(T // tm,),
        in_specs=[pl.BlockSpec((tm, D), lambda i: (i, 0)), _full((1, D))]
        + [_full(w.shape) for w in ws],
        out_specs=[pl.BlockSpec((tm, w.shape[1]), lambda i: (i, 0)) for w in ws],
        out_shape=[jax.ShapeDtypeStruct((T, w.shape[1]), F32) for w in ws],
        compiler_params=_cparams(("parallel",)),
        name="in_proj",
    )(x2, g, *ws)


def _retention_kernel(z_ref, cos_ref, sin_ref, xi_ref, zeta_ref, decay_ref, cd_ref,
                      gn_ref, mavg_ref, y_ref, state_ref):
    @pl.when(pl.program_id(1) == 0)
    def _():
        state_ref[...] = jnp.zeros_like(state_ref)

    z = z_ref[0]
    q = z[:, 0:WIDTH]
    k = z[:, WIDTH:2 * WIDTH]
    v = z[:, 2 * WIDTH:3 * WIDTH]
    gr = z[:, 3 * WIDTH:4 * WIDTH]
    cos = cos_ref[...]
    sin = sin_ref[...]
    lane = lax.broadcasted_iota(jnp.int32, q.shape, 1)
    first_half = (lane % HEAD_DIM) < (HEAD_DIM // 2)

    def rot(x):
        partner = jnp.where(first_half, pltpu.roll(x, WIDTH - HEAD_DIM // 2, 1),
                            pltpu.roll(x, HEAD_DIM // 2, 1))
        return x * cos + partner * sin

    qr = rot(q)
    kr = rot(k) * (HEAD_DIM ** -0.5)
    qx = qr * xi_ref[...]
    kz = kr * zeta_ref[...]
    outs = []
    for h in range(N_HEADS):
        sl = slice(h * HEAD_DIM, (h + 1) * HEAD_DIM)
        vh = v[:, sl]
        scores = _bdot_nt(qr[:, sl], kr[:, sl]) * decay_ref[h]
        state = state_ref[h]
        outs.append(_bdot(scores, vh) + _bdot(qx[:, sl], state))
        state_ref[h] = state * cd_ref[h] + _bdot_tn(kz[:, sl], vh)
    o = jnp.concatenate(outs, axis=1)
    y = _head_norm(o, mavg_ref[...], RET_GN_EPS)
    y_ref[0] = gr * _sigmoid(gr) * (y * gn_ref[...])


def _retention(z_ret, gn_g, mavg):
    B, S, _ = z_ret.shape
    C = RET_CHUNK
    half = HEAD_DIM // 2
    inv_freq = ROPE_BASE ** (-jnp.arange(half, dtype=F32) * 2.0 / HEAD_DIM)
    ang = jnp.arange(S, dtype=F32)[:, None] * inv_freq[None, :]
    cos_h = jnp.concatenate([jnp.cos(ang), jnp.cos(ang)], axis=1)
    sin_h = jnp.concatenate([-jnp.sin(ang), jnp.sin(ang)], axis=1)
    cos = jnp.tile(cos_h, (1, N_HEADS))
    sin = jnp.tile(sin_h, (1, N_HEADS))
    log_gamma = jnp.log1p(-(2.0 ** (-5.0 - jnp.arange(N_HEADS, dtype=F32))))
    idx = jnp.arange(C, dtype=F32)
    diff = idx[:, None] - idx[None, :]
    causal = diff >= 0
    decay = jnp.where(causal[None], jnp.exp(log_gamma[:, None, None] * jnp.where(causal, diff, 0.0)[None]), 0.0)
    zeta = jnp.exp(log_gamma[:, None] * (C - 1.0 - idx)[None, :])
    xi = jnp.exp(log_gamma[:, None] * (idx + 1.0)[None, :])
    widen = lambda t: jnp.repeat(t.T, HEAD_DIM, axis=1)
    cd = jnp.broadcast_to(jnp.exp(log_gamma * C)[:, None, None], (N_HEADS, HEAD_DIM, HEAD_DIM))
    return pl.pallas_call(
        _retention_kernel,
        grid=(B, S // C),
        in_specs=[pl.BlockSpec((1, C, 4 * WIDTH), lambda b, c: (b, c, 0)),
                  pl.BlockSpec((C, WIDTH), lambda b, c: (c, 0)),
                  pl.BlockSpec((C, WIDTH), lambda b, c: (c, 0)),
                  _full((C, WIDTH)), _full((C, WIDTH)), _full((N_HEADS, C, C)),
                  _full((N_HEADS, HEAD_DIM, HEAD_DIM)), _full((1, WIDTH)), _full((WIDTH, WIDTH))],
        out_specs=pl.BlockSpec((1, C, WIDTH), lambda b, c: (b, c, 0)),
        out_shape=jax.ShapeDtypeStruct((B, S, WIDTH), F32),
        scratch_shapes=[pltpu.VMEM((N_HEADS, HEAD_DIM, HEAD_DIM), F32)],
        compiler_params=_cparams(("parallel", "arbitrary")),
        name="retention",
    )(z_ret, cos, sin, widen(xi), widen(zeta), decay, cd, gn_g, mavg)


def _rwkv_prep_kernel(z_ref, mu_ref, w0_ref, wup_ref, a0_ref, aup_ref, gup_ref, kk_ref,
                      ka_ref, rk_ref, mones_ref,
                      r_out, lw_out, k_out, v_out, kk_out, kka_out, g_out, bonus_out,
                      carry_ref):
    @pl.when(pl.program_id(1) == 0)
    def _():
        carry_ref[...] = jnp.zeros_like(carry_ref)

    z = z_ref[0]
    n = z.shape[0]
    row = lax.broadcasted_iota(jnp.int32, z.shape, 0)
    prev = jnp.where(row == 0, carry_ref[0:1, :], pltpu.roll(z, 1, 0))
    carry_ref[0:1, :] = z[n - 1:n, :]
    zs = z + (prev - z) * mu_ref[...]
    r = zs[:, 0:WIDTH]
    kr = zs[:, WIDTH:2 * WIDTH]
    vr = zs[:, 2 * WIDTH:3 * WIDTH]
    o = 3 * WIDTH
    wl = zs[:, o:o + DECAY_LORA]
    al = zs[:, o + DECAY_LORA:o + DECAY_LORA + AAA_LORA]
    gl = zs[:, o + DECAY_LORA + AAA_LORA:]
    t = -(w0_ref[...] + _bdot(jnp.tanh(wl), wup_ref[...]))
    softplus = jnp.maximum(t, 0.0) + jnp.log1p(jnp.exp(-jnp.abs(t)))
    w_log = -softplus - 0.5
    a = _sigmoid(a0_ref[...] + _bdot(al, aup_ref[...]))
    g = _bdot(_sigmoid(gl), gup_ref[...])
    mones = mones_ref[...]
    kk = kr * kk_ref[...]
    norm = jnp.sqrt(_seg_dot(kk * kk, mones))
    kk = kk / jnp.maximum(norm, L2_EPS)
    k2 = kr * (1.0 + (a - 1.0) * ka_ref[...])
    r_out[0] = r
    lw_out[0] = -jnp.exp(w_log)
    k_out[0] = k2
    v_out[0] = vr
    kk_out[0] = kk
    kka_out[0] = kk * a
    g_out[0] = g
    bonus_out[0] = _seg_dot(r * k2 * rk_ref[...], mones) * vr


def _rwkv_prep(z_rwkv, mu, w0, w_up, a0, a_up, g_up, k_k, k_a, r_k, mones):
    B, S, _ = z_rwkv.shape
    ts = ROW_TILE
    row = lambda t: t.reshape(1, -1)
    args = (row(mu), row(w0), w_up.astype(BF16), row(a0), a_up.astype(BF16),
            g_up.astype(BF16), row(k_k), row(k_a), row(r_k), mones)
    out_spec = pl.BlockSpec((1, ts, WIDTH), lambda b, s: (b, s, 0))
    return pl.pallas_call(
        _rwkv_prep_kernel,
        grid=(B, S // ts),
        in_specs=[pl.BlockSpec((1, ts, RWKV_COLS), lambda b, s: (b, s, 0))]
        + [_full(a.shape) for a in args],
        out_specs=[out_spec] * 8,
        out_shape=[jax.ShapeDtypeStruct((B, S, WIDTH), F32)] * 8,
        scratch_shapes=[pltpu.VMEM((8, RWKV_COLS), F32)],
        compiler_params=_cparams(("parallel", "arbitrary")),
        name="rwkv_prep",
    )(z_rwkv, *args)


def _wkv7_kernel(r_ref, lw_ref, k_ref, v_ref, kk_ref, kka_ref, tri_ref, o_ref, state_ref):
    @pl.when(pl.program_id(1) == 0)
    def _():
        state_ref[...] = jnp.zeros_like(state_ref)

    L = WKV_CHUNK
    lw = lw_ref[0]
    tri = tri_ref[...]
    hi = lw.astype(BF16)
    rem = lw - hi.astype(F32)
    mid = rem.astype(BF16)
    lo = (rem - mid.astype(F32)).astype(BF16)
    cum = (jnp.dot(tri, hi, preferred_element_type=F32)
           + jnp.dot(tri, mid, preferred_element_type=F32)
           + jnp.dot(tri, lo, preferred_element_type=F32))
    cum_last = cum[L - 1:L, :]
    inv_g = jnp.exp(-cum)
    to_end = jnp.exp(cum_last - cum)
    g_last = jnp.exp(cum_last)
    kk = kk_ref[0]
    kka = kka_ref[0]
    k = k_ref[0]
    v = v_ref[0]
    a_t = -kk * jnp.exp(cum - lw)
    b_t = kka * inv_g
    k_t = k * inv_g
    r_t = r_ref[0] * jnp.exp(cum)
    b_end = kka * to_end
    k_end = k * to_end
    ri = lax.broadcasted_iota(jnp.int32, (2 * L, L), 0)
    ci = lax.broadcasted_iota(jnp.int32, (2 * L, L), 1)
    mask = jnp.where(ri < L, ri, ri - L + 1) > ci
    eye = (lax.broadcasted_iota(jnp.int32, (L, L), 0)
           == lax.broadcasted_iota(jnp.int32, (L, L), 1)).astype(F32)
    heads = range(N_HEADS)
    sls = [slice(h * HEAD_DIM, (h + 1) * HEAD_DIM) for h in heads]
    ar = [jnp.concatenate([a_t[:, sl], r_t[:, sl]], axis=0).astype(BF16) for sl in sls]
    vs = [v[:, sl].astype(BF16) for sl in sls]
    abrb = [jnp.where(mask, _bdot_nt(ar[h], b_t[:, sls[h]]), 0.0) for h in heads]
    akrk = [jnp.where(mask, _bdot_nt(ar[h], k_t[:, sls[h]]), 0.0) for h in heads]
    p = [m[:L] for m in abrb]
    inv = [eye + m for m in p]
    for _ in range(int(math.log2(L)) - 1):
        p = [_bdot(m, m) for m in p]
        inv = [inv[h] + _bdot(inv[h], p[h]) for h in heads]
    states = [state_ref[h] for h in heads]
    xs = [_bdot_nt(ar[h], states[h]) for h in heads]
    kv = [_bdot(akrk[h], vs[h]) for h in heads]
    u = [_bdot(inv[h], xs[h][:L] + kv[h][:L]) for h in heads]
    y = [xs[h][L:] + kv[h][L:] + _bdot(abrb[h][L:], u[h]) for h in heads]
    for h in heads:
        uv = jnp.concatenate([u[h], v[:, sls[h]]], axis=0)
        bk_end = jnp.concatenate([b_end[:, sls[h]], k_end[:, sls[h]]], axis=0)
        state_ref[h] = states[h] * g_last[:, sls[h]] + _bdot_tn(uv, bk_end)
    o_ref[0] = jnp.concatenate(y, axis=1)


def _wkv7(r, lw, k, v, kk, kka):
    B, S, _ = r.shape
    L = WKV_CHUNK
    tri = jnp.tril(jnp.ones((L, L), F32)).astype(BF16)
    spec = pl.BlockSpec((1, L, WIDTH), lambda b, c: (b, c, 0))
    return pl.pallas_call(
        _wkv7_kernel,
        grid=(B, S // L),
        in_specs=[spec] * 6 + [_full((L, L))],
        out_specs=spec,
        out_shape=jax.ShapeDtypeStruct((B, S, WIDTH), F32),
        scratch_shapes=[pltpu.VMEM((N_HEADS, HEAD_DIM, HEAD_DIM), F32)],
        compiler_params=_cparams(("parallel", "arbitrary")),
        name="wkv7",
    )(r, lw, k, v, kk, kka, tri)


def _merge_kernel(x_ref, yret_ref, o_ref, bonus_ref, g_ref, zg_ref, gng_ref, gnb_ref,
                  mavg_ref, wret_ref, wrwkv_ref, wo_ref, x1_ref):
    y = _head_norm(o_ref[...], mavg_ref[...], RWKV_GN_EPS)
    y_rwkv = (y * gng_ref[...] + gnb_ref[...] + bonus_ref[...]) * g_ref[...]
    br = _bdot(yret_ref[...], wret_ref[...])
    bw = _bdot(y_rwkv, wrwkv_ref[...])
    zg = zg_ref[...]
    d = br.shape[1]
    merged = _sigmoid(zg[:, :d]) * br + _sigmoid(zg[:, d:]) * bw
    x1_ref[...] = x_ref[...] + _bdot(merged, wo_ref[...])


def _merge(x2, y_ret, o, bonus, g, z_gate, gn_g, gn_b, mavg, w_ret_br, w_rwkv_br, w_o):
    T, D = x2.shape
    tm = ROW_TILE
    rows = lambda n: pl.BlockSpec((tm, n), lambda i: (i, 0))
    consts = (gn_g, gn_b, mavg, w_ret_br, w_rwkv_br, w_o)
    return pl.pallas_call(
        _merge_kernel,
        grid=(T // tm,),
        in_specs=[rows(D), rows(WIDTH), rows(WIDTH), rows(WIDTH), rows(WIDTH), rows(2 * D)]
        + [_full(c.shape) for c in consts],
        out_specs=rows(D),
        out_shape=jax.ShapeDtypeStruct((T, D), F32),
        compiler_params=_cparams(("parallel",)),
        name="merge",
    )(x2, y_ret, o, bonus, g, z_gate, *consts)


def _peer_scores_kernel(x_ref, g_ref, wq_ref, keys_ref, h_ref, s_ref):
    h = _rms(x_ref[...], g_ref[...])
    h_ref[...] = h
    q = _bdot(h, wq_ref[...]).astype(BF16)
    for grp in range(PEER_GROUPS):
        s_ref[grp] = lax.dot_general(keys_ref[grp], q[:, grp * PEER_HALF:(grp + 1) * PEER_HALF],
                                     _NT, preferred_element_type=F32)


def _peer_scores(x1, g_ffn, w_pq, keys):
    T, D = x1.shape
    tm = ROW_TILE
    return pl.pallas_call(
        _peer_scores_kernel,
        grid=(T // tm,),
        in_specs=[pl.BlockSpec((tm, D), lambda i: (i, 0)), _full((1, D)), _full(w_pq.shape),
                  _full(keys.shape)],
        out_specs=[pl.BlockSpec((tm, D), lambda i: (i, 0)),
                   pl.BlockSpec((PEER_GROUPS, PEER_N_KEYS, tm), lambda i: (0, 0, i))],
        out_shape=[jax.ShapeDtypeStruct((T, D), F32),
                   jax.ShapeDtypeStruct((PEER_GROUPS, PEER_N_KEYS, T), F32)],
        compiler_params=_cparams(("parallel",)),
        name="peer_scores",
    )(x1, g_ffn, w_pq, keys)


def _top_rows(s, count, payload=None):
    rows = lax.broadcasted_iota(jnp.int32, s.shape, 0).astype(F32)
    vals, picks = [], []
    for _ in range(count):
        m = jnp.max(s, axis=0, keepdims=True)
        idx = jnp.min(jnp.where(s == m, rows, float(s.shape[0])), axis=0, keepdims=True)
        hit = rows == idx
        s = jnp.where(hit, -jnp.inf, s)
        vals.append(m)
        if payload is None:
            picks.append(idx)
        else:
            picks.append(jnp.max(jnp.where(hit, payload, -1), axis=0, keepdims=True))
    return jnp.concatenate(vals, axis=0), jnp.concatenate(picks, axis=0)


def _pair_rows(a, b):
    K = PEER_TOPK
    out = []
    for i in range(K // 2):
        jn = K if i == 0 else K // 2
        out.append((a[i:i + 1, :], b[0:jn, :]))
    out.append((a[K // 2:K, :], b[0:1, :]))
    return out


def _peer_topk_kernel(s_ref, ids_ref, gates_ref):
    K = PEER_TOPK

    def head(h, carry):
        s0, i0 = _top_rows(s_ref[2 * h], K)
        s1, i1 = _top_rows(s_ref[2 * h + 1], K)
        e0 = i0.astype(jnp.int32) * PEER_N_KEYS
        e1 = i1.astype(jnp.int32)
        cand = jnp.concatenate([x + y for x, y in _pair_rows(s0, s1)], axis=0)
        cand_id = jnp.concatenate([x + y for x, y in _pair_rows(e0, e1)], axis=0)
        best, ids = _top_rows(cand, K, payload=cand_id)
        e = jnp.exp(best - best[0:1, :])
        gates = e / jnp.sum(e, axis=0, keepdims=True)
        off = pl.multiple_of(h * K, K)
        ids_ref[pl.ds(off, K), :] = ids
        gates_ref[pl.ds(off, K), :] = gates
        return carry

    lax.fori_loop(0, PEER_HEADS, head, 0)


def _peer_topk(scores_t):
    _, _, T = scores_t.shape
    tk = 128
    return pl.pallas_call(
        _peer_topk_kernel,
        grid=(T // tk,),
        in_specs=[pl.BlockSpec((PEER_GROUPS, PEER_N_KEYS, tk), lambda i: (0, 0, i))],
        out_specs=[pl.BlockSpec((PEER_SEL, tk), lambda i: (0, i))] * 2,
        out_shape=[jax.ShapeDtypeStruct((PEER_SEL, T), jnp.int32),
                   jax.ShapeDtypeStruct((PEER_SEL, T), F32)],
        compiler_params=_cparams(("parallel",)),
        name="peer_topk",
    )(scores_t)


def _gelu_tanh(x):
    return 0.5 * x * (1.0 + jnp.tanh(math.sqrt(2.0 / math.pi) * (x + 0.044715 * x * x * x)))


def _peer_experts_kernel(ids_cur, ids_next, gexp_ref, h_ref, x_ref, uv_hbm, out_ref, buf, sem):
    i = pl.program_id(0)
    n = pl.num_programs(0)
    slot = i % 2
    sub = h_ref.shape[1]
    cols = PEER_SEL * sub

    def issue_token(ids_ref, t, dst_slot):
        for k in range(PEER_SEL):
            e = ids_ref[0, t, k]
            pltpu.make_async_copy(uv_hbm.at[pl.ds(e, 1)],
                                  buf.at[dst_slot, pl.ds(t * PEER_SEL + k, 1)],
                                  sem.at[dst_slot, t]).start(priority=k % 2)

    def wait_token(t, src_slot):
        pltpu.make_async_copy(uv_hbm.at[pl.ds(0, PEER_SEL)],
                              buf.at[src_slot, pl.ds(t * PEER_SEL, PEER_SEL)],
                              sem.at[src_slot, t]).wait()

    def for_tokens(fn):
        def body(t, carry):
            fn(t)
            return carry
        lax.fori_loop(0, PEER_TOK, body, 0)

    @pl.when(i == 0)
    def _():
        for_tokens(lambda t: issue_token(ids_cur, t, 0))

    lane = lax.broadcasted_iota(jnp.int32, (sub, cols), 1)
    diag = (lane % sub) == lax.broadcasted_iota(jnp.int32, (sub, cols), 0)

    def group_sum(x):
        step = 1
        while step < sub:
            partner = jnp.where((lane % (2 * step)) < step, pltpu.roll(x, cols - step, 1),
                                pltpu.roll(x, step, 1))
            x = x + partner
            step *= 2
        return x

    cur = buf.at[slot]
    for t in range(PEER_TOK):
        wait_token(t, slot)
        rows = pl.ds(t * PEER_SEL, PEER_SEL)
        u2 = cur[rows, 0:sub, :].reshape(cols, 128).astype(BF16)
        v2 = cur[rows, sub:2 * sub, :].reshape(cols, 128).astype(BF16)
        p = lax.dot_general(h_ref[t].astype(BF16), u2, _NT, preferred_element_type=F32)
        act = jnp.sum(group_sum(jnp.where(diag, p, 0.0)), axis=0, keepdims=True)
        w = _gelu_tanh(act) * gexp_ref[t:t + 1, :]
        wm = jnp.where(diag, jnp.broadcast_to(w, (sub, cols)), 0.0).astype(BF16)
        o = x_ref[t] + jnp.dot(wm, v2, preferred_element_type=F32)
        issue_token(ids_next, t, 1 - slot)
        out_ref[t] = o

    @pl.when(i == n - 1)
    def _():
        for_tokens(lambda t: wait_token(t, 1 - slot))


def _peer_experts(ids, gates, h2, x1, uv):
    T, D = x1.shape
    tb = PEER_TOK
    nb = T // tb
    sub = D // 128
    ids3 = ids.reshape(nb, tb, PEER_SEL)
    gexp = jnp.repeat(gates, sub, axis=1)
    smem = lambda imap: pl.BlockSpec((1, tb, PEER_SEL), imap, memory_space=pltpu.SMEM)
    tiles = pl.BlockSpec((tb, sub, 128), lambda i: (i, 0, 0))
    out = pl.pallas_call(
        _peer_experts_kernel,
        grid=(nb,),
        in_specs=[smem(lambda i: (i, 0, 0)),
                  smem(lambda i: (jnp.minimum(i + 1, nb - 1), 0, 0)),
                  pl.BlockSpec((tb, PEER_SEL * sub), lambda i: (i, 0)),
                  tiles, tiles,
                  pl.BlockSpec(memory_space=pl.ANY)],
        out_specs=tiles,
        out_shape=jax.ShapeDtypeStruct((T, sub, 128), F32),
        scratch_shapes=[pltpu.VMEM((2, tb * PEER_SEL, 2 * sub, 128), F32),
                        pltpu.SemaphoreType.DMA((2, tb))],
        compiler_params=_cparams(("arbitrary",)),
        name="peer_experts",
    )(ids3, ids3, gexp, h2.reshape(T, sub, 128), x1.reshape(T, sub, 128), uv)
    return out.reshape(T, D)


def _ple_final_kernel(x_ref, p_ref, gple_ref, wg_ref, wu_ref, gfin_ref, out_ref):
    x = x_ref[...]
    gate = _sigmoid(_bdot(_rms(x, gple_ref[...]), wg_ref[...]))
    x = x + gate * _bdot(p_ref[...], wu_ref[...])
    out_ref[...] = _rms(x, gfin_ref[...])


def _ple_final(x2, p2, g_ple, w_gate, w_up, g_final):
    T, D = x2.shape
    tm = ROW_TILE
    consts = (g_ple, w_gate, w_up, g_final)
    return pl.pallas_call(
        _ple_final_kernel,
        grid=(T // tm,),
        in_specs=[pl.BlockSpec((tm, D), lambda i: (i, 0)),
                  pl.BlockSpec((tm, p2.shape[1]), lambda i: (i, 0)),
                  _full(g_ple.shape), _full(w_gate.shape), _full(w_up.shape), _full(g_final.shape)],
        out_specs=pl.BlockSpec((tm, D), lambda i: (i, 0)),
        out_shape=jax.ShapeDtypeStruct((T, D), F32),
        compiler_params=_cparams(("parallel",)),
        name="ple_final",
    )(x2, p2, *consts)


def kernel(x, p, g_mix, w_in, ret_gn_g, rwkv_mu, rwkv_w0, rwkv_w_up, rwkv_a0, rwkv_a_up, rwkv_g_up, rwkv_k_k, rwkv_k_a, rwkv_r_k, rwkv_gn_g, rwkv_gn_b, w_ret_br, w_rwkv_br, w_o, g_ffn, w_pq, peer_sub_keys, peer_u, peer_v, g_ple, w_ple_gate, w_ple_up, g_final):
    B, S, D = x.shape
    T = B * S
    assert w_in.shape[0] == 1, "single-layer block: the final RMSNorm is fused into its last step"
    i = 0
    row = lambda t: t.reshape(1, -1)
    head_of = jnp.arange(WIDTH) // HEAD_DIM
    same_head = head_of[:, None] == head_of[None, :]
    mones = same_head.astype(BF16)
    mavg = (same_head.astype(F32) / HEAD_DIM).astype(BF16)
    ret_cols = 4 * WIDTH
    x2 = x.reshape(T, D)
    wi = w_in[i].astype(BF16)
    z_ret, z_rwkv, z_gate = _in_proj(
        x2, row(g_mix[i]), wi[:, :ret_cols], wi[:, ret_cols:ret_cols + RWKV_COLS],
        wi[:, ret_cols + RWKV_COLS:])
    y_ret = _retention(z_ret.reshape(B, S, ret_cols), row(ret_gn_g[i]), mavg)
    r, lw, k, v, kk, kka, g, bonus = _rwkv_prep(
        z_rwkv.reshape(B, S, RWKV_COLS), rwkv_mu[i], rwkv_w0[i], rwkv_w_up[i], rwkv_a0[i],
        rwkv_a_up[i], rwkv_g_up[i], rwkv_k_k[i], rwkv_k_a[i], rwkv_r_k[i], mones)
    o = _wkv7(r, lw, k, v, kk, kka)
    flat = lambda t: t.reshape(T, WIDTH)
    x2 = _merge(x2, flat(y_ret), flat(o), flat(bonus), flat(g), z_gate,
                row(rwkv_gn_g[i]), row(rwkv_gn_b[i]), mavg, w_ret_br[i].astype(BF16),
                w_rwkv_br[i].astype(BF16), w_o[i].astype(BF16))
    keys = peer_sub_keys[i].reshape(PEER_GROUPS, PEER_N_KEYS, PEER_HALF).astype(BF16)
    h2, scores_t = _peer_scores(x2, row(g_ffn[i]), w_pq[i].astype(BF16), keys)
    ids_t, gates_t = _peer_topk(scores_t)
    n_exp = peer_u.shape[1]
    uv = jnp.concatenate([peer_u[i].reshape(n_exp, D // 128, 128),
                          peer_v[i].reshape(n_exp, D // 128, 128)], axis=1)
    x2 = _peer_experts(ids_t.T, gates_t.T, h2, x2, uv)
    x2 = _ple_final(x2, p[i].reshape(T, -1), row(g_ple[i]), w_ple_gate[i].astype(BF16),
                    w_ple_up[i].astype(BF16), row(g_final))
    return x2.reshape(B, S, D)
```

```python
import math

import jax
import jax.numpy as jnp
from jax import lax
from jax.experimental import pallas as pl
from jax.experimental.pallas import tpu as pltpu

F32 = jnp.float32
BF16 = jnp.bfloat16

RMS_EPS = 1e-6
HEAD_DIM = 64
N_HEADS = 8
WIDTH = N_HEADS * HEAD_DIM
RET_CHUNK = 128
RET_GN_EPS = 1e-5
ROPE_BASE = 10000.0
RWKV_GN_EPS = 64e-5
L2_EPS = 1e-12
DECAY_LORA = 64
AAA_LORA = 64
GATE_LORA = 128
RWKV_COLS = 3 * WIDTH + DECAY_LORA + AAA_LORA + GATE_LORA
WKV_CHUNK = 64

PEER_HEADS = 8
PEER_N_KEYS = 128
PEER_HALF = 128
PEER_TOPK = 16
PEER_GROUPS = 2 * PEER_HEADS
PEER_SEL = PEER_HEADS * PEER_TOPK
PEER_TOK = 4
PEER_GROUPS_PER_STEP = 4
PEER_AHEAD = 2
PEER_CHUNK = 16

ROW_TILE = 256
VMEM_LIMIT = 48 * 1024 * 1024

_NT = (((1,), (1,)), ((), ()))
_TN = (((0,), (0,)), ((), ()))


def _bdot(a, b):
    return jnp.dot(a.astype(BF16), b.astype(BF16), preferred_element_type=F32)


def _bdot_nt(a, b):
    return lax.dot_general(a.astype(BF16), b.astype(BF16), _NT, preferred_element_type=F32)


def _bdot_tn(a, b):
    return lax.dot_general(a.astype(BF16), b.astype(BF16), _TN, preferred_element_type=F32)


def _split2(a):
    hi = a.astype(BF16)
    lo = (a - hi.astype(F32)).astype(BF16)
    return hi, lo


def _seg_dot(a, m):
    hi, lo = _split2(a)
    return (jnp.dot(hi, m, preferred_element_type=F32)
            + jnp.dot(lo, m, preferred_element_type=F32))


def _rms(x, g):
    return x * lax.rsqrt(jnp.mean(x * x, axis=-1, keepdims=True) + RMS_EPS) * g


def _sigmoid(x):
    return 1.0 / (1.0 + jnp.exp(-x))


def _head_norm(o, mavg, eps):
    mu = _seg_dot(o, mavg)
    oc = o - mu
    var = _seg_dot(oc * oc, mavg)
    return oc * lax.rsqrt(var + eps)


def _cparams(sem, vmem=VMEM_LIMIT):
    return pltpu.CompilerParams(dimension_semantics=sem, vmem_limit_bytes=vmem)


def _full(shape):
    nd = len(shape)
    return pl.BlockSpec(shape, lambda *_: (0,) * nd)


def _in_proj_kernel(x_ref, g_ref, w1_ref, w2_ref, w3_ref, o1_ref, o2_ref, o3_ref):
    h = _rms(x_ref[...], g_ref[...]).astype(BF16)
    o1_ref[...] = jnp.dot(h, w1_ref[...], preferred_element_type=F32)
    o2_ref[...] = jnp.dot(h, w2_ref[...], preferred_element_type=F32)
    o3_ref[...] = jnp.dot(h, w3_ref[...], preferred_element_type=F32)


def _in_proj(x2, g, w_ret, w_rwkv, w_gate):
    T, D = x2.shape
    tm = ROW_TILE
    ws = (w_ret, w_rwkv, w_gate)
    return pl.pallas_call(
        _in_proj_kernel,
        grid=(T // tm,),
        in_specs=[pl.BlockSpec((tm, D), lambda i: (i, 0)), _full((1, D))]
        + [_full(w.shape) for w in ws],
        out_specs=[pl.BlockSpec((tm, w.shape[1]), lambda i: (i, 0)) for w in ws],
        out_shape=[jax.ShapeDtypeStruct((T, w.shape[1]), F32) for w in ws],
        compiler_params=_cparams(("parallel",)),
        name="in_proj",
    )(x2, g, *ws)


def _retention_kernel(z_ref, cos_ref, sin_ref, xi_ref, zeta_ref, decay_ref, cd_ref,
                      gn_ref, mavg_ref, y_ref, state_ref):
    @pl.when(pl.program_id(1) == 0)
    def _():
        state_ref[...] = jnp.zeros_like(state_ref)

    z = z_ref[0]
    q = z[:, 0:WIDTH]
    k = z[:, WIDTH:2 * WIDTH]
    v = z[:, 2 * WIDTH:3 * WIDTH]
    gr = z[:, 3 * WIDTH:4 * WIDTH]
    cos = cos_ref[...]
    sin = sin_ref[...]
    lane = lax.broadcasted_iota(jnp.int32, q.shape, 1)
    first_half = (lane % HEAD_DIM) < (HEAD_DIM // 2)

    def rot(x):
        partner = jnp.where(first_half, pltpu.roll(x, WIDTH - HEAD_DIM // 2, 1),
                            pltpu.roll(x, HEAD_DIM // 2, 1))
        return x * cos + partner * sin

    qr = rot(q)
    kr = rot(k) * (HEAD_DIM ** -0.5)
    qx = qr * xi_ref[...]
    kz = kr * zeta_ref[...]
    outs = []
    for h in range(N_HEADS):
        sl = slice(h * HEAD_DIM, (h + 1) * HEAD_DIM)
        vh = v[:, sl]
        scores = _bdot_nt(qr[:, sl], kr[:, sl]) * decay_ref[h]
        state = state_ref[h]
        outs.append(_bdot(scores, vh) + _bdot(qx[:, sl], state))
        state_ref[h] = state * cd_ref[h] + _bdot_tn(kz[:, sl], vh)
    o = jnp.concatenate(outs, axis=1)
    y = _head_norm(o, mavg_ref[...], RET_GN_EPS)
    y_ref[0] = gr * _sigmoid(gr) * (y * gn_ref[...])


def _retention(z_ret, gn_g, mavg):
    B, S, _ = z_ret.shape
    C = RET_CHUNK
    half = HEAD_DIM // 2
    inv_freq = ROPE_BASE ** (-jnp.arange(half, dtype=F32) * 2.0 / HEAD_DIM)
    ang = jnp.arange(S, dtype=F32)[:, None] * inv_freq[None, :]
    cos_h = jnp.concatenate([jnp.cos(ang), jnp.cos(ang)], axis=1)
    sin_h = jnp.concatenate([-jnp.sin(ang), jnp.sin(ang)], axis=1)
    cos = jnp.tile(cos_h, (1, N_HEADS))
    sin = jnp.tile(sin_h, (1, N_HEADS))
    log_gamma = jnp.log1p(-(2.0 ** (-5.0 - jnp.arange(N_HEADS, dtype=F32))))
    idx = jnp.arange(C, dtype=F32)
    diff = idx[:, None] - idx[None, :]
    causal = diff >= 0
    decay = jnp.where(causal[None], jnp.exp(log_gamma[:, None, None] * jnp.where(causal, diff, 0.0)[None]), 0.0)
    zeta = jnp.exp(log_gamma[:, None] * (C - 1.0 - idx)[None, :])
    xi = jnp.exp(log_gamma[:, None] * (idx + 1.0)[None, :])
    widen = lambda t: jnp.repeat(t.T, HEAD_DIM, axis=1)
    cd = jnp.broadcast_to(jnp.exp(log_gamma * C)[:, None, None], (N_HEADS, HEAD_DIM, HEAD_DIM))
    return pl.pallas_call(
        _retention_kernel,
        grid=(B, S // C),
        in_specs=[pl.BlockSpec((1, C, 4 * WIDTH), lambda b, c: (b, c, 0)),
                  pl.BlockSpec((C, WIDTH), lambda b, c: (c, 0)),
                  pl.BlockSpec((C, WIDTH), lambda b, c: (c, 0)),
                  _full((C, WIDTH)), _full((C, WIDTH)), _full((N_HEADS, C, C)),
                  _full((N_HEADS, HEAD_DIM, HEAD_DIM)), _full((1, WIDTH)), _full((WIDTH, WIDTH))],
        out_specs=pl.BlockSpec((1, C, WIDTH), lambda b, c: (b, c, 0)),
        out_shape=jax.ShapeDtypeStruct((B, S, WIDTH), F32),
        scratch_shapes=[pltpu.VMEM((N_HEADS, HEAD_DIM, HEAD_DIM), F32)],
        compiler_params=_cparams(("parallel", "arbitrary")),
        name="retention",
    )(z_ret, cos, sin, widen(xi), widen(zeta), decay, cd, gn_g, mavg)


def _rwkv_prep_kernel(z_ref, mu_ref, w0_ref, wup_ref, a0_ref, aup_ref, gup_ref, kk_ref,
                      ka_ref, rk_ref, mones_ref,
                      r_out, lw_out, k_out, v_out, kk_out, kka_out, g_out, bonus_out,
                      carry_ref):
    @pl.when(pl.program_id(1) == 0)
    def _():
        carry_ref[...] = jnp.zeros_like(carry_ref)

    z = z_ref[0]
    n = z.shape[0]
    row = lax.broadcasted_iota(jnp.int32, z.shape, 0)
    prev = jnp.where(row == 0, carry_ref[0:1, :], pltpu.roll(z, 1, 0))
    carry_ref[0:1, :] = z[n - 1:n, :]
    zs = z + (prev - z) * mu_ref[...]
    r = zs[:, 0:WIDTH]
    kr = zs[:, WIDTH:2 * WIDTH]
    vr = zs[:, 2 * WIDTH:3 * WIDTH]
    o = 3 * WIDTH
    wl = zs[:, o:o + DECAY_LORA]
    al = zs[:, o + DECAY_LORA:o + DECAY_LORA + AAA_LORA]
    gl = zs[:, o + DECAY_LORA + AAA_LORA:]
    t = -(w0_ref[...] + _bdot(jnp.tanh(wl), wup_ref[...]))
    softplus = jnp.maximum(t, 0.0) + jnp.log1p(jnp.exp(-jnp.abs(t)))
    w_log = -softplus - 0.5
    a = _sigmoid(a0_ref[...] + _bdot(al, aup_ref[...]))
    g = _bdot(_sigmoid(gl), gup_ref[...])
    mones = mones_ref[...]
    kk = kr * kk_ref[...]
    norm = jnp.sqrt(_seg_dot(kk * kk, mones))
    kk = kk / jnp.maximum(norm, L2_EPS)
    k2 = kr * (1.0 + (a - 1.0) * ka_ref[...])
    r_out[0] = r
    lw_out[0] = -jnp.exp(w_log)
    k_out[0] = k2
    v_out[0] = vr
    kk_out[0] = kk
    kka_out[0] = kk * a
    g_out[0] = g
    bonus_out[0] = _seg_dot(r * k2 * rk_ref[...], mones) * vr


def _rwkv_prep(z_rwkv, mu, w0, w_up, a0, a_up, g_up, k_k, k_a, r_k, mones):
    B, S, _ = z_rwkv.shape
    ts = ROW_TILE
    row = lambda t: t.reshape(1, -1)
    args = (row(mu), row(w0), w_up.astype(BF16), row(a0), a_up.astype(BF16),
            g_up.astype(BF16), row(k_k), row(k_a), row(r_k), mones)
    out_spec = pl.BlockSpec((1, ts, WIDTH), lambda b, s: (b, s, 0))
    return pl.pallas_call(
        _rwkv_prep_kernel,
        grid=(B, S // ts),
        in_specs=[pl.BlockSpec((1, ts, RWKV_COLS), lambda b, s: (b, s, 0))]
        + [_full(a.shape) for a in args],
        out_specs=[out_spec] * 8,
        out_shape=[jax.ShapeDtypeStruct((B, S, WIDTH), F32)] * 8,
        scratch_shapes=[pltpu.VMEM((8, RWKV_COLS), F32)],
        compiler_params=_cparams(("parallel", "arbitrary")),
        name="rwkv_prep",
    )(z_rwkv, *args)


def _wkv7_kernel(r_ref, lw_ref, k_ref, v_ref, kk_ref, kka_ref, tri_ref, o_ref, state_ref):
    @pl.when(pl.program_id(1) == 0)
    def _():
        state_ref[...] = jnp.zeros_like(state_ref)

    L = WKV_CHUNK
    lw = lw_ref[0]
    tri = tri_ref[...]
    hi = lw.astype(BF16)
    rem = lw - hi.astype(F32)
    mid = rem.astype(BF16)
    lo = (rem - mid.astype(F32)).astype(BF16)
    cum = (jnp.dot(tri, hi, preferred_element_type=F32)
           + jnp.dot(tri, mid, preferred_element_type=F32)
           + jnp.dot(tri, lo, preferred_element_type=F32))
    cum_last = cum[L - 1:L, :]
    inv_g = jnp.exp(-cum)
    to_end = jnp.exp(cum_last - cum)
    g_last = jnp.exp(cum_last)
    kk = kk_ref[0]
    kka = kka_ref[0]
    k = k_ref[0]
    v = v_ref[0]
    a_t = -kk * jnp.exp(cum - lw)
    b_t = kka * inv_g
    k_t = k * inv_g
    r_t = r_ref[0] * jnp.exp(cum)
    b_end = kka * to_end
    k_end = k * to_end
    ri = lax.broadcasted_iota(jnp.int32, (2 * L, L), 0)
    ci = lax.broadcasted_iota(jnp.int32, (2 * L, L), 1)
    mask = jnp.where(ri < L, ri, ri - L + 1) > ci
    eye = (lax.broadcasted_iota(jnp.int32, (L, L), 0)
           == lax.broadcasted_iota(jnp.int32, (L, L), 1)).astype(F32)
    heads = range(N_HEADS)
    sls = [slice(h * HEAD_DIM, (h + 1) * HEAD_DIM) for h in heads]
    ar = [jnp.concatenate([a_t[:, sl], r_t[:, sl]], axis=0).astype(BF16) for sl in sls]
    vs = [v[:, sl].astype(BF16) for sl in sls]
    abrb = [jnp.where(mask, _bdot_nt(ar[h], b_t[:, sls[h]]), 0.0) for h in heads]
    akrk = [jnp.where(mask, _bdot_nt(ar[h], k_t[:, sls[h]]), 0.0) for h in heads]
    p = [m[:L] for m in abrb]
    inv = [eye + m for m in p]
    for _ in range(int(math.log2(L)) - 1):
        p = [_bdot(m, m) for m in p]
        inv = [inv[h] + _bdot(inv[h], p[h]) for h in heads]
    states = [state_ref[h] for h in heads]
    xs = [_bdot_nt(ar[h], states[h]) for h in heads]
    kv = [_bdot(akrk[h], vs[h]) for h in heads]
    u = [_bdot(inv[h], xs[h][:L] + kv[h][:L]) for h in heads]
    y = [xs[h][L:] + kv[h][L:] + _bdot(abrb[h][L:], u[h]) for h in heads]
    for h in heads:
        uv = jnp.concatenate([u[h], v[:, sls[h]]], axis=0)
        bk_end = jnp.concatenate([b_end[:, sls[h]], k_end[:, sls[h]]], axis=0)
        state_ref[h] = states[h] * g_last[:, sls[h]] + _bdot_tn(uv, bk_end)
    o_ref[0] = jnp.concatenate(y, axis=1)


def _wkv7(r, lw, k, v, kk, kka):
    B, S, _ = r.shape
    L = WKV_CHUNK
    tri = jnp.tril(jnp.ones((L, L), F32)).astype(BF16)
    spec = pl.BlockSpec((1, L, WIDTH), lambda b, c: (b, c, 0))
    return pl.pallas_call(
        _wkv7_kernel,
        grid=(B, S // L),
        in_specs=[spec] * 6 + [_full((L, L))],
        out_specs=spec,
        out_shape=jax.ShapeDtypeStruct((B, S, WIDTH), F32),
        scratch_shapes=[pltpu.VMEM((N_HEADS, HEAD_DIM, HEAD_DIM), F32)],
        compiler_params=_cparams(("parallel", "arbitrary")),
        name="wkv7",
    )(r, lw, k, v, kk, kka, tri)


def _merge_kernel(x_ref, yret_ref, o_ref, bonus_ref, g_ref, zg_ref, gng_ref, gnb_ref,
                  mavg_ref, wret_ref, wrwkv_ref, wo_ref, x1_ref):
    y = _head_norm(o_ref[...], mavg_ref[...], RWKV_GN_EPS)
    y_rwkv = (y * gng_ref[...] + gnb_ref[...] + bonus_ref[...]) * g_ref[...]
    br = _bdot(yret_ref[...], wret_ref[...])
    bw = _bdot(y_rwkv, wrwkv_ref[...])
    zg = zg_ref[...]
    d = br.shape[1]
    merged = _sigmoid(zg[:, :d]) * br + _sigmoid(zg[:, d:]) * bw
    x1_ref[...] = x_ref[...] + _bdot(merged, wo_ref[...])


def _merge(x2, y_ret, o, bonus, g, z_gate, gn_g, gn_b, mavg, w_ret_br, w_rwkv_br, w_o):
    T, D = x2.shape
    tm = ROW_TILE
    rows = lambda n: pl.BlockSpec((tm, n), lambda i: (i, 0))
    consts = (gn_g, gn_b, mavg, w_ret_br, w_rwkv_br, w_o)
    return pl.pallas_call(
        _merge_kernel,
        grid=(T // tm,),
        in_specs=[rows(D), rows(WIDTH), rows(WIDTH), rows(WIDTH), rows(WIDTH), rows(2 * D)]
        + [_full(c.shape) for c in consts],
        out_specs=rows(D),
        out_shape=jax.ShapeDtypeStruct((T, D), F32),
        compiler_params=_cparams(("parallel",)),
        name="merge",
    )(x2, y_ret, o, bonus, g, z_gate, *consts)


def _peer_scores_kernel(x_ref, g_ref, wq_ref, keys_ref, h_ref, s_ref):
    h = _rms(x_ref[...], g_ref[...])
    h_ref[...] = h
    q = _bdot(h, wq_ref[...]).astype(BF16)
    for grp in range(PEER_GROUPS):
        s_ref[grp] = lax.dot_general(keys_ref[grp], q[:, grp * PEER_HALF:(grp + 1) * PEER_HALF],
                                     _NT, preferred_element_type=F32)


def _peer_scores(x1, g_ffn, w_pq, keys):
    T, D = x1.shape
    tm = ROW_TILE
    return pl.pallas_call(
        _peer_scores_kernel,
        grid=(T // tm,),
        in_specs=[pl.BlockSpec((tm, D), lambda i: (i, 0)), _full((1, D)), _full(w_pq.shape),
                  _full(keys.shape)],
        out_specs=[pl.BlockSpec((tm, D), lambda i: (i, 0)),
                   pl.BlockSpec((PEER_GROUPS, PEER_N_KEYS, tm), lambda i: (0, 0, i))],
        out_shape=[jax.ShapeDtypeStruct((T, D), F32),
                   jax.ShapeDtypeStruct((PEER_GROUPS, PEER_N_KEYS, T), F32)],
        compiler_params=_cparams(("parallel",)),
        name="peer_scores",
    )(x1, g_ffn, w_pq, keys)


def _top_rows(s, count, payload=None):
    rows = lax.broadcasted_iota(jnp.int32, s.shape, 0).astype(F32)
    vals, picks = [], []
    for _ in range(count):
        m = jnp.max(s, axis=0, keepdims=True)
        idx = jnp.min(jnp.where(s == m, rows, float(s.shape[0])), axis=0, keepdims=True)
        hit = rows == idx
        s = jnp.where(hit, -jnp.inf, s)
        vals.append(m)
        if payload is None:
            picks.append(idx)
        else:
            picks.append(jnp.max(jnp.where(hit, payload, -1), axis=0, keepdims=True))
    return jnp.concatenate(vals, axis=0), jnp.concatenate(picks, axis=0)


def _pair_rows(a, b):
    K = PEER_TOPK
    out = []
    for i in range(K // 2):
        jn = K if i == 0 else K // 2
        out.append((a[i:i + 1, :], b[0:jn, :]))
    out.append((a[K // 2:K, :], b[0:1, :]))
    return out


def _peer_topk_kernel(s_ref, ids_ref, gates_ref):
    K = PEER_TOPK

    def head(h, carry):
        s0, i0 = _top_rows(s_ref[2 * h], K)
        s1, i1 = _top_rows(s_ref[2 * h + 1], K)
        e0 = i0.astype(jnp.int32) * PEER_N_KEYS
        e1 = i1.astype(jnp.int32)
        cand = jnp.concatenate([x + y for x, y in _pair_rows(s0, s1)], axis=0)
        cand_id = jnp.concatenate([x + y for x, y in _pair_rows(e0, e1)], axis=0)
        best, ids = _top_rows(cand, K, payload=cand_id)
        e = jnp.exp(best - best[0:1, :])
        gates = e / jnp.sum(e, axis=0, keepdims=True)
        off = pl.multiple_of(h * K, K)
        ids_ref[pl.ds(off, K), :] = ids
        gates_ref[pl.ds(off, K), :] = gates
        return carry

    lax.fori_loop(0, PEER_HEADS, head, 0)


def _peer_topk(scores_t):
    _, _, T = scores_t.shape
    tk = 128
    return pl.pallas_call(
        _peer_topk_kernel,
        grid=(T // tk,),
        in_specs=[pl.BlockSpec((PEER_GROUPS, PEER_N_KEYS, tk), lambda i: (0, 0, i))],
        out_specs=[pl.BlockSpec((PEER_SEL, tk), lambda i: (0, i))] * 2,
        out_shape=[jax.ShapeDtypeStruct((PEER_SEL, T), jnp.int32),
                   jax.ShapeDtypeStruct((PEER_SEL, T), F32)],
        compiler_params=_cparams(("parallel",)),
        name="peer_topk",
    )(scores_t)


def _gelu_tanh(x):
    return 0.5 * x * (1.0 + jnp.tanh(math.sqrt(2.0 / math.pi) * (x + 0.044715 * x * x * x)))


def _peer_experts_kernel(ids_cur, ids_next, gexp_ref, h_ref, x_ref, uv_hbm, out_ref, buf, sem):
    i = pl.program_id(0)
    n = pl.num_programs(0)
    sub = h_ref.shape[1]
    group_rows = PEER_TOK * PEER_SEL

    def issue_token(ids_ref, tok, group, t, k0=0, count=PEER_SEL):
        for k in range(k0, k0 + count):
            e = ids_ref[0, tok, k]
            pltpu.make_async_copy(uv_hbm.at[pl.ds(e, 1)],
                                  buf.at[group, pl.ds(t * PEER_SEL + k, 1)],
                                  sem.at[group]).start(priority=k % 2)

    def wait_group(group):
        pltpu.make_async_copy(uv_hbm.at[pl.ds(0, group_rows)], buf.at[group], sem.at[group]).wait()

    @pl.when(i == 0)
    def _():
        for group in range(PEER_AHEAD):
            def body(t, carry, group=group):
                issue_token(ids_cur, group * PEER_TOK + t, group, t)
                return carry
            lax.fori_loop(0, PEER_TOK, body, 0)

    ch = PEER_CHUNK
    n_ch = PEER_SEL // ch
    ccols = ch * sub
    lane = lax.broadcasted_iota(jnp.int32, (sub, ccols), 1)
    diag = (lane % sub) == lax.broadcasted_iota(jnp.int32, (sub, ccols), 0)

    def group_sum(x):
        step = 1
        while step < sub:
            partner = jnp.where((lane % (2 * step)) < step, pltpu.roll(x, ccols - step, 1),
                                pltpu.roll(x, step, 1))
            x = x + partner
            step *= 2
        return x

    def compute_group(group):
        ahead = group + PEER_AHEAD
        next_ids_ref = ids_cur if ahead < PEER_GROUPS_PER_STEP else ids_next
        next_group = ahead % PEER_GROUPS_PER_STEP
        burst = PEER_SEL // (2 * n_ch)

        def request(t, piece):
            issue_token(next_ids_ref, next_group * PEER_TOK + t, next_group, t, piece * burst, burst)

        def first_layer(t):
            hb = h_ref[group * PEER_TOK + t].astype(BF16)
            ps = []
            for c in range(n_ch):
                request(t, c)
                rows = pl.ds(t * PEER_SEL + c * ch, ch)
                uc = buf[group, rows, 0:sub, :].reshape(ccols, 128).astype(BF16)
                ps.append(lax.dot_general(hb, uc, _NT, preferred_element_type=F32))
            return ps

        def weights(t, ps):
            tok = group * PEER_TOK + t
            ws = []
            for c in range(n_ch):
                act = jnp.sum(group_sum(jnp.where(diag, ps[c], 0.0)), axis=0, keepdims=True)
                w = _gelu_tanh(act) * gexp_ref[tok:tok + 1, c * ccols:(c + 1) * ccols]
                ws.append(jnp.where(diag, jnp.broadcast_to(w, (sub, ccols)), 0.0).astype(BF16))
            return ws

        def second_layer(t, ws):
            o = x_ref[group * PEER_TOK + t]
            for c in range(n_ch):
                request(t, n_ch + c)
                rows = pl.ds(t * PEER_SEL + c * ch, ch)
                vc = buf[group, rows, sub:2 * sub, :].reshape(ccols, 128).astype(BF16)
                o = o + jnp.dot(ws[c], vc, preferred_element_type=F32)
            return o

        wait_group(group)
        outs = []
        ps = first_layer(0)
        for t in range(PEER_TOK):
            ps_next = first_layer(t + 1) if t + 1 < PEER_TOK else None
            outs.append((group * PEER_TOK + t, second_layer(t, weights(t, ps))))
            ps = ps_next
        return outs

    outs = []
    for group in range(PEER_GROUPS_PER_STEP):
        outs += compute_group(group)
    for tok, o in outs:
        out_ref[tok] = o

    @pl.when(i == n - 1)
    def _():
        for group in range(PEER_AHEAD):
            wait_group(group)


def _peer_experts(ids, gates, h2, x1, uv):
    T, D = x1.shape
    tb = PEER_GROUPS_PER_STEP * PEER_TOK
    nb = T // tb
    sub = D // 128
    ids3 = ids.reshape(nb, tb, PEER_SEL)
    gexp = jnp.repeat(gates, sub, axis=1)
    smem = lambda imap: pl.BlockSpec((1, tb, PEER_SEL), imap, memory_space=pltpu.SMEM)
    tiles = pl.BlockSpec((tb, sub, 128), lambda i: (i, 0, 0))
    out = pl.pallas_call(
        _peer_experts_kernel,
        grid=(nb,),
        in_specs=[smem(lambda i: (i, 0, 0)),
                  smem(lambda i: (jnp.minimum(i + 1, nb - 1), 0, 0)),
                  pl.BlockSpec((tb, PEER_SEL * sub), lambda i: (i, 0)),
                  tiles, tiles,
                  pl.BlockSpec(memory_space=pl.ANY)],
        out_specs=tiles,
        out_shape=jax.ShapeDtypeStruct((T, sub, 128), F32),
        scratch_shapes=[pltpu.VMEM((PEER_GROUPS_PER_STEP, PEER_TOK * PEER_SEL, 2 * sub, 128), F32),
                        pltpu.SemaphoreType.DMA((PEER_GROUPS_PER_STEP,))],
        compiler_params=_cparams(("arbitrary",)),
        name="peer_experts",
    )(ids3, ids3, gexp, h2.reshape(T, sub, 128), x1.reshape(T, sub, 128), uv)
    return out.reshape(T, D)


def _ple_final_kernel(x_ref, p_ref, gple_ref, wg_ref, wu_ref, gfin_ref, out_ref):
    x = x_ref[...]
    gate = _sigmoid(_bdot(_rms(x, gple_ref[...]), wg_ref[...]))
    x = x + gate * _bdot(p_ref[...], wu_ref[...])
    out_ref[...] = _rms(x, gfin_ref[...])


def _ple_final(x2, p2, g_ple, w_gate, w_up, g_final):
    T, D = x2.shape
    tm = ROW_TILE
    consts = (g_ple, w_gate, w_up, g_final)
    return pl.pallas_call(
        _ple_final_kernel,
        grid=(T // tm,),
        in_specs=[pl.BlockSpec((tm, D), lambda i: (i, 0)),
                  pl.BlockSpec((tm, p2.shape[1]), lambda i: (i, 0)),
                  _full(g_ple.shape), _full(w_gate.shape), _full(w_up.shape), _full(g_final.shape)],
        out_specs=pl.BlockSpec((tm, D), lambda i: (i, 0)),
        out_shape=jax.ShapeDtypeStruct((T, D), F32),
        compiler_params=_cparams(("parallel",)),
        name="ple_final",
    )(x2, p2, *consts)


def kernel(x, p, g_mix, w_in, ret_gn_g, rwkv_mu, rwkv_w0, rwkv_w_up, rwkv_a0, rwkv_a_up, rwkv_g_up, rwkv_k_k, rwkv_k_a, rwkv_r_k, rwkv_gn_g, rwkv_gn_b, w_ret_br, w_rwkv_br, w_o, g_ffn, w_pq, peer_sub_keys, peer_u, peer_v, g_ple, w_ple_gate, w_ple_up, g_final):
    B, S, D = x.shape
    T = B * S
    assert w_in.shape[0] == 1, "single-layer block: the final RMSNorm is fused into its last step"
    i = 0
    row = lambda t: t.reshape(1, -1)
    head_of = jnp.arange(WIDTH) // HEAD_DIM
    same_head = head_of[:, None] == head_of[None, :]
    mones = same_head.astype(BF16)
    mavg = (same_head.astype(F32) / HEAD_DIM).astype(BF16)
    ret_cols = 4 * WIDTH
    x2 = x.reshape(T, D)
    wi = w_in[i].astype(BF16)
    z_ret, z_rwkv, z_gate = _in_proj(
        x2, row(g_mix[i]), wi[:, :ret_cols], wi[:, ret_cols:ret_cols + RWKV_COLS],
        wi[:, ret_cols + RWKV_COLS:])
    y_ret = _retention(z_ret.reshape(B, S, ret_cols), row(ret_gn_g[i]), mavg)
    r, lw, k, v, kk, kka, g, bonus = _rwkv_prep(
        z_rwkv.reshape(B, S, RWKV_COLS), rwkv_mu[i], rwkv_w0[i], rwkv_w_up[i], rwkv_a0[i],
        rwkv_a_up[i], rwkv_g_up[i], rwkv_k_k[i], rwkv_k_a[i], rwkv_r_k[i], mones)
    o = _wkv7(r, lw, k, v, kk, kka)
    flat = lambda t: t.reshape(T, WIDTH)
    x2 = _merge(x2, flat(y_ret), flat(o), flat(bonus), flat(g), z_gate,
                row(rwkv_gn_g[i]), row(rwkv_gn_b[i]), mavg, w_ret_br[i].astype(BF16),
                w_rwkv_br[i].astype(BF16), w_o[i].astype(BF16))
    keys = peer_sub_keys[i].reshape(PEER_GROUPS, PEER_N_KEYS, PEER_HALF).astype(BF16)
    h2, scores_t = _peer_scores(x2, row(g_ffn[i]), w_pq[i].astype(BF16), keys)
    ids_t, gates_t = _peer_topk(scores_t)
    n_exp = peer_u.shape[1]
    uv = jnp.concatenate([peer_u[i].reshape(n_exp, D // 128, 128),
                          peer_v[i].reshape(n_exp, D // 128, 128)], axis=1)
    x2 = _peer_experts(ids_t.T, gates_t.T, h2, x2, uv)
    x2 = _ple_final(x2, p[i].reshape(T, -1), row(g_ple[i]), w_ple_gate[i].astype(BF16),
                    w_ple_up[i].astype(BF16), row(g_final))
    return x2.reshape(B, S, D)
```

```python
import math

import jax
import jax.numpy as jnp
from jax import lax
from jax.experimental import pallas as pl
from jax.experimental.pallas import tpu as pltpu

F32 = jnp.float32
BF16 = jnp.bfloat16

RMS_EPS = 1e-6
HEAD_DIM = 64
N_HEADS = 8
WIDTH = N_HEADS * HEAD_DIM
RET_CHUNK = 128
RET_BATCH = 2
RET_GN_EPS = 1e-5
ROPE_BASE = 10000.0
RWKV_GN_EPS = 64e-5
L2_EPS = 1e-12
DECAY_LORA = 64
AAA_LORA = 64
GATE_LORA = 128
RWKV_COLS = 3 * WIDTH + DECAY_LORA + AAA_LORA + GATE_LORA
WKV_CHUNK = 64
WKV_BATCH = 4

PEER_HEADS = 8
PEER_N_KEYS = 128
PEER_HALF = 128
PEER_TOPK = 16
PEER_GROUPS = 2 * PEER_HEADS
PEER_SEL = PEER_HEADS * PEER_TOPK
PEER_TOK = 4
PEER_GROUPS_PER_STEP = 4
PEER_AHEAD = 2
PEER_SKEW = 2
PEER_CHUNK = 16

ROW_TILE = 256
VMEM_LIMIT = 48 * 1024 * 1024

_NT = (((1,), (1,)), ((), ()))
_TN = (((0,), (0,)), ((), ()))


def _bdot(a, b):
    return jnp.dot(a.astype(BF16), b.astype(BF16), preferred_element_type=F32)


def _bdot_nt(a, b):
    return lax.dot_general(a.astype(BF16), b.astype(BF16), _NT, preferred_element_type=F32)


def _bdot_tn(a, b):
    return lax.dot_general(a.astype(BF16), b.astype(BF16), _TN, preferred_element_type=F32)


def _split2(a):
    hi = a.astype(BF16)
    lo = (a - hi.astype(F32)).astype(BF16)
    return hi, lo


def _seg_dot(a, m):
    hi, lo = _split2(a)
    return (jnp.dot(hi, m, preferred_element_type=F32)
            + jnp.dot(lo, m, preferred_element_type=F32))


def _rms(x, g):
    return x * lax.rsqrt(jnp.mean(x * x, axis=-1, keepdims=True) + RMS_EPS) * g


def _sigmoid(x):
    return 1.0 / (1.0 + jnp.exp(-x))


def _head_norm(o, mavg, eps):
    mu = _seg_dot(o, mavg)
    oc = o - mu
    var = _seg_dot(oc * oc, mavg)
    return oc * lax.rsqrt(var + eps)


def _cparams(sem, vmem=VMEM_LIMIT):
    return pltpu.CompilerParams(dimension_semantics=sem, vmem_limit_bytes=vmem)


def _full(shape):
    nd = len(shape)
    return pl.BlockSpec(shape, lambda *_: (0,) * nd)


def _in_proj_kernel(x_ref, g_ref, w1_ref, w2_ref, w3_ref, o1_ref, o2_ref, o3_ref):
    h = _rms(x_ref[...], g_ref[...]).astype(BF16)
    o1_ref[...] = jnp.dot(h, w1_ref[...], preferred_element_type=F32)
    o2_ref[...] = jnp.dot(h, w2_ref[...], preferred_element_type=F32)
    o3_ref[...] = jnp.dot(h, w3_ref[...], preferred_element_type=F32)


def _in_proj(x2, g, w_ret, w_rwkv, w_gate):
    T, D = x2.shape
    tm = ROW_TILE
    ws = (w_ret, w_rwkv, w_gate)
    return pl.pallas_call(
        _in_proj_kernel,
        grid=(T // tm,),
        in_specs=[pl.BlockSpec((tm, D), lambda i: (i, 0)), _full((1, D))]
        + [_full(w.shape) for w in ws],
        out_specs=[pl.BlockSpec((tm, w.shape[1]), lambda i: (i, 0)) for w in ws],
        out_shape=[jax.ShapeDtypeStruct((T, w.shape[1]), F32) for w in ws],
        compiler_params=_cparams(("parallel",)),
        name="in_proj",
    )(x2, g, *ws)


def _retention_kernel(z_ref, cos_ref, sin_ref, xi_ref, zeta_ref, decay_ref, cd_ref,
                      gn_ref, mavg_ref, y_ref, state_ref):
    @pl.when(pl.program_id(1) == 0)
    def _():
        state_ref[...] = jnp.zeros_like(state_ref)

    cos = cos_ref[...]
    sin = sin_ref[...]
    lane = lax.broadcasted_iota(jnp.int32, cos.shape, 1)
    first_half = (lane % HEAD_DIM) < (HEAD_DIM // 2)

    def rot(x):
        partner = jnp.where(first_half, pltpu.roll(x, WIDTH - HEAD_DIM // 2, 1),
                            pltpu.roll(x, HEAD_DIM // 2, 1))
        return x * cos + partner * sin

    qs, ks, vs, qxs, kzs, grs = [], [], [], [], [], []
    for b in range(RET_BATCH):
        z = z_ref[b]
        v = z[:, 2 * WIDTH:3 * WIDTH]
        grs.append(z[:, 3 * WIDTH:4 * WIDTH])
        qr = rot(z[:, 0:WIDTH])
        kr = rot(z[:, WIDTH:2 * WIDTH]) * (HEAD_DIM ** -0.5)
        qx = qr * xi_ref[...]
        kz = kr * zeta_ref[...]
        for h in range(N_HEADS):
            sl = slice(h * HEAD_DIM, (h + 1) * HEAD_DIM)
            qs.append(qr[:, sl].astype(BF16))
            ks.append(kr[:, sl].astype(BF16))
            vs.append(v[:, sl].astype(BF16))
            qxs.append(qx[:, sl])
            kzs.append(kz[:, sl])
    chains = range(RET_BATCH * N_HEADS)
    scores = [_bdot_nt(qs[c], ks[c]) * decay_ref[c % N_HEADS] for c in chains]
    states = [state_ref[c] for c in chains]
    cross = [_bdot(qxs[c], states[c]) for c in chains]
    outs = [_bdot(scores[c], vs[c]) + cross[c] for c in chains]
    for c in chains:
        state_ref[c] = states[c] * cd_ref[c % N_HEADS] + _bdot_tn(kzs[c], vs[c])
    for b in range(RET_BATCH):
        o = jnp.concatenate(outs[b * N_HEADS:(b + 1) * N_HEADS], axis=1)
        y = _head_norm(o, mavg_ref[...], RET_GN_EPS)
        y_ref[b] = grs[b] * _sigmoid(grs[b]) * (y * gn_ref[...])


def _retention(z_ret, gn_g, mavg):
    B, S, _ = z_ret.shape
    C = RET_CHUNK
    half = HEAD_DIM // 2
    inv_freq = ROPE_BASE ** (-jnp.arange(half, dtype=F32) * 2.0 / HEAD_DIM)
    ang = jnp.arange(S, dtype=F32)[:, None] * inv_freq[None, :]
    cos_h = jnp.concatenate([jnp.cos(ang), jnp.cos(ang)], axis=1)
    sin_h = jnp.concatenate([-jnp.sin(ang), jnp.sin(ang)], axis=1)
    cos = jnp.tile(cos_h, (1, N_HEADS))
    sin = jnp.tile(sin_h, (1, N_HEADS))
    log_gamma = jnp.log1p(-(2.0 ** (-5.0 - jnp.arange(N_HEADS, dtype=F32))))
    idx = jnp.arange(C, dtype=F32)
    diff = idx[:, None] - idx[None, :]
    causal = diff >= 0
    decay = jnp.where(causal[None], jnp.exp(log_gamma[:, None, None] * jnp.where(causal, diff, 0.0)[None]), 0.0)
    zeta = jnp.exp(log_gamma[:, None] * (C - 1.0 - idx)[None, :])
    xi = jnp.exp(log_gamma[:, None] * (idx + 1.0)[None, :])
    widen = lambda t: jnp.repeat(t.T, HEAD_DIM, axis=1)
    cd = jnp.broadcast_to(jnp.exp(log_gamma * C)[:, None, None], (N_HEADS, HEAD_DIM, HEAD_DIM))
    return pl.pallas_call(
        _retention_kernel,
        grid=(B // RET_BATCH, S // C),
        in_specs=[pl.BlockSpec((RET_BATCH, C, 4 * WIDTH), lambda b, c: (b, c, 0)),
                  pl.BlockSpec((C, WIDTH), lambda b, c: (c, 0)),
                  pl.BlockSpec((C, WIDTH), lambda b, c: (c, 0)),
                  _full((C, WIDTH)), _full((C, WIDTH)), _full((N_HEADS, C, C)),
                  _full((N_HEADS, HEAD_DIM, HEAD_DIM)), _full((1, WIDTH)), _full((WIDTH, WIDTH))],
        out_specs=pl.BlockSpec((RET_BATCH, C, WIDTH), lambda b, c: (b, c, 0)),
        out_shape=jax.ShapeDtypeStruct((B, S, WIDTH), F32),
        scratch_shapes=[pltpu.VMEM((RET_BATCH * N_HEADS, HEAD_DIM, HEAD_DIM), F32)],
        compiler_params=_cparams(("parallel", "arbitrary")),
        name="retention",
    )(z_ret, cos, sin, widen(xi), widen(zeta), decay, cd, gn_g, mavg)


def _rwkv_prep_kernel(z_ref, mu_ref, w0_ref, wup_ref, a0_ref, aup_ref, gup_ref, kk_ref,
                      ka_ref, rk_ref, mones_ref,
                      r_out, lw_out, k_out, v_out, kk_out, kka_out, g_out, bonus_out,
                      carry_ref):
    @pl.when(pl.program_id(1) == 0)
    def _():
        carry_ref[...] = jnp.zeros_like(carry_ref)

    z = z_ref[0]
    n = z.shape[0]
    row = lax.broadcasted_iota(jnp.int32, z.shape, 0)
    prev = jnp.where(row == 0, carry_ref[0:1, :], pltpu.roll(z, 1, 0))
    carry_ref[0:1, :] = z[n - 1:n, :]
    zs = z + (prev - z) * mu_ref[...]
    r = zs[:, 0:WIDTH]
    kr = zs[:, WIDTH:2 * WIDTH]
    vr = zs[:, 2 * WIDTH:3 * WIDTH]
    o = 3 * WIDTH
    wl = zs[:, o:o + DECAY_LORA]
    al = zs[:, o + DECAY_LORA:o + DECAY_LORA + AAA_LORA]
    gl = zs[:, o + DECAY_LORA + AAA_LORA:]
    t = -(w0_ref[...] + _bdot(jnp.tanh(wl), wup_ref[...]))
    softplus = jnp.maximum(t, 0.0) + jnp.log1p(jnp.exp(-jnp.abs(t)))
    w_log = -softplus - 0.5
    a = _sigmoid(a0_ref[...] + _bdot(al, aup_ref[...]))
    g = _bdot(_sigmoid(gl), gup_ref[...])
    mones = mones_ref[...]
    kk = kr * kk_ref[...]
    norm = jnp.sqrt(_seg_dot(kk * kk, mones))
    kk = kk / jnp.maximum(norm, L2_EPS)
    k2 = kr * (1.0 + (a - 1.0) * ka_ref[...])
    r_out[0] = r
    lw_out[0] = -jnp.exp(w_log)
    k_out[0] = k2
    v_out[0] = vr
    kk_out[0] = kk
    kka_out[0] = kk * a
    g_out[0] = g
    bonus_out[0] = _seg_dot(r * k2 * rk_ref[...], mones) * vr


def _rwkv_prep(z_rwkv, mu, w0, w_up, a0, a_up, g_up, k_k, k_a, r_k, mones):
    B, S, _ = z_rwkv.shape
    ts = ROW_TILE
    row = lambda t: t.reshape(1, -1)
    args = (row(mu), row(w0), w_up.astype(BF16), row(a0), a_up.astype(BF16),
            g_up.astype(BF16), row(k_k), row(k_a), row(r_k), mones)
    out_spec = pl.BlockSpec((1, ts, WIDTH), lambda b, s: (b, s, 0))
    return pl.pallas_call(
        _rwkv_prep_kernel,
        grid=(B, S // ts),
        in_specs=[pl.BlockSpec((1, ts, RWKV_COLS), lambda b, s: (b, s, 0))]
        + [_full(a.shape) for a in args],
        out_specs=[out_spec] * 8,
        out_shape=[jax.ShapeDtypeStruct((B, S, WIDTH), F32)] * 8,
        scratch_shapes=[pltpu.VMEM((8, RWKV_COLS), F32)],
        compiler_params=_cparams(("parallel", "arbitrary")),
        name="rwkv_prep",
    )(z_rwkv, *args)


def _wkv7_kernel(r_ref, lw_ref, k_ref, v_ref, kk_ref, kka_ref, tri_ref, o_ref, state_ref):
    @pl.when(pl.program_id(1) == 0)
    def _():
        state_ref[...] = jnp.zeros_like(state_ref)

    L = WKV_CHUNK
    tri = tri_ref[...]
    ri = lax.broadcasted_iota(jnp.int32, (2 * L, L), 0)
    ci = lax.broadcasted_iota(jnp.int32, (2 * L, L), 1)
    mask = jnp.where(ri < L, ri, ri - L + 1) > ci
    eye = (lax.broadcasted_iota(jnp.int32, (L, L), 0)
           == lax.broadcasted_iota(jnp.int32, (L, L), 1)).astype(F32)

    ar, vf, bt, kt, bke, gl = [], [], [], [], [], []
    for b in range(WKV_BATCH):
        lw = lw_ref[b]
        hi = lw.astype(BF16)
        rem = lw - hi.astype(F32)
        mid = rem.astype(BF16)
        lo = (rem - mid.astype(F32)).astype(BF16)
        cum = (jnp.dot(tri, hi, preferred_element_type=F32)
               + jnp.dot(tri, mid, preferred_element_type=F32)
               + jnp.dot(tri, lo, preferred_element_type=F32))
        cum_last = cum[L - 1:L, :]
        inv_g = jnp.exp(-cum)
        to_end = jnp.exp(cum_last - cum)
        g_last = jnp.exp(cum_last)
        kk = kk_ref[b]
        kka = kka_ref[b]
        k = k_ref[b]
        v = v_ref[b]
        a_t = -kk * jnp.exp(cum - lw)
        b_t = kka * inv_g
        k_t = k * inv_g
        r_t = r_ref[b] * jnp.exp(cum)
        b_end = kka * to_end
        k_end = k * to_end
        for h in range(N_HEADS):
            sl = slice(h * HEAD_DIM, (h + 1) * HEAD_DIM)
            ar.append(jnp.concatenate([a_t[:, sl], r_t[:, sl]], axis=0).astype(BF16))
            vf.append(v[:, sl])
            bt.append(b_t[:, sl])
            kt.append(k_t[:, sl])
            bke.append(jnp.concatenate([b_end[:, sl], k_end[:, sl]], axis=0))
            gl.append(g_last[:, sl])
    chains = range(WKV_BATCH * N_HEADS)
    vs = [x.astype(BF16) for x in vf]
    abrb = [jnp.where(mask, _bdot_nt(ar[c], bt[c]), 0.0) for c in chains]
    akrk = [jnp.where(mask, _bdot_nt(ar[c], kt[c]), 0.0) for c in chains]
    p = [m[:L] for m in abrb]
    inv = [eye + m for m in p]
    for _ in range(int(math.log2(L)) - 1):
        p = [_bdot(m, m) for m in p]
        inv = [inv[c] + _bdot(inv[c], p[c]) for c in chains]
    states = [state_ref[c] for c in chains]
    xs = [_bdot_nt(ar[c], states[c]) for c in chains]
    kv = [_bdot(akrk[c], vs[c]) for c in chains]
    u = [_bdot(inv[c], xs[c][:L] + kv[c][:L]) for c in chains]
    y = [xs[c][L:] + kv[c][L:] + _bdot(abrb[c][L:], u[c]) for c in chains]
    for c in chains:
        uv = jnp.concatenate([u[c], vf[c]], axis=0)
        state_ref[c] = states[c] * gl[c] + _bdot_tn(uv, bke[c])
    for b in range(WKV_BATCH):
        o_ref[b] = jnp.concatenate(y[b * N_HEADS:(b + 1) * N_HEADS], axis=1)


def _wkv7(r, lw, k, v, kk, kka):
    B, S, _ = r.shape
    L = WKV_CHUNK
    nb = WKV_BATCH
    tri = jnp.tril(jnp.ones((L, L), F32)).astype(BF16)
    spec = pl.BlockSpec((nb, L, WIDTH), lambda b, c: (b, c, 0))
    return pl.pallas_call(
        _wkv7_kernel,
        grid=(B // nb, S // L),
        in_specs=[spec] * 6 + [_full((L, L))],
        out_specs=spec,
        out_shape=jax.ShapeDtypeStruct((B, S, WIDTH), F32),
        scratch_shapes=[pltpu.VMEM((nb * N_HEADS, HEAD_DIM, HEAD_DIM), F32)],
        compiler_params=_cparams(("parallel", "arbitrary")),
        name="wkv7",
    )(r, lw, k, v, kk, kka, tri)


def _merge_kernel(x_ref, yret_ref, o_ref, bonus_ref, g_ref, zg_ref, gng_ref, gnb_ref,
                  mavg_ref, wret_ref, wrwkv_ref, wo_ref, x1_ref):
    y = _head_norm(o_ref[...], mavg_ref[...], RWKV_GN_EPS)
    y_rwkv = (y * gng_ref[...] + gnb_ref[...] + bonus_ref[...]) * g_ref[...]
    br = _bdot(yret_ref[...], wret_ref[...])
    bw = _bdot(y_rwkv, wrwkv_ref[...])
    zg = zg_ref[...]
    d = br.shape[1]
    merged = _sigmoid(zg[:, :d]) * br + _sigmoid(zg[:, d:]) * bw
    x1_ref[...] = x_ref[...] + _bdot(merged, wo_ref[...])


def _merge(x2, y_ret, o, bonus, g, z_gate, gn_g, gn_b, mavg, w_ret_br, w_rwkv_br, w_o):
    T, D = x2.shape
    tm = ROW_TILE
    rows = lambda n: pl.BlockSpec((tm, n), lambda i: (i, 0))
    consts = (gn_g, gn_b, mavg, w_ret_br, w_rwkv_br, w_o)
    return pl.pallas_call(
        _merge_kernel,
        grid=(T // tm,),
        in_specs=[rows(D), rows(WIDTH), rows(WIDTH), rows(WIDTH), rows(WIDTH), rows(2 * D)]
        + [_full(c.shape) for c in consts],
        out_specs=rows(D),
        out_shape=jax.ShapeDtypeStruct((T, D), F32),
        compiler_params=_cparams(("parallel",)),
        name="merge",
    )(x2, y_ret, o, bonus, g, z_gate, *consts)


def _peer_scores_kernel(x_ref, g_ref, wq_ref, keys_ref, h_ref, s_ref):
    h = _rms(x_ref[...], g_ref[...])
    h_ref[...] = h.reshape(h_ref.shape)
    q = _bdot(h, wq_ref[...]).astype(BF16)
    for grp in range(PEER_GROUPS):
        s_ref[grp] = lax.dot_general(keys_ref[grp], q[:, grp * PEER_HALF:(grp + 1) * PEER_HALF],
                                     _NT, preferred_element_type=F32)


def _peer_scores(x1, g_ffn, w_pq, keys):
    T, D = x1.shape
    tm = ROW_TILE
    return pl.pallas_call(
        _peer_scores_kernel,
        grid=(T // tm,),
        in_specs=[pl.BlockSpec((tm, D), lambda i: (i, 0)), _full((1, D)), _full(w_pq.shape),
                  _full(keys.shape)],
        out_specs=[pl.BlockSpec((tm, D // 128, 128), lambda i: (i, 0, 0)),
                   pl.BlockSpec((PEER_GROUPS, PEER_N_KEYS, tm), lambda i: (0, 0, i))],
        out_shape=[jax.ShapeDtypeStruct((T, D // 128, 128), F32),
                   jax.ShapeDtypeStruct((PEER_GROUPS, PEER_N_KEYS, T), F32)],
        compiler_params=_cparams(("parallel",)),
        name="peer_scores",
    )(x1, g_ffn, w_pq, keys)


def _top_rows(s, count, payload=None):
    rows = lax.broadcasted_iota(jnp.int32, s.shape, 0).astype(F32)
    vals, picks = [], []
    for _ in range(count):
        m = jnp.max(s, axis=0, keepdims=True)
        idx = jnp.min(jnp.where(s == m, rows, float(s.shape[0])), axis=0, keepdims=True)
        hit = rows == idx
        s = jnp.where(hit, -jnp.inf, s)
        vals.append(m)
        if payload is None:
            picks.append(idx)
        else:
            picks.append(jnp.max(jnp.where(hit, payload, -1), axis=0, keepdims=True))
    return jnp.concatenate(vals, axis=0), jnp.concatenate(picks, axis=0)


def _pair_rows(a, b):
    K = PEER_TOPK
    out = []
    for i in range(K // 2):
        jn = K if i == 0 else K // 2
        out.append((a[i:i + 1, :], b[0:jn, :]))
    out.append((a[K // 2:K, :], b[0:1, :]))
    return out


def _peer_topk_kernel(s_ref, ids_ref, gates_ref):
    K = PEER_TOPK

    def head(h, carry):
        s0, i0 = _top_rows(s_ref[2 * h], K)
        s1, i1 = _top_rows(s_ref[2 * h + 1], K)
        e0 = i0.astype(jnp.int32) * PEER_N_KEYS
        e1 = i1.astype(jnp.int32)
        cand = jnp.concatenate([x + y for x, y in _pair_rows(s0, s1)], axis=0)
        cand_id = jnp.concatenate([x + y for x, y in _pair_rows(e0, e1)], axis=0)
        best, ids = _top_rows(cand, K, payload=cand_id)
        e = jnp.exp(best - best[0:1, :])
        gates = e / jnp.sum(e, axis=0, keepdims=True)
        off = pl.multiple_of(h * K, K)
        ids_ref[pl.ds(off, K), :] = ids
        gates_ref[pl.ds(off, K), :] = gates
        return carry

    lax.fori_loop(0, PEER_HEADS, head, 0)


def _peer_topk(scores_t):
    _, _, T = scores_t.shape
    tk = 128
    return pl.pallas_call(
        _peer_topk_kernel,
        grid=(T // tk,),
        in_specs=[pl.BlockSpec((PEER_GROUPS, PEER_N_KEYS, tk), lambda i: (0, 0, i))],
        out_specs=[pl.BlockSpec((PEER_SEL, tk), lambda i: (0, i))] * 2,
        out_shape=[jax.ShapeDtypeStruct((PEER_SEL, T), jnp.int32),
                   jax.ShapeDtypeStruct((PEER_SEL, T), F32)],
        compiler_params=_cparams(("parallel",)),
        name="peer_topk",
    )(scores_t)


def _gelu_tanh(x):
    return 0.5 * x * (1.0 + jnp.tanh(math.sqrt(2.0 / math.pi) * (x + 0.044715 * x * x * x)))


def _peer_experts_kernel(ids_cur, ids_next, gexp_ref, h_ref, uv_hbm, out_ref, buf, sem):
    i = pl.program_id(0)
    n = pl.num_programs(0)
    sub = h_ref.shape[1]
    group_rows = PEER_TOK * PEER_SEL

    def issue_token(ids_ref, tok, group, t, k0=0, count=PEER_SEL):
        for k in range(k0, k0 + count):
            e = ids_ref[0, tok, k]
            pltpu.make_async_copy(uv_hbm.at[pl.ds(e, 1)],
                                  buf.at[group, pl.ds(t * PEER_SEL + k, 1)],
                                  sem.at[group]).start(priority=k % 2)

    def wait_group(group):
        pltpu.make_async_copy(uv_hbm.at[pl.ds(0, group_rows)], buf.at[group], sem.at[group]).wait()

    @pl.when(i == 0)
    def _():
        for group in range(PEER_AHEAD):
            def body(t, carry, group=group):
                issue_token(ids_cur, group * PEER_TOK + t, group, t)
                return carry
            lax.fori_loop(0, PEER_TOK, body, 0)

    ch = PEER_CHUNK
    n_ch = PEER_SEL // ch
    ccols = ch * sub
    lane = lax.broadcasted_iota(jnp.int32, (sub, ccols), 1)
    diag = (lane % sub) == lax.broadcasted_iota(jnp.int32, (sub, ccols), 0)

    def group_sum(x):
        step = 1
        while step < sub:
            partner = jnp.where((lane % (2 * step)) < step, pltpu.roll(x, ccols - step, 1),
                                pltpu.roll(x, step, 1))
            x = x + partner
            step *= 2
        return x

    tokens = PEER_GROUPS_PER_STEP * PEER_TOK
    burst = PEER_SEL // (2 * n_ch)
    issued = [0]

    def request_burst():
        for r in range(issued[0], issued[0] + burst):
            tok, k = divmod(r, PEER_SEL)
            ahead = tok // PEER_TOK + PEER_AHEAD
            ids_ref = ids_cur if ahead < PEER_GROUPS_PER_STEP else ids_next
            group = ahead % PEER_GROUPS_PER_STEP
            issue_token(ids_ref, group * PEER_TOK + tok % PEER_TOK, group, tok % PEER_TOK, k, 1)
        issued[0] += burst

    def first_layer(tok):
        group, t = divmod(tok, PEER_TOK)
        if t == 0:
            wait_group(group)
        hb = h_ref[tok].astype(BF16)
        ps = []
        for c in range(n_ch):
            request_burst()
            rows = pl.ds(t * PEER_SEL + c * ch, ch)
            uc = buf[group, rows, 0:sub, :].reshape(ccols, 128).astype(BF16)
            ps.append(lax.dot_general(hb, uc, _NT, preferred_element_type=F32))
        return ps

    def weights(tok, ps):
        ws = []
        for c in range(n_ch):
            act = jnp.sum(group_sum(jnp.where(diag, ps[c], 0.0)), axis=0, keepdims=True)
            w = _gelu_tanh(act) * gexp_ref[tok:tok + 1, c * ccols:(c + 1) * ccols]
            ws.append(jnp.where(diag, jnp.broadcast_to(w, (sub, ccols)), 0.0).astype(BF16))
        return ws

    def second_layer(tok, ws):
        group, t = divmod(tok, PEER_TOK)
        o = None
        for c in range(n_ch):
            request_burst()
            rows = pl.ds(t * PEER_SEL + c * ch, ch)
            vc = buf[group, rows, sub:2 * sub, :].reshape(ccols, 128).astype(BF16)
            part = jnp.dot(ws[c], vc, preferred_element_type=F32)
            o = part if o is None else o + part
        return o

    ps = {tok: first_layer(tok) for tok in range(PEER_SKEW)}
    outs = []
    for tok in range(tokens):
        if tok + PEER_SKEW < tokens:
            ps[tok + PEER_SKEW] = first_layer(tok + PEER_SKEW)
        outs.append(second_layer(tok, weights(tok, ps.pop(tok))))
    assert issued[0] == tokens * PEER_SEL
    for tok, o in enumerate(outs):
        out_ref[tok] = o

    @pl.when(i == n - 1)
    def _():
        for group in range(PEER_AHEAD):
            wait_group(group)


def _peer_experts(ids, gates, h_tiles, uv):
    T, sub, _ = h_tiles.shape
    tb = PEER_GROUPS_PER_STEP * PEER_TOK
    nb = T // tb
    ids3 = ids.reshape(nb, tb, PEER_SEL)
    gexp = jnp.repeat(gates, sub, axis=1)
    smem = lambda imap: pl.BlockSpec((1, tb, PEER_SEL), imap, memory_space=pltpu.SMEM)
    tiles = pl.BlockSpec((tb, sub, 128), lambda i: (i, 0, 0))
    return pl.pallas_call(
        _peer_experts_kernel,
        grid=(nb,),
        in_specs=[smem(lambda i: (i, 0, 0)),
                  smem(lambda i: (jnp.minimum(i + 1, nb - 1), 0, 0)),
                  pl.BlockSpec((tb, PEER_SEL * sub), lambda i: (i, 0)),
                  tiles,
                  pl.BlockSpec(memory_space=pl.ANY)],
        out_specs=tiles,
        out_shape=jax.ShapeDtypeStruct((T, sub, 128), F32),
        scratch_shapes=[pltpu.VMEM((PEER_GROUPS_PER_STEP, PEER_TOK * PEER_SEL, 2 * sub, 128), F32),
                        pltpu.SemaphoreType.DMA((PEER_GROUPS_PER_STEP,))],
        compiler_params=_cparams(("arbitrary",)),
        name="peer_experts",
    )(ids3, ids3, gexp, h_tiles, uv)


def _ple_final_kernel(x_ref, d_ref, p_ref, gple_ref, wg_ref, wu_ref, gfin_ref, out_ref):
    x = x_ref[...] + d_ref[...].reshape(x_ref.shape)
    gate = _sigmoid(_bdot(_rms(x, gple_ref[...]), wg_ref[...]))
    x = x + gate * _bdot(p_ref[...], wu_ref[...])
    out_ref[...] = _rms(x, gfin_ref[...])


def _ple_final(x2, delta_tiles, p2, g_ple, w_gate, w_up, g_final):
    T, D = x2.shape
    tm = ROW_TILE
    consts = (g_ple, w_gate, w_up, g_final)
    return pl.pallas_call(
        _ple_final_kernel,
        grid=(T // tm,),
        in_specs=[pl.BlockSpec((tm, D), lambda i: (i, 0)),
                  pl.BlockSpec((tm,) + delta_tiles.shape[1:], lambda i: (i, 0, 0)),
                  pl.BlockSpec((tm, p2.shape[1]), lambda i: (i, 0)),
                  _full(g_ple.shape), _full(w_gate.shape), _full(w_up.shape), _full(g_final.shape)],
        out_specs=pl.BlockSpec((tm, D), lambda i: (i, 0)),
        out_shape=jax.ShapeDtypeStruct((T, D), F32),
        compiler_params=_cparams(("parallel",)),
        name="ple_final",
    )(x2, delta_tiles, p2, *consts)


def kernel(x, p, g_mix, w_in, ret_gn_g, rwkv_mu, rwkv_w0, rwkv_w_up, rwkv_a0, rwkv_a_up, rwkv_g_up, rwkv_k_k, rwkv_k_a, rwkv_r_k, rwkv_gn_g, rwkv_gn_b, w_ret_br, w_rwkv_br, w_o, g_ffn, w_pq, peer_sub_keys, peer_u, peer_v, g_ple, w_ple_gate, w_ple_up, g_final):
    B, S, D = x.shape
    T = B * S
    assert w_in.shape[0] == 1, "single-layer block: the final RMSNorm is fused into its last step"
    i = 0
    row = lambda t: t.reshape(1, -1)
    head_of = jnp.arange(WIDTH) // HEAD_DIM
    same_head = head_of[:, None] == head_of[None, :]
    mones = same_head.astype(BF16)
    mavg = (same_head.astype(F32) / HEAD_DIM).astype(BF16)
    ret_cols = 4 * WIDTH
    x2 = x.reshape(T, D)
    wi = w_in[i].astype(BF16)
    z_ret, z_rwkv, z_gate = _in_proj(
        x2, row(g_mix[i]), wi[:, :ret_cols], wi[:, ret_cols:ret_cols + RWKV_COLS],
        wi[:, ret_cols + RWKV_COLS:])
    y_ret = _retention(z_ret.reshape(B, S, ret_cols), row(ret_gn_g[i]), mavg)
    r, lw, k, v, kk, kka, g, bonus = _rwkv_prep(
        z_rwkv.reshape(B, S, RWKV_COLS), rwkv_mu[i], rwkv_w0[i], rwkv_w_up[i], rwkv_a0[i],
        rwkv_a_up[i], rwkv_g_up[i], rwkv_k_k[i], rwkv_k_a[i], rwkv_r_k[i], mones)
    o = _wkv7(r, lw, k, v, kk, kka)
    flat = lambda t: t.reshape(T, WIDTH)
    x2 = _merge(x2, flat(y_ret), flat(o), flat(bonus), flat(g), z_gate,
                row(rwkv_gn_g[i]), row(rwkv_gn_b[i]), mavg, w_ret_br[i].astype(BF16),
                w_rwkv_br[i].astype(BF16), w_o[i].astype(BF16))
    keys = peer_sub_keys[i].reshape(PEER_GROUPS, PEER_N_KEYS, PEER_HALF).astype(BF16)
    h_tiles, scores_t = _peer_scores(x2, row(g_ffn[i]), w_pq[i].astype(BF16), keys)
    ids_t, gates_t = _peer_topk(scores_t)
    n_exp = peer_u.shape[1]
    uv = jnp.concatenate([peer_u[i].reshape(n_exp, D // 128, 128),
                          peer_v[i].reshape(n_exp, D // 128, 128)], axis=1)
    delta_tiles = _peer_experts(ids_t.T, gates_t.T, h_tiles, uv)
    out = _ple_final(x2, delta_tiles, p[i].reshape(T, -1), row(g_ple[i]),
                     w_ple_gate[i].astype(BF16), w_ple_up[i].astype(BF16), row(g_final))
    return out.reshape(B, S, D)
```

```python
import math

import jax
import jax.numpy as jnp
from jax import lax
from jax.experimental import pallas as pl
from jax.experimental.pallas import tpu as pltpu

F32 = jnp.float32
BF16 = jnp.bfloat16

RMS_EPS = 1e-6
HEAD_DIM = 64
N_HEADS = 8
WIDTH = N_HEADS * HEAD_DIM
RET_CHUNK = 128
RET_BATCH = 2
RET_GN_EPS = 1e-5
ROPE_BASE = 10000.0
RWKV_GN_EPS = 64e-5
L2_EPS = 1e-12
DECAY_LORA = 64
AAA_LORA = 64
GATE_LORA = 128
RWKV_COLS = 3 * WIDTH + DECAY_LORA + AAA_LORA + GATE_LORA
WKV_CHUNK = 64
WKV_BATCH = 4

PEER_HEADS = 8
PEER_N_KEYS = 128
PEER_HALF = 128
PEER_TOPK = 16
PEER_GROUPS = 2 * PEER_HEADS
PEER_SEL = PEER_HEADS * PEER_TOPK
PEER_TOK = 8
PEER_GROUPS_PER_STEP = 4
PEER_AHEAD = 2
PEER_SKEW = 2
PEER_CHUNK = 32

ROW_TILE = 256
VMEM_LIMIT = 48 * 1024 * 1024

_NT = (((1,), (1,)), ((), ()))
_TN = (((0,), (0,)), ((), ()))


def _bdot(a, b):
    return jnp.dot(a.astype(BF16), b.astype(BF16), preferred_element_type=F32)


def _bdot_nt(a, b):
    return lax.dot_general(a.astype(BF16), b.astype(BF16), _NT, preferred_element_type=F32)


def _bdot_tn(a, b):
    return lax.dot_general(a.astype(BF16), b.astype(BF16), _TN, preferred_element_type=F32)


def _split2(a):
    hi = a.astype(BF16)
    lo = (a - hi.astype(F32)).astype(BF16)
    return hi, lo


def _seg_dot(a, m):
    hi, lo = _split2(a)
    return (jnp.dot(hi, m, preferred_element_type=F32)
            + jnp.dot(lo, m, preferred_element_type=F32))


def _rms(x, g):
    return x * lax.rsqrt(jnp.mean(x * x, axis=-1, keepdims=True) + RMS_EPS) * g


def _sigmoid(x):
    return 1.0 / (1.0 + jnp.exp(-x))


def _head_norm(o, mavg, eps):
    mu = _seg_dot(o, mavg)
    oc = o - mu
    var = _seg_dot(oc * oc, mavg)
    return oc * lax.rsqrt(var + eps)


def _cparams(sem, vmem=VMEM_LIMIT):
    return pltpu.CompilerParams(dimension_semantics=sem, vmem_limit_bytes=vmem)


def _full(shape):
    nd = len(shape)
    return pl.BlockSpec(shape, lambda *_: (0,) * nd)


def _in_proj_kernel(x_ref, g_ref, w1_ref, w2_ref, w3_ref, o1_ref, o2_ref, o3_ref):
    h = _rms(x_ref[...], g_ref[...]).astype(BF16)
    o1_ref[...] = jnp.dot(h, w1_ref[...], preferred_element_type=F32)
    o2_ref[...] = jnp.dot(h, w2_ref[...], preferred_element_type=F32)
    o3_ref[...] = jnp.dot(h, w3_ref[...], preferred_element_type=F32)


def _in_proj(x2, g, w_ret, w_rwkv, w_gate):
    T, D = x2.shape
    tm = ROW_TILE
    ws = (w_ret, w_rwkv, w_gate)
    return pl.pallas_call(
        _in_proj_kernel,
        grid=(T // tm,),
        in_specs=[pl.BlockSpec((tm, D), lambda i: (i, 0)), _full((1, D))]
        + [_full(w.shape) for w in ws],
        out_specs=[pl.BlockSpec((tm, w.shape[1]), lambda i: (i, 0)) for w in ws],
        out_shape=[jax.ShapeDtypeStruct((T, w.shape[1]), F32) for w in ws],
        compiler_params=_cparams(("parallel",)),
        name="in_proj",
    )(x2, g, *ws)


def _retention_kernel(z_ref, cos_ref, sin_ref, xi_ref, zeta_ref, decay_ref, cd_ref,
                      gn_ref, mavg_ref, y_ref, state_ref):
    @pl.when(pl.program_id(1) == 0)
    def _():
        state_ref[...] = jnp.zeros_like(state_ref)

    cos = cos_ref[...]
    sin = sin_ref[...]
    lane = lax.broadcasted_iota(jnp.int32, cos.shape, 1)
    first_half = (lane % HEAD_DIM) < (HEAD_DIM // 2)

    def rot(x):
        partner = jnp.where(first_half, pltpu.roll(x, WIDTH - HEAD_DIM // 2, 1),
                            pltpu.roll(x, HEAD_DIM // 2, 1))
        return x * cos + partner * sin

    qs, ks, vs, qxs, kzs, grs = [], [], [], [], [], []
    for b in range(RET_BATCH):
        z = z_ref[b]
        v = z[:, 2 * WIDTH:3 * WIDTH]
        grs.append(z[:, 3 * WIDTH:4 * WIDTH])
        qr = rot(z[:, 0:WIDTH])
        kr = rot(z[:, WIDTH:2 * WIDTH]) * (HEAD_DIM ** -0.5)
        qx = qr * xi_ref[...]
        kz = kr * zeta_ref[...]
        for h in range(N_HEADS):
            sl = slice(h * HEAD_DIM, (h + 1) * HEAD_DIM)
            qs.append(qr[:, sl].astype(BF16))
            ks.append(kr[:, sl].astype(BF16))
            vs.append(v[:, sl].astype(BF16))
            qxs.append(qx[:, sl])
            kzs.append(kz[:, sl])
    chains = range(RET_BATCH * N_HEADS)
    scores = [_bdot_nt(qs[c], ks[c]) * decay_ref[c % N_HEADS] for c in chains]
    states = [state_ref[c] for c in chains]
    cross = [_bdot(qxs[c], states[c]) for c in chains]
    outs = [_bdot(scores[c], vs[c]) + cross[c] for c in chains]
    for c in chains:
        state_ref[c] = states[c] * cd_ref[c % N_HEADS] + _bdot_tn(kzs[c], vs[c])
    for b in range(RET_BATCH):
        o = jnp.concatenate(outs[b * N_HEADS:(b + 1) * N_HEADS], axis=1)
        y = _head_norm(o, mavg_ref[...], RET_GN_EPS)
        y_ref[b] = grs[b] * _sigmoid(grs[b]) * (y * gn_ref[...])


def _retention(z_ret, gn_g, mavg):
    B, S, _ = z_ret.shape
    C = RET_CHUNK
    half = HEAD_DIM // 2
    inv_freq = ROPE_BASE ** (-jnp.arange(half, dtype=F32) * 2.0 / HEAD_DIM)
    ang = jnp.arange(S, dtype=F32)[:, None] * inv_freq[None, :]
    cos_h = jnp.concatenate([jnp.cos(ang), jnp.cos(ang)], axis=1)
    sin_h = jnp.concatenate([-jnp.sin(ang), jnp.sin(ang)], axis=1)
    cos = jnp.tile(cos_h, (1, N_HEADS))
    sin = jnp.tile(sin_h, (1, N_HEADS))
    log_gamma = jnp.log1p(-(2.0 ** (-5.0 - jnp.arange(N_HEADS, dtype=F32))))
    idx = jnp.arange(C, dtype=F32)
    diff = idx[:, None] - idx[None, :]
    causal = diff >= 0
    decay = jnp.where(causal[None], jnp.exp(log_gamma[:, None, None] * jnp.where(causal, diff, 0.0)[None]), 0.0)
    zeta = jnp.exp(log_gamma[:, None] * (C - 1.0 - idx)[None, :])
    xi = jnp.exp(log_gamma[:, None] * (idx + 1.0)[None, :])
    widen = lambda t: jnp.repeat(t.T, HEAD_DIM, axis=1)
    cd = jnp.broadcast_to(jnp.exp(log_gamma * C)[:, None, None], (N_HEADS, HEAD_DIM, HEAD_DIM))
    return pl.pallas_call(
        _retention_kernel,
        grid=(B // RET_BATCH, S // C),
        in_specs=[pl.BlockSpec((RET_BATCH, C, 4 * WIDTH), lambda b, c: (b, c, 0)),
                  pl.BlockSpec((C, WIDTH), lambda b, c: (c, 0)),
                  pl.BlockSpec((C, WIDTH), lambda b, c: (c, 0)),
                  _full((C, WIDTH)), _full((C, WIDTH)), _full((N_HEADS, C, C)),
                  _full((N_HEADS, HEAD_DIM, HEAD_DIM)), _full((1, WIDTH)), _full((WIDTH, WIDTH))],
        out_specs=pl.BlockSpec((RET_BATCH, C, WIDTH), lambda b, c: (b, c, 0)),
        out_shape=jax.ShapeDtypeStruct((B, S, WIDTH), F32),
        scratch_shapes=[pltpu.VMEM((RET_BATCH * N_HEADS, HEAD_DIM, HEAD_DIM), F32)],
        compiler_params=_cparams(("parallel", "arbitrary")),
        name="retention",
    )(z_ret, cos, sin, widen(xi), widen(zeta), decay, cd, gn_g, mavg)


def _rwkv_prep_kernel(z_ref, mu_ref, w0_ref, wup_ref, a0_ref, aup_ref, gup_ref, kk_ref,
                      ka_ref, rk_ref, mones_ref,
                      r_out, lw_out, k_out, v_out, kk_out, kka_out, g_out, bonus_out,
                      carry_ref):
    @pl.when(pl.program_id(1) == 0)
    def _():
        carry_ref[...] = jnp.zeros_like(carry_ref)

    z = z_ref[0]
    n = z.shape[0]
    row = lax.broadcasted_iota(jnp.int32, z.shape, 0)
    prev = jnp.where(row == 0, carry_ref[0:1, :], pltpu.roll(z, 1, 0))
    carry_ref[0:1, :] = z[n - 1:n, :]
    zs = z + (prev - z) * mu_ref[...]
    r = zs[:, 0:WIDTH]
    kr = zs[:, WIDTH:2 * WIDTH]
    vr = zs[:, 2 * WIDTH:3 * WIDTH]
    o = 3 * WIDTH
    wl = zs[:, o:o + DECAY_LORA]
    al = zs[:, o + DECAY_LORA:o + DECAY_LORA + AAA_LORA]
    gl = zs[:, o + DECAY_LORA + AAA_LORA:]
    t = -(w0_ref[...] + _bdot(jnp.tanh(wl), wup_ref[...]))
    softplus = jnp.maximum(t, 0.0) + jnp.log1p(jnp.exp(-jnp.abs(t)))
    w_log = -softplus - 0.5
    a = _sigmoid(a0_ref[...] + _bdot(al, aup_ref[...]))
    g = _bdot(_sigmoid(gl), gup_ref[...])
    mones = mones_ref[...]
    kk = kr * kk_ref[...]
    norm = jnp.sqrt(_seg_dot(kk * kk, mones))
    kk = kk / jnp.maximum(norm, L2_EPS)
    k2 = kr * (1.0 + (a - 1.0) * ka_ref[...])
    r_out[0] = r
    lw_out[0] = -jnp.exp(w_log)
    k_out[0] = k2
    v_out[0] = vr
    kk_out[0] = kk
    kka_out[0] = kk * a
    g_out[0] = g
    bonus_out[0] = _seg_dot(r * k2 * rk_ref[...], mones) * vr


def _rwkv_prep(z_rwkv, mu, w0, w_up, a0, a_up, g_up, k_k, k_a, r_k, mones):
    B, S, _ = z_rwkv.shape
    ts = ROW_TILE
    row = lambda t: t.reshape(1, -1)
    args = (row(mu), row(w0), w_up.astype(BF16), row(a0), a_up.astype(BF16),
            g_up.astype(BF16), row(k_k), row(k_a), row(r_k), mones)
    out_spec = pl.BlockSpec((1, ts, WIDTH), lambda b, s: (b, s, 0))
    return pl.pallas_call(
        _rwkv_prep_kernel,
        grid=(B, S // ts),
        in_specs=[pl.BlockSpec((1, ts, RWKV_COLS), lambda b, s: (b, s, 0))]
        + [_full(a.shape) for a in args],
        out_specs=[out_spec] * 8,
        out_shape=[jax.ShapeDtypeStruct((B, S, WIDTH), F32)] * 8,
        scratch_shapes=[pltpu.VMEM((8, RWKV_COLS), F32)],
        compiler_params=_cparams(("parallel", "arbitrary")),
        name="rwkv_prep",
    )(z_rwkv, *args)


def _wkv7_kernel(r_ref, lw_ref, k_ref, v_ref, kk_ref, kka_ref, tri_ref, o_ref, state_ref):
    @pl.when(pl.program_id(1) == 0)
    def _():
        state_ref[...] = jnp.zeros_like(state_ref)

    L = WKV_CHUNK
    tri = tri_ref[...]
    ri = lax.broadcasted_iota(jnp.int32, (2 * L, L), 0)
    ci = lax.broadcasted_iota(jnp.int32, (2 * L, L), 1)
    mask = jnp.where(ri < L, ri, ri - L + 1) > ci
    eye = (lax.broadcasted_iota(jnp.int32, (L, L), 0)
           == lax.broadcasted_iota(jnp.int32, (L, L), 1)).astype(F32)

    ar, vf, bt, kt, bke, gl = [], [], [], [], [], []
    for b in range(WKV_BATCH):
        lw = lw_ref[b]
        hi = lw.astype(BF16)
        rem = lw - hi.astype(F32)
        mid = rem.astype(BF16)
        lo = (rem - mid.astype(F32)).astype(BF16)
        cum = (jnp.dot(tri, hi, preferred_element_type=F32)
               + jnp.dot(tri, mid, preferred_element_type=F32)
               + jnp.dot(tri, lo, preferred_element_type=F32))
        cum_last = cum[L - 1:L, :]
        inv_g = jnp.exp(-cum)
        to_end = jnp.exp(cum_last - cum)
        g_last = jnp.exp(cum_last)
        kk = kk_ref[b]
        kka = kka_ref[b]
        k = k_ref[b]
        v = v_ref[b]
        a_t = -kk * jnp.exp(cum - lw)
        b_t = kka * inv_g
        k_t = k * inv_g
        r_t = r_ref[b] * jnp.exp(cum)
        b_end = kka * to_end
        k_end = k * to_end
        for h in range(N_HEADS):
            sl = slice(h * HEAD_DIM, (h + 1) * HEAD_DIM)
            ar.append(jnp.concatenate([a_t[:, sl], r_t[:, sl]], axis=0).astype(BF16))
            vf.append(v[:, sl])
            bt.append(b_t[:, sl])
            kt.append(k_t[:, sl])
            bke.append(jnp.concatenate([b_end[:, sl], k_end[:, sl]], axis=0))
            gl.append(g_last[:, sl])
    chains = range(WKV_BATCH * N_HEADS)
    vs = [x.astype(BF16) for x in vf]
    abrb = [jnp.where(mask, _bdot_nt(ar[c], bt[c]), 0.0) for c in chains]
    akrk = [jnp.where(mask, _bdot_nt(ar[c], kt[c]), 0.0) for c in chains]
    p = [m[:L] for m in abrb]
    inv = [eye + m for m in p]
    for _ in range(int(math.log2(L)) - 1):
        p = [_bdot(m, m) for m in p]
        inv = [inv[c] + _bdot(inv[c], p[c]) for c in chains]
    states = [state_ref[c] for c in chains]
    xs = [_bdot_nt(ar[c], states[c]) for c in chains]
    kv = [_bdot(akrk[c], vs[c]) for c in chains]
    u = [_bdot(inv[c], xs[c][:L] + kv[c][:L]) for c in chains]
    y = [xs[c][L:] + kv[c][L:] + _bdot(abrb[c][L:], u[c]) for c in chains]
    for c in chains:
        uv = jnp.concatenate([u[c], vf[c]], axis=0)
        state_ref[c] = states[c] * gl[c] + _bdot_tn(uv, bke[c])
    for b in range(WKV_BATCH):
        o_ref[b] = jnp.concatenate(y[b * N_HEADS:(b + 1) * N_HEADS], axis=1)


def _wkv7(r, lw, k, v, kk, kka):
    B, S, _ = r.shape
    L = WKV_CHUNK
    nb = WKV_BATCH
    tri = jnp.tril(jnp.ones((L, L), F32)).astype(BF16)
    spec = pl.BlockSpec((nb, L, WIDTH), lambda b, c: (b, c, 0))
    return pl.pallas_call(
        _wkv7_kernel,
        grid=(B // nb, S // L),
        in_specs=[spec] * 6 + [_full((L, L))],
        out_specs=spec,
        out_shape=jax.ShapeDtypeStruct((B, S, WIDTH), F32),
        scratch_shapes=[pltpu.VMEM((nb * N_HEADS, HEAD_DIM, HEAD_DIM), F32)],
        compiler_params=_cparams(("parallel", "arbitrary")),
        name="wkv7",
    )(r, lw, k, v, kk, kka, tri)


def _merge_kernel(x_ref, yret_ref, o_ref, bonus_ref, g_ref, zg_ref, gng_ref, gnb_ref,
                  mavg_ref, wret_ref, wrwkv_ref, wo_ref, x1_ref):
    y = _head_norm(o_ref[...], mavg_ref[...], RWKV_GN_EPS)
    y_rwkv = (y * gng_ref[...] + gnb_ref[...] + bonus_ref[...]) * g_ref[...]
    br = _bdot(yret_ref[...], wret_ref[...])
    bw = _bdot(y_rwkv, wrwkv_ref[...])
    zg = zg_ref[...]
    d = br.shape[1]
    merged = _sigmoid(zg[:, :d]) * br + _sigmoid(zg[:, d:]) * bw
    x1_ref[...] = x_ref[...] + _bdot(merged, wo_ref[...])


def _merge(x2, y_ret, o, bonus, g, z_gate, gn_g, gn_b, mavg, w_ret_br, w_rwkv_br, w_o):
    T, D = x2.shape
    tm = ROW_TILE
    rows = lambda n: pl.BlockSpec((tm, n), lambda i: (i, 0))
    consts = (gn_g, gn_b, mavg, w_ret_br, w_rwkv_br, w_o)
    return pl.pallas_call(
        _merge_kernel,
        grid=(T // tm,),
        in_specs=[rows(D), rows(WIDTH), rows(WIDTH), rows(WIDTH), rows(WIDTH), rows(2 * D)]
        + [_full(c.shape) for c in consts],
        out_specs=rows(D),
        out_shape=jax.ShapeDtypeStruct((T, D), F32),
        compiler_params=_cparams(("parallel",)),
        name="merge",
    )(x2, y_ret, o, bonus, g, z_gate, *consts)


def _peer_scores_kernel(x_ref, g_ref, wq_ref, keys_ref, h_ref, s_ref):
    h = _rms(x_ref[...], g_ref[...])
    h_ref[...] = h.reshape(h_ref.shape)
    q = _bdot(h, wq_ref[...]).astype(BF16)
    for grp in range(PEER_GROUPS):
        s_ref[grp] = lax.dot_general(keys_ref[grp], q[:, grp * PEER_HALF:(grp + 1) * PEER_HALF],
                                     _NT, preferred_element_type=F32)


def _peer_scores(x1, g_ffn, w_pq, keys):
    T, D = x1.shape
    tm = ROW_TILE
    return pl.pallas_call(
        _peer_scores_kernel,
        grid=(T // tm,),
        in_specs=[pl.BlockSpec((tm, D), lambda i: (i, 0)), _full((1, D)), _full(w_pq.shape),
                  _full(keys.shape)],
        out_specs=[pl.BlockSpec((tm, D // 128, 128), lambda i: (i, 0, 0)),
                   pl.BlockSpec((PEER_GROUPS, PEER_N_KEYS, tm), lambda i: (0, 0, i))],
        out_shape=[jax.ShapeDtypeStruct((T, D // 128, 128), F32),
                   jax.ShapeDtypeStruct((PEER_GROUPS, PEER_N_KEYS, T), F32)],
        compiler_params=_cparams(("parallel",)),
        name="peer_scores",
    )(x1, g_ffn, w_pq, keys)


def _top_rows(s, count, payload=None):
    rows = lax.broadcasted_iota(jnp.int32, s.shape, 0).astype(F32)
    vals, picks = [], []
    for _ in range(count):
        m = jnp.max(s, axis=0, keepdims=True)
        idx = jnp.min(jnp.where(s == m, rows, float(s.shape[0])), axis=0, keepdims=True)
        hit = rows == idx
        s = jnp.where(hit, -jnp.inf, s)
        vals.append(m)
        if payload is None:
            picks.append(idx)
        else:
            picks.append(jnp.max(jnp.where(hit, payload, -1), axis=0, keepdims=True))
    return jnp.concatenate(vals, axis=0), jnp.concatenate(picks, axis=0)


def _pair_rows(a, b):
    K = PEER_TOPK
    out = []
    for i in range(K // 2):
        jn = K if i == 0 else K // 2
        out.append((a[i:i + 1, :], b[0:jn, :]))
    out.append((a[K // 2:K, :], b[0:1, :]))
    return out


def _peer_topk_kernel(s_ref, ids_ref, gates_ref):
    K = PEER_TOPK

    def head(h, carry):
        s0, i0 = _top_rows(s_ref[2 * h], K)
        s1, i1 = _top_rows(s_ref[2 * h + 1], K)
        e0 = i0.astype(jnp.int32) * PEER_N_KEYS
        e1 = i1.astype(jnp.int32)
        cand = jnp.concatenate([x + y for x, y in _pair_rows(s0, s1)], axis=0)
        cand_id = jnp.concatenate([x + y for x, y in _pair_rows(e0, e1)], axis=0)
        best, ids = _top_rows(cand, K, payload=cand_id)
        e = jnp.exp(best - best[0:1, :])
        gates = e / jnp.sum(e, axis=0, keepdims=True)
        off = pl.multiple_of(h * K, K)
        ids_ref[pl.ds(off, K), :] = ids
        gates_ref[pl.ds(off, K), :] = gates
        return carry

    lax.fori_loop(0, PEER_HEADS, head, 0)


def _peer_topk(scores_t):
    _, _, T = scores_t.shape
    tk = 128
    return pl.pallas_call(
        _peer_topk_kernel,
        grid=(T // tk,),
        in_specs=[pl.BlockSpec((PEER_GROUPS, PEER_N_KEYS, tk), lambda i: (0, 0, i))],
        out_specs=[pl.BlockSpec((PEER_SEL, tk), lambda i: (0, i))] * 2,
        out_shape=[jax.ShapeDtypeStruct((PEER_SEL, T), jnp.int32),
                   jax.ShapeDtypeStruct((PEER_SEL, T), F32)],
        compiler_params=_cparams(("parallel",)),
        name="peer_topk",
    )(scores_t)


def _gelu_tanh(x):
    return 0.5 * x * (1.0 + jnp.tanh(math.sqrt(2.0 / math.pi) * (x + 0.044715 * x * x * x)))


def _peer_experts_kernel(ids_cur, ids_next, gexp_ref, h_ref, uv_hbm, out_ref, buf, sem):
    i = pl.program_id(0)
    n = pl.num_programs(0)
    sub = h_ref.shape[1]
    group_rows = PEER_TOK * PEER_SEL

    def issue_token(ids_ref, tok, group, t, k0=0, count=PEER_SEL):
        for k in range(k0, k0 + count):
            e = ids_ref[0, tok, k]
            pltpu.make_async_copy(uv_hbm.at[pl.ds(e, 1)],
                                  buf.at[group, pl.ds(t * PEER_SEL + k, 1)],
                                  sem.at[group]).start(priority=k % 2)

    def wait_group(group):
        pltpu.make_async_copy(uv_hbm.at[pl.ds(0, group_rows)], buf.at[group], sem.at[group]).wait()

    @pl.when(i == 0)
    def _():
        for group in range(PEER_AHEAD):
            def body(t, carry, group=group):
                issue_token(ids_cur, group * PEER_TOK + t, group, t)
                return carry
            lax.fori_loop(0, PEER_TOK, body, 0)

    ch = PEER_CHUNK
    n_ch = PEER_SEL // ch
    ccols = ch * sub
    lane = lax.broadcasted_iota(jnp.int32, (sub, ccols), 1)
    diag = (lane % sub) == lax.broadcasted_iota(jnp.int32, (sub, ccols), 0)

    def group_sum(x):
        step = 1
        while step < sub:
            partner = jnp.where((lane % (2 * step)) < step, pltpu.roll(x, ccols - step, 1),
                                pltpu.roll(x, step, 1))
            x = x + partner
            step *= 2
        return x

    tokens = PEER_GROUPS_PER_STEP * PEER_TOK
    burst = PEER_SEL // (2 * n_ch)
    issued = [0]

    def request_burst():
        for r in range(issued[0], issued[0] + burst):
            tok, k = divmod(r, PEER_SEL)
            ahead = tok // PEER_TOK + PEER_AHEAD
            ids_ref = ids_cur if ahead < PEER_GROUPS_PER_STEP else ids_next
            group = ahead % PEER_GROUPS_PER_STEP
            issue_token(ids_ref, group * PEER_TOK + tok % PEER_TOK, group, tok % PEER_TOK, k, 1)
        issued[0] += burst

    def first_layer(tok):
        group, t = divmod(tok, PEER_TOK)
        if t == 0:
            wait_group(group)
        hb = h_ref[tok].astype(BF16)
        ps = []
        for c in range(n_ch):
            request_burst()
            rows = pl.ds(t * PEER_SEL + c * ch, ch)
            uc = buf[group, rows, 0:sub, :].reshape(ccols, 128).astype(BF16)
            ps.append(lax.dot_general(hb, uc, _NT, preferred_element_type=F32))
        return ps

    def weights(tok, ps):
        ws = []
        for c in range(n_ch):
            act = jnp.sum(group_sum(jnp.where(diag, ps[c], 0.0)), axis=0, keepdims=True)
            w = _gelu_tanh(act) * gexp_ref[tok:tok + 1, c * ccols:(c + 1) * ccols]
            ws.append(jnp.where(diag, jnp.broadcast_to(w, (sub, ccols)), 0.0).astype(BF16))
        return ws

    def second_layer(tok, ws):
        group, t = divmod(tok, PEER_TOK)
        o = None
        for c in range(n_ch):
            request_burst()
            rows = pl.ds(t * PEER_SEL + c * ch, ch)
            vc = buf[group, rows, sub:2 * sub, :].reshape(ccols, 128).astype(BF16)
            part = jnp.dot(ws[c], vc, preferred_element_type=F32)
            o = part if o is None else o + part
        return o

    ps = {tok: first_layer(tok) for tok in range(PEER_SKEW)}
    outs = []
    for tok in range(tokens):
        if tok + PEER_SKEW < tokens:
            ps[tok + PEER_SKEW] = first_layer(tok + PEER_SKEW)
        outs.append(second_layer(tok, weights(tok, ps.pop(tok))))
    assert issued[0] == tokens * PEER_SEL
    for tok, o in enumerate(outs):
        out_ref[tok] = o

    @pl.when(i == n - 1)
    def _():
        for group in range(PEER_AHEAD):
            wait_group(group)


def _peer_experts(ids, gates, h_tiles, uv):
    T, sub, _ = h_tiles.shape
    tb = PEER_GROUPS_PER_STEP * PEER_TOK
    nb = T // tb
    ids3 = ids.reshape(nb, tb, PEER_SEL)
    gexp = jnp.repeat(gates, sub, axis=1)
    smem = lambda imap: pl.BlockSpec((1, tb, PEER_SEL), imap, memory_space=pltpu.SMEM)
    tiles = pl.BlockSpec((tb, sub, 128), lambda i: (i, 0, 0))
    return pl.pallas_call(
        _peer_experts_kernel,
        grid=(nb,),
        in_specs=[smem(lambda i: (i, 0, 0)),
                  smem(lambda i: (jnp.minimum(i + 1, nb - 1), 0, 0)),
                  pl.BlockSpec((tb, PEER_SEL * sub), lambda i: (i, 0)),
                  tiles,
                  pl.BlockSpec(memory_space=pl.ANY)],
        out_specs=tiles,
        out_shape=jax.ShapeDtypeStruct((T, sub, 128), F32),
        scratch_shapes=[pltpu.VMEM((PEER_GROUPS_PER_STEP, PEER_TOK * PEER_SEL, 2 * sub, 128), F32),
                        pltpu.SemaphoreType.DMA((PEER_GROUPS_PER_STEP,))],
        compiler_params=_cparams(("arbitrary",)),
        name="peer_experts",
    )(ids3, ids3, gexp, h_tiles, uv)


def _ple_final_kernel(x_ref, d_ref, p_ref, gple_ref, wg_ref, wu_ref, gfin_ref, out_ref):
    x = x_ref[...] + d_ref[...].reshape(x_ref.shape)
    gate = _sigmoid(_bdot(_rms(x, gple_ref[...]), wg_ref[...]))
    x = x + gate * _bdot(p_ref[...], wu_ref[...])
    out_ref[...] = _rms(x, gfin_ref[...])


def _ple_final(x2, delta_tiles, p2, g_ple, w_gate, w_up, g_final):
    T, D = x2.shape
    tm = ROW_TILE
    consts = (g_ple, w_gate, w_up, g_final)
    return pl.pallas_call(
        _ple_final_kernel,
        grid=(T // tm,),
        in_specs=[pl.BlockSpec((tm, D), lambda i: (i, 0)),
                  pl.BlockSpec((tm,) + delta_tiles.shape[1:], lambda i: (i, 0, 0)),
                  pl.BlockSpec((tm, p2.shape[1]), lambda i: (i, 0)),
                  _full(g_ple.shape), _full(w_gate.shape), _full(w_up.shape), _full(g_final.shape)],
        out_specs=pl.BlockSpec((tm, D), lambda i: (i, 0)),
        out_shape=jax.ShapeDtypeStruct((T, D), F32),
        compiler_params=_cparams(("parallel",)),
        name="ple_final",
    )(x2, delta_tiles, p2, *consts)


def kernel(x, p, g_mix, w_in, ret_gn_g, rwkv_mu, rwkv_w0, rwkv_w_up, rwkv_a0, rwkv_a_up, rwkv_g_up, rwkv_k_k, rwkv_k_a, rwkv_r_k, rwkv_gn_g, rwkv_gn_b, w_ret_br, w_rwkv_br, w_o, g_ffn, w_pq, peer_sub_keys, peer_u, peer_v, g_ple, w_ple_gate, w_ple_up, g_final):
    B, S, D = x.shape
    T = B * S
    assert w_in.shape[0] == 1, "single-layer block: the final RMSNorm is fused into its last step"
    i = 0
    row = lambda t: t.reshape(1, -1)
    head_of = jnp.arange(WIDTH) // HEAD_DIM
    same_head = head_of[:, None] == head_of[None, :]
    mones = same_head.astype(BF16)
    mavg = (same_head.astype(F32) / HEAD_DIM).astype(BF16)
    ret_cols = 4 * WIDTH
    x2 = x.reshape(T, D)
    wi = w_in[i].astype(BF16)
    z_ret, z_rwkv, z_gate = _in_proj(
        x2, row(g_mix[i]), wi[:, :ret_cols], wi[:, ret_cols:ret_cols + RWKV_COLS],
        wi[:, ret_cols + RWKV_COLS:])
    y_ret = _retention(z_ret.reshape(B, S, ret_cols), row(ret_gn_g[i]), mavg)
    r, lw, k, v, kk, kka, g, bonus = _rwkv_prep(
        z_rwkv.reshape(B, S, RWKV_COLS), rwkv_mu[i], rwkv_w0[i], rwkv_w_up[i], rwkv_a0[i],
        rwkv_a_up[i], rwkv_g_up[i], rwkv_k_k[i], rwkv_k_a[i], rwkv_r_k[i], mones)
    o = _wkv7(r, lw, k, v, kk, kka)
    flat = lambda t: t.reshape(T, WIDTH)
    x2 = _merge(x2, flat(y_ret), flat(o), flat(bonus), flat(g), z_gate,
                row(rwkv_gn_g[i]), row(rwkv_gn_b[i]), mavg, w_ret_br[i].astype(BF16),
                w_rwkv_br[i].astype(BF16), w_o[i].astype(BF16))
    keys = peer_sub_keys[i].reshape(PEER_GROUPS, PEER_N_KEYS, PEER_HALF).astype(BF16)
    h_tiles, scores_t = _peer_scores(x2, row(g_ffn[i]), w_pq[i].astype(BF16), keys)
    ids_t, gates_t = _peer_topk(scores_t)
    n_exp = peer_u.shape[1]
    uv = jnp.concatenate([peer_u[i].reshape(n_exp, D // 128, 128),
                          peer_v[i].reshape(n_exp, D // 128, 128)], axis=1)
    delta_tiles = _peer_experts(ids_t.T, gates_t.T, h_tiles, uv)
    out = _ple_final(x2, delta_tiles, p[i].reshape(T, -1), row(g_ple[i]),
                     w_ple_gate[i].astype(BF16), w_ple_up[i].astype(BF16), row(g_final))
    return out.reshape(B, S, D)
```

```python
import math

import jax
import jax.numpy as jnp
from jax import lax
from jax.experimental import pallas as pl
from jax.experimental.pallas import tpu as pltpu

F32 = jnp.float32
BF16 = jnp.bfloat16

RMS_EPS = 1e-6
HEAD_DIM = 64
N_HEADS = 8
WIDTH = N_HEADS * HEAD_DIM
RET_CHUNK = 128
RET_BATCH = 2
RET_GN_EPS = 1e-5
ROPE_BASE = 10000.0
RWKV_GN_EPS = 64e-5
L2_EPS = 1e-12
DECAY_LORA = 64
AAA_LORA = 64
GATE_LORA = 128
RWKV_COLS = 3 * WIDTH + DECAY_LORA + AAA_LORA + GATE_LORA
WKV_CHUNK = 64
WKV_BATCH = 4

PEER_HEADS = 8
PEER_N_KEYS = 128
PEER_HALF = 128
PEER_TOPK = 16
PEER_GROUPS = 2 * PEER_HEADS
PEER_SEL = PEER_HEADS * PEER_TOPK
PEER_ROUTE_TOK = 128
PEER_ROUTE_EVERY = 2
PEER_RANGES = 8
PEER_TOK = 8
PEER_GROUPS_PER_STEP = 4
PEER_AHEAD = 2
PEER_SKEW = 2
PEER_CHUNK = 32

ROW_TILE = 256
VMEM_LIMIT = 48 * 1024 * 1024

_NT = (((1,), (1,)), ((), ()))
_TN = (((0,), (0,)), ((), ()))


def _bdot(a, b):
    return jnp.dot(a.astype(BF16), b.astype(BF16), preferred_element_type=F32)


def _bdot_nt(a, b):
    return lax.dot_general(a.astype(BF16), b.astype(BF16), _NT, preferred_element_type=F32)


def _bdot_tn(a, b):
    return lax.dot_general(a.astype(BF16), b.astype(BF16), _TN, preferred_element_type=F32)


def _split2(a):
    hi = a.astype(BF16)
    lo = (a - hi.astype(F32)).astype(BF16)
    return hi, lo


def _seg_dot(a, m):
    hi, lo = _split2(a)
    return (jnp.dot(hi, m, preferred_element_type=F32)
            + jnp.dot(lo, m, preferred_element_type=F32))


def _rms(x, g):
    return x * lax.rsqrt(jnp.mean(x * x, axis=-1, keepdims=True) + RMS_EPS) * g


def _sigmoid(x):
    return 1.0 / (1.0 + jnp.exp(-x))


def _head_norm(o, mavg, eps):
    mu = _seg_dot(o, mavg)
    oc = o - mu
    var = _seg_dot(oc * oc, mavg)
    return oc * lax.rsqrt(var + eps)


def _cparams(sem, vmem=VMEM_LIMIT):
    return pltpu.CompilerParams(dimension_semantics=sem, vmem_limit_bytes=vmem)


def _full(shape):
    nd = len(shape)
    return pl.BlockSpec(shape, lambda *_: (0,) * nd)


def _in_proj_kernel(x_ref, g_ref, w1_ref, w2_ref, w3_ref, o1_ref, o2_ref, o3_ref):
    h = _rms(x_ref[...], g_ref[...]).astype(BF16)
    o1_ref[...] = jnp.dot(h, w1_ref[...], preferred_element_type=F32)
    o2_ref[...] = jnp.dot(h, w2_ref[...], preferred_element_type=F32)
    o3_ref[...] = jnp.dot(h, w3_ref[...], preferred_element_type=F32)


def _in_proj(x2, g, w_ret, w_rwkv, w_gate):
    T, D = x2.shape
    tm = ROW_TILE
    ws = (w_ret, w_rwkv, w_gate)
    return pl.pallas_call(
        _in_proj_kernel,
        grid=(T // tm,),
        in_specs=[pl.BlockSpec((tm, D), lambda i: (i, 0)), _full((1, D))]
        + [_full(w.shape) for w in ws],
        out_specs=[pl.BlockSpec((tm, w.shape[1]), lambda i: (i, 0)) for w in ws],
        out_shape=[jax.ShapeDtypeStruct((T, w.shape[1]), F32) for w in ws],
        compiler_params=_cparams(("parallel",)),
        name="in_proj",
    )(x2, g, *ws)


def _retention_kernel(z_ref, cos_ref, sin_ref, xi_ref, zeta_ref, decay_ref, cd_ref,
                      gn_ref, mavg_ref, y_ref, state_ref):
    @pl.when(pl.program_id(1) == 0)
    def _():
        state_ref[...] = jnp.zeros_like(state_ref)

    cos = cos_ref[...]
    sin = sin_ref[...]
    lane = lax.broadcasted_iota(jnp.int32, cos.shape, 1)
    first_half = (lane % HEAD_DIM) < (HEAD_DIM // 2)

    def rot(x):
        partner = jnp.where(first_half, pltpu.roll(x, WIDTH - HEAD_DIM // 2, 1),
                            pltpu.roll(x, HEAD_DIM // 2, 1))
        return x * cos + partner * sin

    qs, ks, vs, qxs, kzs, grs = [], [], [], [], [], []
    for b in range(RET_BATCH):
        z = z_ref[b]
        v = z[:, 2 * WIDTH:3 * WIDTH]
        grs.append(z[:, 3 * WIDTH:4 * WIDTH])
        qr = rot(z[:, 0:WIDTH])
        kr = rot(z[:, WIDTH:2 * WIDTH]) * (HEAD_DIM ** -0.5)
        qx = qr * xi_ref[...]
        kz = kr * zeta_ref[...]
        for h in range(N_HEADS):
            sl = slice(h * HEAD_DIM, (h + 1) * HEAD_DIM)
            qs.append(qr[:, sl].astype(BF16))
            ks.append(kr[:, sl].astype(BF16))
            vs.append(v[:, sl].astype(BF16))
            qxs.append(qx[:, sl])
            kzs.append(kz[:, sl])
    chains = range(RET_BATCH * N_HEADS)
    scores = [_bdot_nt(qs[c], ks[c]) * decay_ref[c % N_HEADS] for c in chains]
    states = [state_ref[c] for c in chains]
    cross = [_bdot(qxs[c], states[c]) for c in chains]
    outs = [_bdot(scores[c], vs[c]) + cross[c] for c in chains]
    for c in chains:
        state_ref[c] = states[c] * cd_ref[c % N_HEADS] + _bdot_tn(kzs[c], vs[c])
    for b in range(RET_BATCH):
        o = jnp.concatenate(outs[b * N_HEADS:(b + 1) * N_HEADS], axis=1)
        y = _head_norm(o, mavg_ref[...], RET_GN_EPS)
        y_ref[b] = grs[b] * _sigmoid(grs[b]) * (y * gn_ref[...])


def _retention(z_ret, gn_g, mavg):
    B, S, _ = z_ret.shape
    C = RET_CHUNK
    half = HEAD_DIM // 2
    inv_freq = ROPE_BASE ** (-jnp.arange(half, dtype=F32) * 2.0 / HEAD_DIM)
    ang = jnp.arange(S, dtype=F32)[:, None] * inv_freq[None, :]
    cos_h = jnp.concatenate([jnp.cos(ang), jnp.cos(ang)], axis=1)
    sin_h = jnp.concatenate([-jnp.sin(ang), jnp.sin(ang)], axis=1)
    cos = jnp.tile(cos_h, (1, N_HEADS))
    sin = jnp.tile(sin_h, (1, N_HEADS))
    log_gamma = jnp.log1p(-(2.0 ** (-5.0 - jnp.arange(N_HEADS, dtype=F32))))
    idx = jnp.arange(C, dtype=F32)
    diff = idx[:, None] - idx[None, :]
    causal = diff >= 0
    decay = jnp.where(causal[None], jnp.exp(log_gamma[:, None, None] * jnp.where(causal, diff, 0.0)[None]), 0.0)
    zeta = jnp.exp(log_gamma[:, None] * (C - 1.0 - idx)[None, :])
    xi = jnp.exp(log_gamma[:, None] * (idx + 1.0)[None, :])
    widen = lambda t: jnp.repeat(t.T, HEAD_DIM, axis=1)
    cd = jnp.broadcast_to(jnp.exp(log_gamma * C)[:, None, None], (N_HEADS, HEAD_DIM, HEAD_DIM))
    return pl.pallas_call(
        _retention_kernel,
        grid=(B // RET_BATCH, S // C),
        in_specs=[pl.BlockSpec((RET_BATCH, C, 4 * WIDTH), lambda b, c: (b, c, 0)),
                  pl.BlockSpec((C, WIDTH), lambda b, c: (c, 0)),
                  pl.BlockSpec((C, WIDTH), lambda b, c: (c, 0)),
                  _full((C, WIDTH)), _full((C, WIDTH)), _full((N_HEADS, C, C)),
                  _full((N_HEADS, HEAD_DIM, HEAD_DIM)), _full((1, WIDTH)), _full((WIDTH, WIDTH))],
        out_specs=pl.BlockSpec((RET_BATCH, C, WIDTH), lambda b, c: (b, c, 0)),
        out_shape=jax.ShapeDtypeStruct((B, S, WIDTH), F32),
        scratch_shapes=[pltpu.VMEM((RET_BATCH * N_HEADS, HEAD_DIM, HEAD_DIM), F32)],
        compiler_params=_cparams(("parallel", "arbitrary")),
        name="retention",
    )(z_ret, cos, sin, widen(xi), widen(zeta), decay, cd, gn_g, mavg)


def _rwkv_prep_kernel(z_ref, mu_ref, w0_ref, wup_ref, a0_ref, aup_ref, gup_ref, kk_ref,
                      ka_ref, rk_ref, mones_ref,
                      r_out, lw_out, k_out, v_out, kk_out, kka_out, g_out, bonus_out,
                      carry_ref):
    @pl.when(pl.program_id(1) == 0)
    def _():
        carry_ref[...] = jnp.zeros_like(carry_ref)

    z = z_ref[0]
    n = z.shape[0]
    row = lax.broadcasted_iota(jnp.int32, z.shape, 0)
    prev = jnp.where(row == 0, carry_ref[0:1, :], pltpu.roll(z, 1, 0))
    carry_ref[0:1, :] = z[n - 1:n, :]
    zs = z + (prev - z) * mu_ref[...]
    r = zs[:, 0:WIDTH]
    kr = zs[:, WIDTH:2 * WIDTH]
    vr = zs[:, 2 * WIDTH:3 * WIDTH]
    o = 3 * WIDTH
    wl = zs[:, o:o + DECAY_LORA]
    al = zs[:, o + DECAY_LORA:o + DECAY_LORA + AAA_LORA]
    gl = zs[:, o + DECAY_LORA + AAA_LORA:]
    t = -(w0_ref[...] + _bdot(jnp.tanh(wl), wup_ref[...]))
    softplus = jnp.maximum(t, 0.0) + jnp.log1p(jnp.exp(-jnp.abs(t)))
    w_log = -softplus - 0.5
    a = _sigmoid(a0_ref[...] + _bdot(al, aup_ref[...]))
    g = _bdot(_sigmoid(gl), gup_ref[...])
    mones = mones_ref[...]
    kk = kr * kk_ref[...]
    norm = jnp.sqrt(_seg_dot(kk * kk, mones))
    kk = kk / jnp.maximum(norm, L2_EPS)
    k2 = kr * (1.0 + (a - 1.0) * ka_ref[...])
    r_out[0] = r
    lw_out[0] = -jnp.exp(w_log)
    k_out[0] = k2
    v_out[0] = vr
    kk_out[0] = kk
    kka_out[0] = kk * a
    g_out[0] = g
    bonus_out[0] = _seg_dot(r * k2 * rk_ref[...], mones) * vr


def _rwkv_prep(z_rwkv, mu, w0, w_up, a0, a_up, g_up, k_k, k_a, r_k, mones):
    B, S, _ = z_rwkv.shape
    ts = ROW_TILE
    row = lambda t: t.reshape(1, -1)
    args = (row(mu), row(w0), w_up.astype(BF16), row(a0), a_up.astype(BF16),
            g_up.astype(BF16), row(k_k), row(k_a), row(r_k), mones)
    out_spec = pl.BlockSpec((1, ts, WIDTH), lambda b, s: (b, s, 0))
    return pl.pallas_call(
        _rwkv_prep_kernel,
        grid=(B, S // ts),
        in_specs=[pl.BlockSpec((1, ts, RWKV_COLS), lambda b, s: (b, s, 0))]
        + [_full(a.shape) for a in args],
        out_specs=[out_spec] * 8,
        out_shape=[jax.ShapeDtypeStruct((B, S, WIDTH), F32)] * 8,
        scratch_shapes=[pltpu.VMEM((8, RWKV_COLS), F32)],
        compiler_params=_cparams(("parallel", "arbitrary")),
        name="rwkv_prep",
    )(z_rwkv, *args)


def _wkv7_kernel(r_ref, lw_ref, k_ref, v_ref, kk_ref, kka_ref, tri_ref, o_ref, state_ref):
    @pl.when(pl.program_id(1) == 0)
    def _():
        state_ref[...] = jnp.zeros_like(state_ref)

    L = WKV_CHUNK
    tri = tri_ref[...]
    ri = lax.broadcasted_iota(jnp.int32, (2 * L, L), 0)
    ci = lax.broadcasted_iota(jnp.int32, (2 * L, L), 1)
    mask = jnp.where(ri < L, ri, ri - L + 1) > ci
    eye = (lax.broadcasted_iota(jnp.int32, (L, L), 0)
           == lax.broadcasted_iota(jnp.int32, (L, L), 1)).astype(F32)

    ar, vf, bt, kt, bke, gl = [], [], [], [], [], []
    for b in range(WKV_BATCH):
        lw = lw_ref[b]
        hi = lw.astype(BF16)
        rem = lw - hi.astype(F32)
        mid = rem.astype(BF16)
        lo = (rem - mid.astype(F32)).astype(BF16)
        cum = (jnp.dot(tri, hi, preferred_element_type=F32)
               + jnp.dot(tri, mid, preferred_element_type=F32)
               + jnp.dot(tri, lo, preferred_element_type=F32))
        cum_last = cum[L - 1:L, :]
        inv_g = jnp.exp(-cum)
        to_end = jnp.exp(cum_last - cum)
        g_last = jnp.exp(cum_last)
        kk = kk_ref[b]
        kka = kka_ref[b]
        k = k_ref[b]
        v = v_ref[b]
        a_t = -kk * jnp.exp(cum - lw)
        b_t = kka * inv_g
        k_t = k * inv_g
        r_t = r_ref[b] * jnp.exp(cum)
        b_end = kka * to_end
        k_end = k * to_end
        for h in range(N_HEADS):
            sl = slice(h * HEAD_DIM, (h + 1) * HEAD_DIM)
            ar.append(jnp.concatenate([a_t[:, sl], r_t[:, sl]], axis=0).astype(BF16))
            vf.append(v[:, sl])
            bt.append(b_t[:, sl])
            kt.append(k_t[:, sl])
            bke.append(jnp.concatenate([b_end[:, sl], k_end[:, sl]], axis=0))
            gl.append(g_last[:, sl])
    chains = range(WKV_BATCH * N_HEADS)
    vs = [x.astype(BF16) for x in vf]
    abrb = [jnp.where(mask, _bdot_nt(ar[c], bt[c]), 0.0) for c in chains]
    akrk = [jnp.where(mask, _bdot_nt(ar[c], kt[c]), 0.0) for c in chains]
    p = [m[:L] for m in abrb]
    inv = [eye + m for m in p]
    for _ in range(int(math.log2(L)) - 1):
        p = [_bdot(m, m) for m in p]
        inv = [inv[c] + _bdot(inv[c], p[c]) for c in chains]
    states = [state_ref[c] for c in chains]
    xs = [_bdot_nt(ar[c], states[c]) for c in chains]
    kv = [_bdot(akrk[c], vs[c]) for c in chains]
    u = [_bdot(inv[c], xs[c][:L] + kv[c][:L]) for c in chains]
    y = [xs[c][L:] + kv[c][L:] + _bdot(abrb[c][L:], u[c]) for c in chains]
    for c in chains:
        uv = jnp.concatenate([u[c], vf[c]], axis=0)
        state_ref[c] = states[c] * gl[c] + _bdot_tn(uv, bke[c])
    for b in range(WKV_BATCH):
        o_ref[b] = jnp.concatenate(y[b * N_HEADS:(b + 1) * N_HEADS], axis=1)


def _wkv7(r, lw, k, v, kk, kka):
    B, S, _ = r.shape
    L = WKV_CHUNK
    nb = WKV_BATCH
    tri = jnp.tril(jnp.ones((L, L), F32)).astype(BF16)
    spec = pl.BlockSpec((nb, L, WIDTH), lambda b, c: (b, c, 0))
    return pl.pallas_call(
        _wkv7_kernel,
        grid=(B // nb, S // L),
        in_specs=[spec] * 6 + [_full((L, L))],
        out_specs=spec,
        out_shape=jax.ShapeDtypeStruct((B, S, WIDTH), F32),
        scratch_shapes=[pltpu.VMEM((nb * N_HEADS, HEAD_DIM, HEAD_DIM), F32)],
        compiler_params=_cparams(("parallel", "arbitrary")),
        name="wkv7",
    )(r, lw, k, v, kk, kka, tri)


def _merge_kernel(x_ref, yret_ref, o_ref, bonus_ref, g_ref, zg_ref, gng_ref, gnb_ref,
                  mavg_ref, wret_ref, wrwkv_ref, wo_ref, x1_ref):
    y = _head_norm(o_ref[...], mavg_ref[...], RWKV_GN_EPS)
    y_rwkv = (y * gng_ref[...] + gnb_ref[...] + bonus_ref[...]) * g_ref[...]
    br = _bdot(yret_ref[...], wret_ref[...])
    bw = _bdot(y_rwkv, wrwkv_ref[...])
    zg = zg_ref[...]
    d = br.shape[1]
    merged = _sigmoid(zg[:, :d]) * br + _sigmoid(zg[:, d:]) * bw
    x1_ref[...] = x_ref[...] + _bdot(merged, wo_ref[...])


def _merge(x2, y_ret, o, bonus, g, z_gate, gn_g, gn_b, mavg, w_ret_br, w_rwkv_br, w_o):
    T, D = x2.shape
    tm = ROW_TILE
    rows = lambda n: pl.BlockSpec((tm, n), lambda i: (i, 0))
    consts = (gn_g, gn_b, mavg, w_ret_br, w_rwkv_br, w_o)
    return pl.pallas_call(
        _merge_kernel,
        grid=(T // tm,),
        in_specs=[rows(D), rows(WIDTH), rows(WIDTH), rows(WIDTH), rows(WIDTH), rows(2 * D)]
        + [_full(c.shape) for c in consts],
        out_specs=rows(D),
        out_shape=jax.ShapeDtypeStruct((T, D), F32),
        compiler_params=_cparams(("parallel",)),
        name="merge",
    )(x2, y_ret, o, bonus, g, z_gate, *consts)


def _peer_scores_kernel(x_ref, g_ref, wq_ref, keys_ref, h_ref, s_ref):
    h = _rms(x_ref[...], g_ref[...])
    h_ref[...] = h.reshape(h_ref.shape)
    q = _bdot(h, wq_ref[...]).astype(BF16)
    for grp in range(PEER_GROUPS):
        s_ref[grp] = lax.dot_general(keys_ref[grp], q[:, grp * PEER_HALF:(grp + 1) * PEER_HALF],
                                     _NT, preferred_element_type=F32)


def _peer_scores(x1, g_ffn, w_pq, keys):
    T, D = x1.shape
    tm = ROW_TILE
    return pl.pallas_call(
        _peer_scores_kernel,
        grid=(T // tm,),
        in_specs=[pl.BlockSpec((tm, D), lambda i: (i, 0)), _full((1, D)), _full(w_pq.shape),
                  _full(keys.shape)],
        out_specs=[pl.BlockSpec((tm, D // 128, 128), lambda i: (i, 0, 0)),
                   pl.BlockSpec((PEER_GROUPS, PEER_N_KEYS, tm), lambda i: (0, 0, i))],
        out_shape=[jax.ShapeDtypeStruct((T, D // 128, 128), F32),
                   jax.ShapeDtypeStruct((PEER_GROUPS, PEER_N_KEYS, T), F32)],
        compiler_params=_cparams(("parallel",)),
        name="peer_scores",
    )(x1, g_ffn, w_pq, keys)


def _top_rows_steps(s, count, payload=None):
    rows = lax.broadcasted_iota(jnp.int32, s.shape, 0).astype(F32)
    vals, picks = [], []
    for _ in range(count):
        m = jnp.max(s, axis=0, keepdims=True)
        idx = jnp.min(jnp.where(s == m, rows, float(s.shape[0])), axis=0, keepdims=True)
        hit = rows == idx
        s = jnp.where(hit, -jnp.inf, s)
        vals.append(m)
        if payload is None:
            picks.append(idx)
        else:
            picks.append(jnp.max(jnp.where(hit, payload, -1), axis=0, keepdims=True))
        yield
    return jnp.concatenate(vals, axis=0), jnp.concatenate(picks, axis=0)


def _run(steps):
    try:
        while True:
            next(steps)
    except StopIteration as done:
        return done.value


def _pair_rows(a, b):
    K = PEER_TOPK
    out = []
    for i in range(K // 2):
        jn = K if i == 0 else K // 2
        out.append((a[i:i + 1, :], b[0:jn, :]))
    out.append((a[K // 2:K, :], b[0:1, :]))
    return out


def _peer_topk_kernel(s_ref, ids_ref, gates_ref):
    K = PEER_TOPK

    def head(h, carry):
        ids, gates = _run(_route_head_steps(lambda: s_ref[2 * h], lambda: s_ref[2 * h + 1]))
        off = pl.multiple_of(h * K, K)
        ids_ref[pl.ds(off, K), :] = ids
        gates_ref[pl.ds(off, K), :] = gates
        return carry

    lax.fori_loop(0, PEER_HEADS, head, 0)


def _route_head_steps(load_scores0, load_scores1):
    K = PEER_TOPK
    s0, i0 = yield from _top_rows_steps(load_scores0(), K)
    s1, i1 = yield from _top_rows_steps(load_scores1(), K)
    e0 = i0.astype(jnp.int32) * PEER_N_KEYS
    e1 = i1.astype(jnp.int32)
    cand = jnp.concatenate([x + y for x, y in _pair_rows(s0, s1)], axis=0)
    cand_id = jnp.concatenate([x + y for x, y in _pair_rows(e0, e1)], axis=0)
    best, ids = yield from _top_rows_steps(cand, K, payload=cand_id)
    e = jnp.exp(best - best[0:1, :])
    return ids, e / jnp.sum(e, axis=0, keepdims=True)


def _peer_topk(scores_t, n_tokens):
    T = n_tokens
    tk = PEER_ROUTE_TOK
    return pl.pallas_call(
        _peer_topk_kernel,
        grid=(T // tk,),
        in_specs=[pl.BlockSpec((PEER_GROUPS, PEER_N_KEYS, tk), lambda i: (0, 0, i))],
        out_specs=[pl.BlockSpec((PEER_SEL, tk), lambda i: (0, i))] * 2,
        out_shape=[jax.ShapeDtypeStruct((PEER_SEL, T), jnp.int32),
                   jax.ShapeDtypeStruct((PEER_SEL, T), F32)],
        compiler_params=_cparams(("parallel",)),
        name="peer_topk",
    )(scores_t)


def _gelu_tanh(x):
    return 0.5 * x * (1.0 + jnp.tanh(math.sqrt(2.0 / math.pi) * (x + 0.044715 * x * x * x)))


def _peer_experts_kernel(ids_cur, ids_next, gexp_ref, h_ref, uv_hbm, s_ref, out_ref, nids_ref,
                         ngates_ref, buf, sem):
    i = pl.program_id(0)
    n = pl.num_programs(0)
    sub = h_ref.shape[1]
    group_rows = PEER_TOK * PEER_SEL

    def issue_token(ids_ref, tok, group, t, k0=0, count=PEER_SEL):
        for k in range(k0, k0 + count):
            e = ids_ref[0, tok, k]
            pltpu.make_async_copy(uv_hbm.at[pl.ds(e, 1)],
                                  buf.at[group, pl.ds(t * PEER_SEL + k, 1)],
                                  sem.at[group]).start(priority=k % 2)

    def wait_group(group):
        pltpu.make_async_copy(uv_hbm.at[pl.ds(0, group_rows)], buf.at[group], sem.at[group]).wait()

    @pl.when(i == 0)
    def _():
        for group in range(PEER_AHEAD):
            def body(t, carry, group=group):
                issue_token(ids_cur, group * PEER_TOK + t, group, t)
                return carry
            lax.fori_loop(0, PEER_TOK, body, 0)

    ch = PEER_CHUNK
    n_ch = PEER_SEL // ch
    ccols = ch * sub
    lane = lax.broadcasted_iota(jnp.int32, (sub, ccols), 1)
    diag = (lane % sub) == lax.broadcasted_iota(jnp.int32, (sub, ccols), 0)

    def group_sum(x):
        step = 1
        while step < sub:
            partner = jnp.where((lane % (2 * step)) < step, pltpu.roll(x, ccols - step, 1),
                                pltpu.roll(x, step, 1))
            x = x + partner
            step *= 2
        return x

    tokens = PEER_GROUPS_PER_STEP * PEER_TOK
    burst = PEER_SEL // (2 * n_ch)
    issued = [0]

    def routing_steps():
        routed = []
        for j in range(s_ref.shape[0] // 2):
            routed.append((yield from _route_head_steps(lambda j=j: s_ref[2 * j],
                                                        lambda j=j: s_ref[2 * j + 1])))
        return routed

    routing = routing_steps()
    routed = []

    def routing_tick():
        if not routed:
            try:
                next(routing)
            except StopIteration as done:
                routed.extend(done.value)

    def request_burst():
        for r in range(issued[0], issued[0] + burst):
            tok, k = divmod(r, PEER_SEL)
            ahead = tok // PEER_TOK + PEER_AHEAD
            ids_ref = ids_cur if ahead < PEER_GROUPS_PER_STEP else ids_next
            group = ahead % PEER_GROUPS_PER_STEP
            issue_token(ids_ref, group * PEER_TOK + tok % PEER_TOK, group, tok % PEER_TOK, k, 1)
        issued[0] += burst
        if (issued[0] // burst) % PEER_ROUTE_EVERY == 0:
            routing_tick()

    def first_layer(tok):
        group, t = divmod(tok, PEER_TOK)
        if t == 0:
            wait_group(group)
        hb = h_ref[tok].astype(BF16)
        ps = []
        for c in range(n_ch):
            request_burst()
            rows = pl.ds(t * PEER_SEL + c * ch, ch)
            uc = buf[group, rows, 0:sub, :].reshape(ccols, 128).astype(BF16)
            ps.append(lax.dot_general(hb, uc, _NT, preferred_element_type=F32))
        return ps

    def weights(tok, ps):
        ws = []
        for c in range(n_ch):
            act = jnp.sum(group_sum(jnp.where(diag, ps[c], 0.0)), axis=0, keepdims=True)
            w = _gelu_tanh(act) * gexp_ref[tok:tok + 1, c * ccols:(c + 1) * ccols]
            ws.append(jnp.where(diag, jnp.broadcast_to(w, (sub, ccols)), 0.0).astype(BF16))
        return ws

    def second_layer(tok, ws):
        group, t = divmod(tok, PEER_TOK)
        o = None
        for c in range(n_ch):
            request_burst()
            rows = pl.ds(t * PEER_SEL + c * ch, ch)
            vc = buf[group, rows, sub:2 * sub, :].reshape(ccols, 128).astype(BF16)
            part = jnp.dot(ws[c], vc, preferred_element_type=F32)
            o = part if o is None else o + part
        return o

    ps = {tok: first_layer(tok) for tok in range(PEER_SKEW)}
    outs = []
    for tok in range(tokens):
        if tok + PEER_SKEW < tokens:
            ps[tok + PEER_SKEW] = first_layer(tok + PEER_SKEW)
        outs.append(second_layer(tok, weights(tok, ps.pop(tok))))
    assert issued[0] == tokens * PEER_SEL
    while not routed:
        routing_tick()
    for tok, o in enumerate(outs):
        out_ref[tok] = o
    for j, (ids, gates) in enumerate(routed):
        nids_ref[j * PEER_TOPK:(j + 1) * PEER_TOPK, :] = ids
        ngates_ref[j * PEER_TOPK:(j + 1) * PEER_TOPK, :] = gates

    @pl.when(i == n - 1)
    def _():
        for group in range(PEER_AHEAD):
            wait_group(group)


def _peer_experts(ids, gates, h_tiles, uv, scores_next):
    T, sub, _ = h_tiles.shape
    tb = PEER_GROUPS_PER_STEP * PEER_TOK
    nb = T // tb
    rsteps = PEER_ROUTE_TOK // tb
    rgroups = PEER_GROUPS // rsteps
    rrows = PEER_SEL // rsteps
    ids3 = ids.reshape(nb, tb, PEER_SEL)
    gexp = jnp.repeat(gates, sub, axis=1)
    smem = lambda imap: pl.BlockSpec((1, tb, PEER_SEL), imap, memory_space=pltpu.SMEM)
    tiles = pl.BlockSpec((tb, sub, 128), lambda i: (i, 0, 0))
    routed = pl.BlockSpec((rrows, PEER_ROUTE_TOK), lambda i: (i % rsteps, i // rsteps))
    return pl.pallas_call(
        _peer_experts_kernel,
        grid=(nb,),
        in_specs=[smem(lambda i: (i, 0, 0)),
                  smem(lambda i: (jnp.minimum(i + 1, nb - 1), 0, 0)),
                  pl.BlockSpec((tb, PEER_SEL * sub), lambda i: (i, 0)),
                  tiles,
                  pl.BlockSpec(memory_space=pl.ANY),
                  pl.BlockSpec((rgroups, PEER_N_KEYS, PEER_ROUTE_TOK),
                               lambda i: (i % rsteps, 0, i // rsteps))],
        out_specs=[tiles, routed, routed],
        out_shape=[jax.ShapeDtypeStruct((T, sub, 128), F32),
                   jax.ShapeDtypeStruct((PEER_SEL, T), jnp.int32),
                   jax.ShapeDtypeStruct((PEER_SEL, T), F32)],
        scratch_shapes=[pltpu.VMEM((PEER_GROUPS_PER_STEP, PEER_TOK * PEER_SEL, 2 * sub, 128), F32),
                        pltpu.SemaphoreType.DMA((PEER_GROUPS_PER_STEP,))],
        compiler_params=_cparams(("arbitrary",)),
        name="peer_experts",
    )(ids3, ids3, gexp, h_tiles, uv, scores_next)


def _ple_final_kernel(x_ref, d_ref, p_ref, gple_ref, wg_ref, wu_ref, gfin_ref, out_ref):
    x = x_ref[...] + d_ref[...].reshape(x_ref.shape)
    gate = _sigmoid(_bdot(_rms(x, gple_ref[...]), wg_ref[...]))
    x = x + gate * _bdot(p_ref[...], wu_ref[...])
    out_ref[...] = _rms(x, gfin_ref[...])


def _ple_final(x2, delta_tiles, p2, g_ple, w_gate, w_up, g_final):
    T, D = x2.shape
    tm = ROW_TILE
    consts = (g_ple, w_gate, w_up, g_final)
    return pl.pallas_call(
        _ple_final_kernel,
        grid=(T // tm,),
        in_specs=[pl.BlockSpec((tm, D), lambda i: (i, 0)),
                  pl.BlockSpec((tm,) + delta_tiles.shape[1:], lambda i: (i, 0, 0)),
                  pl.BlockSpec((tm, p2.shape[1]), lambda i: (i, 0)),
                  _full(g_ple.shape), _full(w_gate.shape), _full(w_up.shape), _full(g_final.shape)],
        out_specs=pl.BlockSpec((tm, D), lambda i: (i, 0)),
        out_shape=jax.ShapeDtypeStruct((T, D), F32),
        compiler_params=_cparams(("parallel",)),
        name="ple_final",
    )(x2, delta_tiles, p2, *consts)


def kernel(x, p, g_mix, w_in, ret_gn_g, rwkv_mu, rwkv_w0, rwkv_w_up, rwkv_a0, rwkv_a_up, rwkv_g_up, rwkv_k_k, rwkv_k_a, rwkv_r_k, rwkv_gn_g, rwkv_gn_b, w_ret_br, w_rwkv_br, w_o, g_ffn, w_pq, peer_sub_keys, peer_u, peer_v, g_ple, w_ple_gate, w_ple_up, g_final):
    B, S, D = x.shape
    T = B * S
    assert w_in.shape[0] == 1, "single-layer block: the final RMSNorm is fused into its last step"
    i = 0
    row = lambda t: t.reshape(1, -1)
    head_of = jnp.arange(WIDTH) // HEAD_DIM
    same_head = head_of[:, None] == head_of[None, :]
    mones = same_head.astype(BF16)
    mavg = (same_head.astype(F32) / HEAD_DIM).astype(BF16)
    ret_cols = 4 * WIDTH
    x2 = x.reshape(T, D)
    wi = w_in[i].astype(BF16)
    z_ret, z_rwkv, z_gate = _in_proj(
        x2, row(g_mix[i]), wi[:, :ret_cols], wi[:, ret_cols:ret_cols + RWKV_COLS],
        wi[:, ret_cols + RWKV_COLS:])
    y_ret = _retention(z_ret.reshape(B, S, ret_cols), row(ret_gn_g[i]), mavg)
    r, lw, k, v, kk, kka, g, bonus = _rwkv_prep(
        z_rwkv.reshape(B, S, RWKV_COLS), rwkv_mu[i], rwkv_w0[i], rwkv_w_up[i], rwkv_a0[i],
        rwkv_a_up[i], rwkv_g_up[i], rwkv_k_k[i], rwkv_k_a[i], rwkv_r_k[i], mones)
    o = _wkv7(r, lw, k, v, kk, kka)
    flat = lambda t: t.reshape(T, WIDTH)
    x2 = _merge(x2, flat(y_ret), flat(o), flat(bonus), flat(g), z_gate,
                row(rwkv_gn_g[i]), row(rwkv_gn_b[i]), mavg, w_ret_br[i].astype(BF16),
                w_rwkv_br[i].astype(BF16), w_o[i].astype(BF16))
    keys = peer_sub_keys[i].reshape(PEER_GROUPS, PEER_N_KEYS, PEER_HALF).astype(BF16)
    h_tiles, scores_t = _peer_scores(x2, row(g_ffn[i]), w_pq[i].astype(BF16), keys)
    n_exp = peer_u.shape[1]
    uv = jnp.concatenate([peer_u[i].reshape(n_exp, D // 128, 128),
                          peer_v[i].reshape(n_exp, D // 128, 128)], axis=1)
    R = T // PEER_RANGES
    ids_t, gates_t = _peer_topk(scores_t, R)
    deltas = []
    for c in range(PEER_RANGES):
        nxt = min(c + 1, PEER_RANGES - 1)
        delta, ids_t, gates_t = _peer_experts(
            ids_t.T, gates_t.T, h_tiles[c * R:(c + 1) * R], uv, scores_t[:, :, nxt * R:(nxt + 1) * R])
        deltas.append(delta)
    delta_tiles = jnp.concatenate(deltas, axis=0)
    out = _ple_final(x2, delta_tiles, p[i].reshape(T, -1), row(g_ple[i]),
                     w_ple_gate[i].astype(BF16), w_ple_up[i].astype(BF16), row(g_final))
    return out.reshape(B, S, D)
```

```python
import math

import jax
import jax.numpy as jnp
from jax import lax
from jax.experimental import pallas as pl
from jax.experimental.pallas import tpu as pltpu

F32 = jnp.float32
BF16 = jnp.bfloat16

RMS_EPS = 1e-6
HEAD_DIM = 64
N_HEADS = 8
WIDTH = N_HEADS * HEAD_DIM
RET_CHUNK = 128
RET_BATCH = 2
RET_GN_EPS = 1e-5
ROPE_BASE = 10000.0
RWKV_GN_EPS = 64e-5
L2_EPS = 1e-12
DECAY_LORA = 64
AAA_LORA = 64
GATE_LORA = 128
RWKV_COLS = 3 * WIDTH + DECAY_LORA + AAA_LORA + GATE_LORA
WKV_CHUNK = 64
WKV_BATCH = 4

PEER_HEADS = 8
PEER_N_KEYS = 128
PEER_HALF = 128
PEER_TOPK = 16
PEER_GROUPS = 2 * PEER_HEADS
PEER_SEL = PEER_HEADS * PEER_TOPK
PEER_ROUTE_TOK = 128
PEER_ROUTE_EVERY = 2
PEER_RANGES = 8
PEER_TOK = 8
PEER_GROUPS_PER_STEP = 4
PEER_AHEAD = 2
PEER_SKEW = 2
PEER_CHUNK = 32

ROW_TILE = 256
VMEM_LIMIT = 48 * 1024 * 1024

_NT = (((1,), (1,)), ((), ()))
_TN = (((0,), (0,)), ((), ()))


def _bdot(a, b):
    return jnp.dot(a.astype(BF16), b.astype(BF16), preferred_element_type=F32)


def _bdot_nt(a, b):
    return lax.dot_general(a.astype(BF16), b.astype(BF16), _NT, preferred_element_type=F32)


def _bdot_tn(a, b):
    return lax.dot_general(a.astype(BF16), b.astype(BF16), _TN, preferred_element_type=F32)


def _split2(a):
    hi = a.astype(BF16)
    lo = (a - hi.astype(F32)).astype(BF16)
    return hi, lo


def _seg_dot(a, m):
    hi, lo = _split2(a)
    return (jnp.dot(hi, m, preferred_element_type=F32)
            + jnp.dot(lo, m, preferred_element_type=F32))


def _rms(x, g):
    return x * lax.rsqrt(jnp.mean(x * x, axis=-1, keepdims=True) + RMS_EPS) * g


def _sigmoid(x):
    return 1.0 / (1.0 + jnp.exp(-x))


def _head_norm(o, mavg, eps):
    mu = _seg_dot(o, mavg)
    oc = o - mu
    var = _seg_dot(oc * oc, mavg)
    return oc * lax.rsqrt(var + eps)


def _cparams(sem, vmem=VMEM_LIMIT):
    return pltpu.CompilerParams(dimension_semantics=sem, vmem_limit_bytes=vmem)


def _full(shape):
    nd = len(shape)
    return pl.BlockSpec(shape, lambda *_: (0,) * nd)


def _in_proj_kernel(x_ref, g_ref, w1_ref, w2_ref, w3_ref, o1_ref, o2_ref, o3_ref):
    h = _rms(x_ref[...], g_ref[...]).astype(BF16)
    o1_ref[...] = jnp.dot(h, w1_ref[...], preferred_element_type=F32)
    o2_ref[...] = jnp.dot(h, w2_ref[...], preferred_element_type=F32)
    o3_ref[...] = jnp.dot(h, w3_ref[...], preferred_element_type=F32)


def _in_proj(x2, g, w_ret, w_rwkv, w_gate):
    T, D = x2.shape
    tm = ROW_TILE
    ws = (w_ret, w_rwkv, w_gate)
    return pl.pallas_call(
        _in_proj_kernel,
        grid=(T // tm,),
        in_specs=[pl.BlockSpec((tm, D), lambda i: (i, 0)), _full((1, D))]
        + [_full(w.shape) for w in ws],
        out_specs=[pl.BlockSpec((tm, w.shape[1]), lambda i: (i, 0)) for w in ws],
        out_shape=[jax.ShapeDtypeStruct((T, w.shape[1]), F32) for w in ws],
        compiler_params=_cparams(("parallel",)),
        name="in_proj",
    )(x2, g, *ws)


def _retention_kernel(z_ref, cos_ref, sin_ref, xi_ref, zeta_ref, decay_ref, cd_ref,
                      gn_ref, mavg_ref, y_ref, state_ref):
    @pl.when(pl.program_id(1) == 0)
    def _():
        state_ref[...] = jnp.zeros_like(state_ref)

    cos = cos_ref[...]
    sin = sin_ref[...]
    lane = lax.broadcasted_iota(jnp.int32, cos.shape, 1)
    first_half = (lane % HEAD_DIM) < (HEAD_DIM // 2)

    def rot(x):
        partner = jnp.where(first_half, pltpu.roll(x, WIDTH - HEAD_DIM // 2, 1),
                            pltpu.roll(x, HEAD_DIM // 2, 1))
        return x * cos + partner * sin

    qs, ks, vs, qxs, kzs, grs = [], [], [], [], [], []
    for b in range(RET_BATCH):
        z = z_ref[b]
        v = z[:, 2 * WIDTH:3 * WIDTH]
        grs.append(z[:, 3 * WIDTH:4 * WIDTH])
        qr = rot(z[:, 0:WIDTH])
        kr = rot(z[:, WIDTH:2 * WIDTH]) * (HEAD_DIM ** -0.5)
        qx = qr * xi_ref[...]
        kz = kr * zeta_ref[...]
        for h in range(N_HEADS):
            sl = slice(h * HEAD_DIM, (h + 1) * HEAD_DIM)
            qs.append(qr[:, sl].astype(BF16))
            ks.append(kr[:, sl].astype(BF16))
            vs.append(v[:, sl].astype(BF16))
            qxs.append(qx[:, sl])
            kzs.append(kz[:, sl])
    chains = range(RET_BATCH * N_HEADS)
    scores = [_bdot_nt(qs[c], ks[c]) * decay_ref[c % N_HEADS] for c in chains]
    states = [state_ref[c] for c in chains]
    cross = [_bdot(qxs[c], states[c]) for c in chains]
    outs = [_bdot(scores[c], vs[c]) + cross[c] for c in chains]
    for c in chains:
        state_ref[c] = states[c] * cd_ref[c % N_HEADS] + _bdot_tn(kzs[c], vs[c])
    for b in range(RET_BATCH):
        o = jnp.concatenate(outs[b * N_HEADS:(b + 1) * N_HEADS], axis=1)
        y = _head_norm(o, mavg_ref[...], RET_GN_EPS)
        y_ref[b] = grs[b] * _sigmoid(grs[b]) * (y * gn_ref[...])


def _retention(z_ret, gn_g, mavg):
    B, S, _ = z_ret.shape
    C = RET_CHUNK
    half = HEAD_DIM // 2
    inv_freq = ROPE_BASE ** (-jnp.arange(half, dtype=F32) * 2.0 / HEAD_DIM)
    ang = jnp.arange(S, dtype=F32)[:, None] * inv_freq[None, :]
    cos_h = jnp.concatenate([jnp.cos(ang), jnp.cos(ang)], axis=1)
    sin_h = jnp.concatenate([-jnp.sin(ang), jnp.sin(ang)], axis=1)
    cos = jnp.tile(cos_h, (1, N_HEADS))
    sin = jnp.tile(sin_h, (1, N_HEADS))
    log_gamma = jnp.log1p(-(2.0 ** (-5.0 - jnp.arange(N_HEADS, dtype=F32))))
    idx = jnp.arange(C, dtype=F32)
    diff = idx[:, None] - idx[None, :]
    causal = diff >= 0
    decay = jnp.where(causal[None], jnp.exp(log_gamma[:, None, None] * jnp.where(causal, diff, 0.0)[None]), 0.0)
    zeta = jnp.exp(log_gamma[:, None] * (C - 1.0 - idx)[None, :])
    xi = jnp.exp(log_gamma[:, None] * (idx + 1.0)[None, :])
    widen = lambda t: jnp.repeat(t.T, HEAD_DIM, axis=1)
    cd = jnp.broadcast_to(jnp.exp(log_gamma * C)[:, None, None], (N_HEADS, HEAD_DIM, HEAD_DIM))
    return pl.pallas_call(
        _retention_kernel,
        grid=(B // RET_BATCH, S // C),
        in_specs=[pl.BlockSpec((RET_BATCH, C, 4 * WIDTH), lambda b, c: (b, c, 0)),
                  pl.BlockSpec((C, WIDTH), lambda b, c: (c, 0)),
                  pl.BlockSpec((C, WIDTH), lambda b, c: (c, 0)),
                  _full((C, WIDTH)), _full((C, WIDTH)), _full((N_HEADS, C, C)),
                  _full((N_HEADS, HEAD_DIM, HEAD_DIM)), _full((1, WIDTH)), _full((WIDTH, WIDTH))],
        out_specs=pl.BlockSpec((RET_BATCH, C, WIDTH), lambda b, c: (b, c, 0)),
        out_shape=jax.ShapeDtypeStruct((B, S, WIDTH), F32),
        scratch_shapes=[pltpu.VMEM((RET_BATCH * N_HEADS, HEAD_DIM, HEAD_DIM), F32)],
        compiler_params=_cparams(("parallel", "arbitrary")),
        name="retention",
    )(z_ret, cos, sin, widen(xi), widen(zeta), decay, cd, gn_g, mavg)


def _rwkv_prep_kernel(z_ref, mu_ref, w0_ref, wup_ref, a0_ref, aup_ref, gup_ref, kk_ref,
                      ka_ref, rk_ref, mones_ref,
                      r_out, lw_out, k_out, v_out, kk_out, kka_out, g_out, bonus_out,
                      carry_ref):
    @pl.when(pl.program_id(1) == 0)
    def _():
        carry_ref[...] = jnp.zeros_like(carry_ref)

    z = z_ref[0]
    n = z.shape[0]
    row = lax.broadcasted_iota(jnp.int32, z.shape, 0)
    prev = jnp.where(row == 0, carry_ref[0:1, :], pltpu.roll(z, 1, 0))
    carry_ref[0:1, :] = z[n - 1:n, :]
    zs = z + (prev - z) * mu_ref[...]
    r = zs[:, 0:WIDTH]
    kr = zs[:, WIDTH:2 * WIDTH]
    vr = zs[:, 2 * WIDTH:3 * WIDTH]
    o = 3 * WIDTH
    wl = zs[:, o:o + DECAY_LORA]
    al = zs[:, o + DECAY_LORA:o + DECAY_LORA + AAA_LORA]
    gl = zs[:, o + DECAY_LORA + AAA_LORA:]
    t = -(w0_ref[...] + _bdot(jnp.tanh(wl), wup_ref[...]))
    softplus = jnp.maximum(t, 0.0) + jnp.log1p(jnp.exp(-jnp.abs(t)))
    w_log = -softplus - 0.5
    a = _sigmoid(a0_ref[...] + _bdot(al, aup_ref[...]))
    g = _bdot(_sigmoid(gl), gup_ref[...])
    mones = mones_ref[...]
    kk = kr * kk_ref[...]
    norm = jnp.sqrt(_seg_dot(kk * kk, mones))
    kk = kk / jnp.maximum(norm, L2_EPS)
    k2 = kr * (1.0 + (a - 1.0) * ka_ref[...])
    r_out[0] = r
    lw_out[0] = -jnp.exp(w_log)
    k_out[0] = k2
    v_out[0] = vr
    kk_out[0] = kk
    kka_out[0] = kk * a
    g_out[0] = g
    bonus_out[0] = _seg_dot(r * k2 * rk_ref[...], mones) * vr


def _rwkv_prep(z_rwkv, mu, w0, w_up, a0, a_up, g_up, k_k, k_a, r_k, mones):
    B, S, _ = z_rwkv.shape
    ts = ROW_TILE
    row = lambda t: t.reshape(1, -1)
    args = (row(mu), row(w0), w_up.astype(BF16), row(a0), a_up.astype(BF16),
            g_up.astype(BF16), row(k_k), row(k_a), row(r_k), mones)
    out_spec = pl.BlockSpec((1, ts, WIDTH), lambda b, s: (b, s, 0))
    return pl.pallas_call(
        _rwkv_prep_kernel,
        grid=(B, S // ts),
        in_specs=[pl.BlockSpec((1, ts, RWKV_COLS), lambda b, s: (b, s, 0))]
        + [_full(a.shape) for a in args],
        out_specs=[out_spec] * 8,
        out_shape=[jax.ShapeDtypeStruct((B, S, WIDTH), F32)] * 8,
        scratch_shapes=[pltpu.VMEM((8, RWKV_COLS), F32)],
        compiler_params=_cparams(("parallel", "arbitrary")),
        name="rwkv_prep",
    )(z_rwkv, *args)


def _wkv7_kernel(r_ref, lw_ref, k_ref, v_ref, kk_ref, kka_ref, tri_ref, o_ref, state_ref):
    @pl.when(pl.program_id(1) == 0)
    def _():
        state_ref[...] = jnp.zeros_like(state_ref)

    L = WKV_CHUNK
    tri = tri_ref[...]
    ri = lax.broadcasted_iota(jnp.int32, (2 * L, L), 0)
    ci = lax.broadcasted_iota(jnp.int32, (2 * L, L), 1)
    mask = jnp.where(ri < L, ri, ri - L + 1) > ci
    eye = (lax.broadcasted_iota(jnp.int32, (L, L), 0)
           == lax.broadcasted_iota(jnp.int32, (L, L), 1)).astype(F32)

    ar, vf, bt, kt, bke, gl = [], [], [], [], [], []
    for b in range(WKV_BATCH):
        lw = lw_ref[b]
        hi = lw.astype(BF16)
        rem = lw - hi.astype(F32)
        mid = rem.astype(BF16)
        lo = (rem - mid.astype(F32)).astype(BF16)
        cum = (jnp.dot(tri, hi, preferred_element_type=F32)
               + jnp.dot(tri, mid, preferred_element_type=F32)
               + jnp.dot(tri, lo, preferred_element_type=F32))
        cum_last = cum[L - 1:L, :]
        inv_g = jnp.exp(-cum)
        to_end = jnp.exp(cum_last - cum)
        g_last = jnp.exp(cum_last)
        kk = kk_ref[b]
        kka = kka_ref[b]
        k = k_ref[b]
        v = v_ref[b]
        a_t = -kk * jnp.exp(cum - lw)
        b_t = kka * inv_g
        k_t = k * inv_g
        r_t = r_ref[b] * jnp.exp(cum)
        b_end = kka * to_end
        k_end = k * to_end
        for h in range(N_HEADS):
            sl = slice(h * HEAD_DIM, (h + 1) * HEAD_DIM)
            ar.append(jnp.concatenate([a_t[:, sl], r_t[:, sl]], axis=0).astype(BF16))
            vf.append(v[:, sl])
            bt.append(b_t[:, sl])
            kt.append(k_t[:, sl])
            bke.append(jnp.concatenate([b_end[:, sl], k_end[:, sl]], axis=0))
            gl.append(g_last[:, sl])
    chains = range(WKV_BATCH * N_HEADS)
    vs = [x.astype(BF16) for x in vf]
    abrb = [jnp.where(mask, _bdot_nt(ar[c], bt[c]), 0.0) for c in chains]
    akrk = [jnp.where(mask, _bdot_nt(ar[c], kt[c]), 0.0) for c in chains]
    p = [m[:L] for m in abrb]
    inv = [eye + m for m in p]
    for _ in range(int(math.log2(L)) - 1):
        p = [_bdot(m, m) for m in p]
        inv = [inv[c] + _bdot(inv[c], p[c]) for c in chains]
    states = [state_ref[c] for c in chains]
    xs = [_bdot_nt(ar[c], states[c]) for c in chains]
    kv = [_bdot(akrk[c], vs[c]) for c in chains]
    u = [_bdot(inv[c], xs[c][:L] + kv[c][:L]) for c in chains]
    y = [xs[c][L:] + kv[c][L:] + _bdot(abrb[c][L:], u[c]) for c in chains]
    for c in chains:
        uv = jnp.concatenate([u[c], vf[c]], axis=0)
        state_ref[c] = states[c] * gl[c] + _bdot_tn(uv, bke[c])
    for b in range(WKV_BATCH):
        o_ref[b] = jnp.concatenate(y[b * N_HEADS:(b + 1) * N_HEADS], axis=1)


def _wkv7(r, lw, k, v, kk, kka):
    B, S, _ = r.shape
    L = WKV_CHUNK
    nb = WKV_BATCH
    tri = jnp.tril(jnp.ones((L, L), F32)).astype(BF16)
    spec = pl.BlockSpec((nb, L, WIDTH), lambda b, c: (b, c, 0))
    return pl.pallas_call(
        _wkv7_kernel,
        grid=(B // nb, S // L),
        in_specs=[spec] * 6 + [_full((L, L))],
        out_specs=spec,
        out_shape=jax.ShapeDtypeStruct((B, S, WIDTH), F32),
        scratch_shapes=[pltpu.VMEM((nb * N_HEADS, HEAD_DIM, HEAD_DIM), F32)],
        compiler_params=_cparams(("parallel", "arbitrary")),
        name="wkv7",
    )(r, lw, k, v, kk, kka, tri)


def _merge_kernel(x_ref, yret_ref, o_ref, bonus_ref, g_ref, zg_ref, gng_ref, gnb_ref,
                  mavg_ref, wret_ref, wrwkv_ref, wo_ref, x1_ref):
    y = _head_norm(o_ref[...], mavg_ref[...], RWKV_GN_EPS)
    y_rwkv = (y * gng_ref[...] + gnb_ref[...] + bonus_ref[...]) * g_ref[...]
    br = _bdot(yret_ref[...], wret_ref[...])
    bw = _bdot(y_rwkv, wrwkv_ref[...])
    zg = zg_ref[...]
    d = br.shape[1]
    merged = _sigmoid(zg[:, :d]) * br + _sigmoid(zg[:, d:]) * bw
    x1_ref[...] = x_ref[...] + _bdot(merged, wo_ref[...])


def _merge(x2, y_ret, o, bonus, g, z_gate, gn_g, gn_b, mavg, w_ret_br, w_rwkv_br, w_o):
    T, D = x2.shape
    tm = ROW_TILE
    rows = lambda n: pl.BlockSpec((tm, n), lambda i: (i, 0))
    consts = (gn_g, gn_b, mavg, w_ret_br, w_rwkv_br, w_o)
    return pl.pallas_call(
        _merge_kernel,
        grid=(T // tm,),
        in_specs=[rows(D), rows(WIDTH), rows(WIDTH), rows(WIDTH), rows(WIDTH), rows(2 * D)]
        + [_full(c.shape) for c in consts],
        out_specs=rows(D),
        out_shape=jax.ShapeDtypeStruct((T, D), F32),
        compiler_params=_cparams(("parallel",)),
        name="merge",
    )(x2, y_ret, o, bonus, g, z_gate, *consts)


def _peer_scores_kernel(x_ref, g_ref, wq_ref, keys_ref, h_ref, s_ref):
    h = _rms(x_ref[...], g_ref[...])
    h_ref[...] = h.reshape(h_ref.shape)
    q = _bdot(h, wq_ref[...]).astype(BF16)
    for grp in range(PEER_GROUPS):
        s_ref[grp] = lax.dot_general(keys_ref[grp], q[:, grp * PEER_HALF:(grp + 1) * PEER_HALF],
                                     _NT, preferred_element_type=F32)


def _peer_scores(x1, g_ffn, w_pq, keys):
    T, D = x1.shape
    tm = ROW_TILE
    return pl.pallas_call(
        _peer_scores_kernel,
        grid=(T // tm,),
        in_specs=[pl.BlockSpec((tm, D), lambda i: (i, 0)), _full((1, D)), _full(w_pq.shape),
                  _full(keys.shape)],
        out_specs=[pl.BlockSpec((tm, D // 128, 128), lambda i: (i, 0, 0)),
                   pl.BlockSpec((PEER_GROUPS, PEER_N_KEYS, tm), lambda i: (0, 0, i))],
        out_shape=[jax.ShapeDtypeStruct((T, D // 128, 128), F32),
                   jax.ShapeDtypeStruct((PEER_GROUPS, PEER_N_KEYS, T), F32)],
        compiler_params=_cparams(("parallel",)),
        name="peer_scores",
    )(x1, g_ffn, w_pq, keys)


def _top_rows_steps(s, count, payload=None):
    rows = lax.broadcasted_iota(jnp.int32, s.shape, 0).astype(F32)
    vals, picks = [], []
    for _ in range(count):
        m = jnp.max(s, axis=0, keepdims=True)
        idx = jnp.min(jnp.where(s == m, rows, float(s.shape[0])), axis=0, keepdims=True)
        hit = rows == idx
        s = jnp.where(hit, -jnp.inf, s)
        vals.append(m)
        if payload is None:
            picks.append(idx)
        else:
            picks.append(jnp.max(jnp.where(hit, payload, -1), axis=0, keepdims=True))
        yield
    return jnp.concatenate(vals, axis=0), jnp.concatenate(picks, axis=0)


def _run(steps):
    try:
        while True:
            next(steps)
    except StopIteration as done:
        return done.value


def _pair_rows(a, b):
    K = PEER_TOPK
    out = []
    for i in range(K // 2):
        jn = K if i == 0 else K // 2
        out.append((a[i:i + 1, :], b[0:jn, :]))
    out.append((a[K // 2:K, :], b[0:1, :]))
    return out


def _peer_topk_kernel(s_ref, ids_ref, gates_ref):
    K = PEER_TOPK

    def head(h, carry):
        ids, gates = _run(_route_head_steps(lambda: s_ref[2 * h], lambda: s_ref[2 * h + 1]))
        off = pl.multiple_of(h * K, K)
        ids_ref[pl.ds(off, K), :] = ids
        gates_ref[pl.ds(off, K), :] = gates
        return carry

    lax.fori_loop(0, PEER_HEADS, head, 0)


def _route_head_steps(load_scores0, load_scores1):
    K = PEER_TOPK
    s0, i0 = yield from _top_rows_steps(load_scores0(), K)
    s1, i1 = yield from _top_rows_steps(load_scores1(), K)
    e0 = i0.astype(jnp.int32) * PEER_N_KEYS
    e1 = i1.astype(jnp.int32)
    cand = jnp.concatenate([x + y for x, y in _pair_rows(s0, s1)], axis=0)
    cand_id = jnp.concatenate([x + y for x, y in _pair_rows(e0, e1)], axis=0)
    best, ids = yield from _top_rows_steps(cand, K, payload=cand_id)
    e = jnp.exp(best - best[0:1, :])
    return ids, e / jnp.sum(e, axis=0, keepdims=True)


def _peer_topk(scores_t, n_tokens):
    T = n_tokens
    tk = PEER_ROUTE_TOK
    return pl.pallas_call(
        _peer_topk_kernel,
        grid=(T // tk,),
        in_specs=[pl.BlockSpec((PEER_GROUPS, PEER_N_KEYS, tk), lambda i: (0, 0, i))],
        out_specs=[pl.BlockSpec((PEER_SEL, tk), lambda i: (0, i))] * 2,
        out_shape=[jax.ShapeDtypeStruct((PEER_SEL, T), jnp.int32),
                   jax.ShapeDtypeStruct((PEER_SEL, T), F32)],
        compiler_params=_cparams(("parallel",)),
        name="peer_topk",
    )(scores_t)


def _gelu_tanh(x):
    return 0.5 * x * (1.0 + jnp.tanh(math.sqrt(2.0 / math.pi) * (x + 0.044715 * x * x * x)))


def _peer_experts_kernel(off_ref, ids_cur, ids_next, gexp_ref, h_ref, uv_hbm, s_ref, out_ref, nids_ref,
                         ngates_ref, buf, sem, rids_scr, rgates_scr):
    i = pl.program_id(0)
    n = pl.num_programs(0)
    sub = h_ref.shape[1]
    group_rows = PEER_TOK * PEER_SEL

    def issue_token(ids_ref, tok, group, t, k0=0, count=PEER_SEL):
        for k in range(k0, k0 + count):
            e = ids_ref[0, tok, k]
            pltpu.make_async_copy(uv_hbm.at[pl.ds(e, 1)],
                                  buf.at[group, pl.ds(t * PEER_SEL + k, 1)],
                                  sem.at[group]).start(priority=k % 2)

    def wait_group(group):
        pltpu.make_async_copy(uv_hbm.at[pl.ds(0, group_rows)], buf.at[group], sem.at[group]).wait()

    @pl.when(i == 0)
    def _():
        for group in range(PEER_AHEAD):
            def body(t, carry, group=group):
                issue_token(ids_cur, group * PEER_TOK + t, group, t)
                return carry
            lax.fori_loop(0, PEER_TOK, body, 0)

    ch = PEER_CHUNK
    n_ch = PEER_SEL // ch
    ccols = ch * sub
    lane = lax.broadcasted_iota(jnp.int32, (sub, ccols), 1)
    diag = (lane % sub) == lax.broadcasted_iota(jnp.int32, (sub, ccols), 0)

    def group_sum(x):
        step = 1
        while step < sub:
            partner = jnp.where((lane % (2 * step)) < step, pltpu.roll(x, ccols - step, 1),
                                pltpu.roll(x, step, 1))
            x = x + partner
            step *= 2
        return x

    tokens = PEER_GROUPS_PER_STEP * PEER_TOK
    burst = PEER_SEL // (2 * n_ch)
    issued = [0]

    def routing_steps():
        routed = []
        for j in range(s_ref.shape[0] // 2):
            routed.append((yield from _route_head_steps(lambda j=j: s_ref[2 * j],
                                                        lambda j=j: s_ref[2 * j + 1])))
        return routed

    routing = routing_steps()
    routed = []

    def routing_tick():
        if not routed:
            try:
                next(routing)
            except StopIteration as done:
                routed.extend(done.value)

    def request_burst():
        for r in range(issued[0], issued[0] + burst):
            tok, k = divmod(r, PEER_SEL)
            ahead = tok // PEER_TOK + PEER_AHEAD
            ids_ref = ids_cur if ahead < PEER_GROUPS_PER_STEP else ids_next
            group = ahead % PEER_GROUPS_PER_STEP
            issue_token(ids_ref, group * PEER_TOK + tok % PEER_TOK, group, tok % PEER_TOK, k, 1)
        issued[0] += burst
        if (issued[0] // burst) % PEER_ROUTE_EVERY == 0:
            routing_tick()

    def first_layer(tok):
        group, t = divmod(tok, PEER_TOK)
        if t == 0:
            wait_group(group)
        hb = h_ref[tok].astype(BF16)
        ps = []
        for c in range(n_ch):
            request_burst()
            rows = pl.ds(t * PEER_SEL + c * ch, ch)
            uc = buf[group, rows, 0:sub, :].reshape(ccols, 128).astype(BF16)
            ps.append(lax.dot_general(hb, uc, _NT, preferred_element_type=F32))
        return ps

    def weights(tok, ps):
        ws = []
        for c in range(n_ch):
            act = jnp.sum(group_sum(jnp.where(diag, ps[c], 0.0)), axis=0, keepdims=True)
            w = _gelu_tanh(act) * gexp_ref[tok:tok + 1, c * ccols:(c + 1) * ccols]
            ws.append(jnp.where(diag, jnp.broadcast_to(w, (sub, ccols)), 0.0).astype(BF16))
        return ws

    def second_layer(tok, ws):
        group, t = divmod(tok, PEER_TOK)
        o = None
        for c in range(n_ch):
            request_burst()
            rows = pl.ds(t * PEER_SEL + c * ch, ch)
            vc = buf[group, rows, sub:2 * sub, :].reshape(ccols, 128).astype(BF16)
            part = jnp.dot(ws[c], vc, preferred_element_type=F32)
            o = part if o is None else o + part
        return o

    ps = {tok: first_layer(tok) for tok in range(PEER_SKEW)}
    outs = []
    for tok in range(tokens):
        if tok + PEER_SKEW < tokens:
            ps[tok + PEER_SKEW] = first_layer(tok + PEER_SKEW)
        outs.append(second_layer(tok, weights(tok, ps.pop(tok))))
    assert issued[0] == tokens * PEER_SEL
    while not routed:
        routing_tick()
    for tok, o in enumerate(outs):
        out_ref[tok] = o
    rsteps = PEER_SEL // (len(routed) * PEER_TOPK)
    part = i % rsteps
    for j, (ids, gates) in enumerate(routed):
        row0 = pl.multiple_of((part * len(routed) + j) * PEER_TOPK, PEER_TOPK)
        rids_scr[pl.ds(row0, PEER_TOPK), :] = ids
        rgates_scr[pl.ds(row0, PEER_TOPK), :] = gates

    @pl.when(part == rsteps - 1)
    def _():
        nids_ref[...] = rids_scr[...].T
        ngates_ref[...] = rgates_scr[...].T

    @pl.when(i == n - 1)
    def _():
        for group in range(PEER_AHEAD):
            wait_group(group)


def _peer_experts(ids, gates, h_tiles, uv, scores_t, rng, next_rng, n_ranges):
    T, sub, _ = h_tiles.shape
    R = T // n_ranges
    tb = PEER_GROUPS_PER_STEP * PEER_TOK
    nb = R // tb
    rsteps = PEER_ROUTE_TOK // tb
    rgroups = PEER_GROUPS // rsteps
    ids3 = ids.reshape(nb, tb, PEER_SEL)
    gexp = jnp.repeat(gates, sub, axis=1)
    offsets = jnp.array([rng * nb, next_rng * (R // PEER_ROUTE_TOK)], jnp.int32)
    smem = lambda imap: pl.BlockSpec((1, tb, PEER_SEL), imap, memory_space=pltpu.SMEM)
    routed = pl.BlockSpec((PEER_ROUTE_TOK, PEER_SEL), lambda i, off: (i // rsteps, 0))
    grid_spec = pltpu.PrefetchScalarGridSpec(
        num_scalar_prefetch=1,
        grid=(nb,),
        in_specs=[smem(lambda i, off: (i, 0, 0)),
                  smem(lambda i, off: (jnp.minimum(i + 1, nb - 1), 0, 0)),
                  pl.BlockSpec((tb, PEER_SEL * sub), lambda i, off: (i, 0)),
                  pl.BlockSpec((tb, sub, 128), lambda i, off: (off[0] + i, 0, 0)),
                  pl.BlockSpec(memory_space=pl.ANY),
                  pl.BlockSpec((rgroups, PEER_N_KEYS, PEER_ROUTE_TOK),
                               lambda i, off: (i % rsteps, 0, off[1] + i // rsteps))],
        out_specs=[pl.BlockSpec((tb, sub, 128), lambda i, off: (i, 0, 0)), routed, routed],
        scratch_shapes=[pltpu.VMEM((PEER_GROUPS_PER_STEP, PEER_TOK * PEER_SEL, 2 * sub, 128), F32),
                        pltpu.SemaphoreType.DMA((PEER_GROUPS_PER_STEP,)),
                        pltpu.VMEM((PEER_SEL, PEER_ROUTE_TOK), jnp.int32),
                        pltpu.VMEM((PEER_SEL, PEER_ROUTE_TOK), F32)])
    return pl.pallas_call(
        _peer_experts_kernel,
        grid_spec=grid_spec,
        out_shape=[jax.ShapeDtypeStruct((R, sub, 128), F32),
                   jax.ShapeDtypeStruct((R, PEER_SEL), jnp.int32),
                   jax.ShapeDtypeStruct((R, PEER_SEL), F32)],
        compiler_params=_cparams(("arbitrary",)),
        name="peer_experts",
    )(offsets, ids3, ids3, gexp, h_tiles, uv, scores_t)


def _ple_final_kernel(x_ref, d_ref, p_ref, gple_ref, wg_ref, wu_ref, gfin_ref, out_ref):
    x = x_ref[...] + d_ref[...].reshape(x_ref.shape)
    gate = _sigmoid(_bdot(_rms(x, gple_ref[...]), wg_ref[...]))
    x = x + gate * _bdot(p_ref[...], wu_ref[...])
    out_ref[...] = _rms(x, gfin_ref[...])


def _ple_final(x2, delta_tiles, p2, g_ple, w_gate, w_up, g_final):
    T, D = x2.shape
    tm = ROW_TILE
    consts = (g_ple, w_gate, w_up, g_final)
    return pl.pallas_call(
        _ple_final_kernel,
        grid=(T // tm,),
        in_specs=[pl.BlockSpec((tm, D), lambda i: (i, 0)),
                  pl.BlockSpec((tm,) + delta_tiles.shape[1:], lambda i: (i, 0, 0)),
                  pl.BlockSpec((tm, p2.shape[1]), lambda i: (i, 0)),
                  _full(g_ple.shape), _full(w_gate.shape), _full(w_up.shape), _full(g_final.shape)],
        out_specs=pl.BlockSpec((tm, D), lambda i: (i, 0)),
        out_shape=jax.ShapeDtypeStruct((T, D), F32),
        compiler_params=_cparams(("parallel",)),
        name="ple_final",
    )(x2, delta_tiles, p2, *consts)


def kernel(x, p, g_mix, w_in, ret_gn_g, rwkv_mu, rwkv_w0, rwkv_w_up, rwkv_a0, rwkv_a_up, rwkv_g_up, rwkv_k_k, rwkv_k_a, rwkv_r_k, rwkv_gn_g, rwkv_gn_b, w_ret_br, w_rwkv_br, w_o, g_ffn, w_pq, peer_sub_keys, peer_u, peer_v, g_ple, w_ple_gate, w_ple_up, g_final):
    B, S, D = x.shape
    T = B * S
    assert w_in.shape[0] == 1, "single-layer block: the final RMSNorm is fused into its last step"
    i = 0
    row = lambda t: t.reshape(1, -1)
    head_of = jnp.arange(WIDTH) // HEAD_DIM
    same_head = head_of[:, None] == head_of[None, :]
    mones = same_head.astype(BF16)
    mavg = (same_head.astype(F32) / HEAD_DIM).astype(BF16)
    ret_cols = 4 * WIDTH
    x2 = x.reshape(T, D)
    wi = w_in[i].astype(BF16)
    z_ret, z_rwkv, z_gate = _in_proj(
        x2, row(g_mix[i]), wi[:, :ret_cols], wi[:, ret_cols:ret_cols + RWKV_COLS],
        wi[:, ret_cols + RWKV_COLS:])
    y_ret = _retention(z_ret.reshape(B, S, ret_cols), row(ret_gn_g[i]), mavg)
    r, lw, k, v, kk, kka, g, bonus = _rwkv_prep(
        z_rwkv.reshape(B, S, RWKV_COLS), rwkv_mu[i], rwkv_w0[i], rwkv_w_up[i], rwkv_a0[i],
        rwkv_a_up[i], rwkv_g_up[i], rwkv_k_k[i], rwkv_k_a[i], rwkv_r_k[i], mones)
    o = _wkv7(r, lw, k, v, kk, kka)
    flat = lambda t: t.reshape(T, WIDTH)
    x2 = _merge(x2, flat(y_ret), flat(o), flat(bonus), flat(g), z_gate,
                row(rwkv_gn_g[i]), row(rwkv_gn_b[i]), mavg, w_ret_br[i].astype(BF16),
                w_rwkv_br[i].astype(BF16), w_o[i].astype(BF16))
    keys = peer_sub_keys[i].reshape(PEER_GROUPS, PEER_N_KEYS, PEER_HALF).astype(BF16)
    h_tiles, scores_t = _peer_scores(x2, row(g_ffn[i]), w_pq[i].astype(BF16), keys)
    n_exp = peer_u.shape[1]
    uv = jnp.concatenate([peer_u[i].reshape(n_exp, D // 128, 128),
                          peer_v[i].reshape(n_exp, D // 128, 128)], axis=1)
    R = T // PEER_RANGES
    ids_t, gates_t = _peer_topk(scores_t, R)
    ids, gates = ids_t.T, gates_t.T
    deltas = []
    for c in range(PEER_RANGES):
        delta, ids, gates = _peer_experts(ids, gates, h_tiles, uv, scores_t, c,
                                          min(c + 1, PEER_RANGES - 1), PEER_RANGES)
        deltas.append(delta)
    delta_tiles = jnp.concatenate(deltas, axis=0)
    out = _ple_final(x2, delta_tiles, p[i].reshape(T, -1), row(g_ple[i]),
                     w_ple_gate[i].astype(BF16), w_ple_up[i].astype(BF16), row(g_final))
    return out.reshape(B, S, D)
```

```python
import math

import jax
import jax.numpy as jnp
from jax import lax
from jax.experimental import pallas as pl
from jax.experimental.pallas import tpu as pltpu

F32 = jnp.float32
BF16 = jnp.bfloat16

LANES = 128

RMS_EPS = 1e-6
HEAD_DIM = 64
N_HEADS = 8
WIDTH = N_HEADS * HEAD_DIM
RET_CHUNK = 128
RET_BATCH = 2
RET_GN_EPS = 1e-5
ROPE_BASE = 10000.0
RWKV_GN_EPS = 64e-5
L2_EPS = 1e-12
DECAY_LORA = 64
AAA_LORA = 64
GATE_LORA = 128
RWKV_COLS = 3 * WIDTH + DECAY_LORA + AAA_LORA + GATE_LORA
WKV_CHUNK = 64
WKV_BATCH = 4

PEER_HEADS = 8
PEER_N_KEYS = 128
PEER_HALF = 128
PEER_TOPK = 16
PEER_GROUPS = 2 * PEER_HEADS
PEER_SEL = PEER_HEADS * PEER_TOPK
PEER_ROUTE_TOK = 128
PEER_ROUTE_EVERY = 2
PEER_RANGES = 8
PEER_TOK = 8
PEER_GROUPS_PER_STEP = 4
PEER_AHEAD = 2
PEER_SKEW = 2
PEER_CHUNK = 32

ROW_TILE = 256
VMEM_LIMIT = 48 * 1024 * 1024

_NT = (((1,), (1,)), ((), ()))
_TN = (((0,), (0,)), ((), ()))


def _bdot(a, b):
    return jnp.dot(a.astype(BF16), b.astype(BF16), preferred_element_type=F32)


def _bdot_nt(a, b):
    return lax.dot_general(a.astype(BF16), b.astype(BF16), _NT, preferred_element_type=F32)


def _bdot_tn(a, b):
    return lax.dot_general(a.astype(BF16), b.astype(BF16), _TN, preferred_element_type=F32)


def _split2(a):
    hi = a.astype(BF16)
    lo = (a - hi.astype(F32)).astype(BF16)
    return hi, lo


def _seg_dot(a, m):
    hi, lo = _split2(a)
    return (jnp.dot(hi, m, preferred_element_type=F32)
            + jnp.dot(lo, m, preferred_element_type=F32))


def _rms(x, g):
    return x * lax.rsqrt(jnp.mean(x * x, axis=-1, keepdims=True) + RMS_EPS) * g


def _sigmoid(x):
    return 1.0 / (1.0 + jnp.exp(-x))


def _head_norm(o, mavg, eps):
    mu = _seg_dot(o, mavg)
    oc = o - mu
    var = _seg_dot(oc * oc, mavg)
    return oc * lax.rsqrt(var + eps)


def _cparams(sem, vmem=VMEM_LIMIT):
    return pltpu.CompilerParams(dimension_semantics=sem, vmem_limit_bytes=vmem)


def _full(shape):
    nd = len(shape)
    return pl.BlockSpec(shape, lambda *_: (0,) * nd)


def _in_proj_kernel(x_ref, g_ref, w1_ref, w2_ref, w3_ref, o1_ref, o2_ref, o3_ref):
    h = _rms(x_ref[...], g_ref[...]).astype(BF16)
    o1_ref[...] = jnp.dot(h, w1_ref[...], preferred_element_type=F32).astype(o1_ref.dtype)
    o2_ref[...] = jnp.dot(h, w2_ref[...], preferred_element_type=F32).astype(o2_ref.dtype)
    o3_ref[...] = jnp.dot(h, w3_ref[...], preferred_element_type=F32).astype(o3_ref.dtype)


def _in_proj(x2, g, w_ret, w_rwkv, w_gate):
    T, D = x2.shape
    tm = ROW_TILE
    ws = (w_ret, w_rwkv, w_gate)
    return pl.pallas_call(
        _in_proj_kernel,
        grid=(T // tm,),
        in_specs=[pl.BlockSpec((tm, D), lambda i: (i, 0)), _full((1, D))]
        + [_full(w.shape) for w in ws],
        out_specs=[pl.BlockSpec((tm, w.shape[1]), lambda i: (i, 0)) for w in ws],
        out_shape=[jax.ShapeDtypeStruct((T, w.shape[1]), dt) for w, dt in zip(ws, (BF16, F32, BF16))],
        compiler_params=_cparams(("parallel",)),
        name="in_proj",
    )(x2, g, *ws)


def _retention_kernel(z_ref, cos_ref, sin_ref, xi_ref, zeta_ref, decay_ref, cd_ref,
                      gn_ref, mavg_ref, y_ref, state_ref):
    @pl.when(pl.program_id(1) == 0)
    def _():
        state_ref[...] = jnp.zeros_like(state_ref)

    cos = cos_ref[...]
    sin = sin_ref[...]
    lane = lax.broadcasted_iota(jnp.int32, cos.shape, 1)
    first_half = (lane % HEAD_DIM) < (HEAD_DIM // 2)

    def rot(x):
        partner = jnp.where(first_half, pltpu.roll(x, WIDTH - HEAD_DIM // 2, 1),
                            pltpu.roll(x, HEAD_DIM // 2, 1))
        return x * cos + partner * sin

    qs, ks, vs, qxs, kzs, grs = [], [], [], [], [], []
    for b in range(RET_BATCH):
        z = z_ref[b].astype(F32)
        v = z[:, 2 * WIDTH:3 * WIDTH]
        grs.append(z[:, 3 * WIDTH:4 * WIDTH])
        qr = rot(z[:, 0:WIDTH])
        kr = rot(z[:, WIDTH:2 * WIDTH]) * (HEAD_DIM ** -0.5)
        qx = qr * xi_ref[...]
        kz = kr * zeta_ref[...]
        for h in range(N_HEADS):
            sl = slice(h * HEAD_DIM, (h + 1) * HEAD_DIM)
            qs.append(qr[:, sl].astype(BF16))
            ks.append(kr[:, sl].astype(BF16))
            vs.append(v[:, sl].astype(BF16))
            qxs.append(qx[:, sl])
            kzs.append(kz[:, sl])
    chains = range(RET_BATCH * N_HEADS)
    scores = [_bdot_nt(qs[c], ks[c]) * decay_ref[c % N_HEADS] for c in chains]
    states = [state_ref[c] for c in chains]
    cross = [_bdot(qxs[c], states[c]) for c in chains]
    outs = [_bdot(scores[c], vs[c]) + cross[c] for c in chains]
    for c in chains:
        state_ref[c] = states[c] * cd_ref[c % N_HEADS] + _bdot_tn(kzs[c], vs[c])
    for b in range(RET_BATCH):
        o = jnp.concatenate(outs[b * N_HEADS:(b + 1) * N_HEADS], axis=1)
        y = _head_norm(o, mavg_ref[...], RET_GN_EPS)
        y_ref[b] = (grs[b] * _sigmoid(grs[b]) * (y * gn_ref[...])).astype(BF16)


def _retention(z_ret, gn_g, mavg):
    B, S, _ = z_ret.shape
    C = RET_CHUNK
    half = HEAD_DIM // 2
    inv_freq = ROPE_BASE ** (-jnp.arange(half, dtype=F32) * 2.0 / HEAD_DIM)
    ang = jnp.arange(S, dtype=F32)[:, None] * inv_freq[None, :]
    cos_h = jnp.concatenate([jnp.cos(ang), jnp.cos(ang)], axis=1)
    sin_h = jnp.concatenate([-jnp.sin(ang), jnp.sin(ang)], axis=1)
    cos = jnp.tile(cos_h, (1, N_HEADS))
    sin = jnp.tile(sin_h, (1, N_HEADS))
    log_gamma = jnp.log1p(-(2.0 ** (-5.0 - jnp.arange(N_HEADS, dtype=F32))))
    idx = jnp.arange(C, dtype=F32)
    diff = idx[:, None] - idx[None, :]
    causal = diff >= 0
    decay = jnp.where(causal[None], jnp.exp(log_gamma[:, None, None] * jnp.where(causal, diff, 0.0)[None]), 0.0)
    zeta = jnp.exp(log_gamma[:, None] * (C - 1.0 - idx)[None, :])
    xi = jnp.exp(log_gamma[:, None] * (idx + 1.0)[None, :])
    widen = lambda t: jnp.repeat(t.T, HEAD_DIM, axis=1)
    cd = jnp.broadcast_to(jnp.exp(log_gamma * C)[:, None, None], (N_HEADS, HEAD_DIM, HEAD_DIM))
    return pl.pallas_call(
        _retention_kernel,
        grid=(B // RET_BATCH, S // C),
        in_specs=[pl.BlockSpec((RET_BATCH, C, 4 * WIDTH), lambda b, c: (b, c, 0)),
                  pl.BlockSpec((C, WIDTH), lambda b, c: (c, 0)),
                  pl.BlockSpec((C, WIDTH), lambda b, c: (c, 0)),
                  _full((C, WIDTH)), _full((C, WIDTH)), _full((N_HEADS, C, C)),
                  _full((N_HEADS, HEAD_DIM, HEAD_DIM)), _full((1, WIDTH)), _full((WIDTH, WIDTH))],
        out_specs=pl.BlockSpec((RET_BATCH, C, WIDTH), lambda b, c: (b, c, 0)),
        out_shape=jax.ShapeDtypeStruct((B, S, WIDTH), BF16),
        scratch_shapes=[pltpu.VMEM((RET_BATCH * N_HEADS, HEAD_DIM, HEAD_DIM), F32)],
        compiler_params=_cparams(("parallel", "arbitrary")),
        name="retention",
    )(z_ret, cos, sin, widen(xi), widen(zeta), decay, cd, gn_g, mavg)


def _rwkv_prep_kernel(z_ref, mu_ref, w0_ref, wup_ref, a0_ref, aup_ref, gup_ref, kk_ref,
                      ka_ref, rk_ref, mones_ref,
                      r_out, lw_out, k_out, v_out, kk_out, kka_out, g_out, bonus_out,
                      carry_ref):
    @pl.when(pl.program_id(1) == 0)
    def _():
        carry_ref[...] = jnp.zeros_like(carry_ref)

    z = z_ref[0]
    n = z.shape[0]
    row = lax.broadcasted_iota(jnp.int32, z.shape, 0)
    prev = jnp.where(row == 0, carry_ref[0:1, :], pltpu.roll(z, 1, 0))
    carry_ref[0:1, :] = z[n - 1:n, :]
    zs = z + (prev - z) * mu_ref[...]
    r = zs[:, 0:WIDTH]
    kr = zs[:, WIDTH:2 * WIDTH]
    vr = zs[:, 2 * WIDTH:3 * WIDTH]
    o = 3 * WIDTH
    wl = zs[:, o:o + DECAY_LORA]
    al = zs[:, o + DECAY_LORA:o + DECAY_LORA + AAA_LORA]
    gl = zs[:, o + DECAY_LORA + AAA_LORA:]
    t = -(w0_ref[...] + _bdot(jnp.tanh(wl), wup_ref[...]))
    softplus = jnp.maximum(t, 0.0) + jnp.log1p(jnp.exp(-jnp.abs(t)))
    w_log = -softplus - 0.5
    a = _sigmoid(a0_ref[...] + _bdot(al, aup_ref[...]))
    g = _bdot(_sigmoid(gl), gup_ref[...])
    mones = mones_ref[...]
    kk = kr * kk_ref[...]
    norm = jnp.sqrt(_seg_dot(kk * kk, mones))
    kk = kk / jnp.maximum(norm, L2_EPS)
    k2 = kr * (1.0 + (a - 1.0) * ka_ref[...])
    r_out[0] = r.astype(BF16)
    lw_out[0] = -jnp.exp(w_log)
    k_out[0] = k2.astype(BF16)
    v_out[0] = vr.astype(BF16)
    kk_out[0] = kk.astype(BF16)
    kka_out[0] = (kk * a).astype(BF16)
    g_out[0] = g.astype(BF16)
    bonus_out[0] = (_seg_dot(r * k2 * rk_ref[...], mones) * vr).astype(BF16)


def _rwkv_prep(z_rwkv, mu, w0, w_up, a0, a_up, g_up, k_k, k_a, r_k, mones):
    B, S, _ = z_rwkv.shape
    ts = ROW_TILE
    row = lambda t: t.reshape(1, -1)
    args = (row(mu), row(w0), w_up.astype(BF16), row(a0), a_up.astype(BF16),
            g_up.astype(BF16), row(k_k), row(k_a), row(r_k), mones)
    out_spec = pl.BlockSpec((1, ts, WIDTH), lambda b, s: (b, s, 0))
    return pl.pallas_call(
        _rwkv_prep_kernel,
        grid=(B, S // ts),
        in_specs=[pl.BlockSpec((1, ts, RWKV_COLS), lambda b, s: (b, s, 0))]
        + [_full(a.shape) for a in args],
        out_specs=[out_spec] * 8,
        out_shape=[jax.ShapeDtypeStruct((B, S, WIDTH), F32 if n == 1 else BF16) for n in range(8)],
        scratch_shapes=[pltpu.VMEM((8, RWKV_COLS), F32)],
        compiler_params=_cparams(("parallel", "arbitrary")),
        name="rwkv_prep",
    )(z_rwkv, *args)


def _wkv7_kernel(r_ref, lw_ref, k_ref, v_ref, kk_ref, kka_ref, tri_ref, o_ref, state_ref):
    @pl.when(pl.program_id(1) == 0)
    def _():
        state_ref[...] = jnp.zeros_like(state_ref)

    L = WKV_CHUNK
    tri = tri_ref[...]
    ri = lax.broadcasted_iota(jnp.int32, (2 * L, L), 0)
    ci = lax.broadcasted_iota(jnp.int32, (2 * L, L), 1)
    mask = jnp.where(ri < L, ri, ri - L + 1) > ci
    eye = (lax.broadcasted_iota(jnp.int32, (L, L), 0)
           == lax.broadcasted_iota(jnp.int32, (L, L), 1)).astype(F32)

    ar, vf, bt, kt, bke, gl = [], [], [], [], [], []
    for b in range(WKV_BATCH):
        lw = lw_ref[b]
        hi = lw.astype(BF16)
        rem = lw - hi.astype(F32)
        mid = rem.astype(BF16)
        lo = (rem - mid.astype(F32)).astype(BF16)
        cum = (jnp.dot(tri, hi, preferred_element_type=F32)
               + jnp.dot(tri, mid, preferred_element_type=F32)
               + jnp.dot(tri, lo, preferred_element_type=F32))
        cum_last = cum[L - 1:L, :]
        inv_g = jnp.exp(-cum)
        to_end = jnp.exp(cum_last - cum)
        g_last = jnp.exp(cum_last)
        kk = kk_ref[b].astype(F32)
        kka = kka_ref[b].astype(F32)
        k = k_ref[b].astype(F32)
        v = v_ref[b].astype(F32)
        a_t = -kk * jnp.exp(cum - lw)
        b_t = kka * inv_g
        k_t = k * inv_g
        r_t = r_ref[b].astype(F32) * jnp.exp(cum)
        b_end = kka * to_end
        k_end = k * to_end
        for h in range(N_HEADS):
            sl = slice(h * HEAD_DIM, (h + 1) * HEAD_DIM)
            ar.append(jnp.concatenate([a_t[:, sl], r_t[:, sl]], axis=0).astype(BF16))
            vf.append(v[:, sl])
            bt.append(b_t[:, sl])
            kt.append(k_t[:, sl])
            bke.append(jnp.concatenate([b_end[:, sl], k_end[:, sl]], axis=0))
            gl.append(g_last[:, sl])
    chains = range(WKV_BATCH * N_HEADS)
    vs = [x.astype(BF16) for x in vf]
    abrb = [jnp.where(mask, _bdot_nt(ar[c], bt[c]), 0.0) for c in chains]
    akrk = [jnp.where(mask, _bdot_nt(ar[c], kt[c]), 0.0) for c in chains]
    p = [m[:L] for m in abrb]
    inv = [eye + m for m in p]
    for _ in range(int(math.log2(L)) - 1):
        p = [_bdot(m, m) for m in p]
        inv = [inv[c] + _bdot(inv[c], p[c]) for c in chains]
    states = [state_ref[c] for c in chains]
    xs = [_bdot_nt(ar[c], states[c]) for c in chains]
    kv = [_bdot(akrk[c], vs[c]) for c in chains]
    u = [_bdot(inv[c], xs[c][:L] + kv[c][:L]) for c in chains]
    y = [xs[c][L:] + kv[c][L:] + _bdot(abrb[c][L:], u[c]) for c in chains]
    for c in chains:
        uv = jnp.concatenate([u[c], vf[c]], axis=0)
        state_ref[c] = states[c] * gl[c] + _bdot_tn(uv, bke[c])
    for b in range(WKV_BATCH):
        o_ref[b] = jnp.concatenate(y[b * N_HEADS:(b + 1) * N_HEADS], axis=1)


def _wkv7(r, lw, k, v, kk, kka):
    B, S, _ = r.shape
    L = WKV_CHUNK
    nb = WKV_BATCH
    tri = jnp.tril(jnp.ones((L, L), F32)).astype(BF16)
    spec = pl.BlockSpec((nb, L, WIDTH), lambda b, c: (b, c, 0))
    return pl.pallas_call(
        _wkv7_kernel,
        grid=(B // nb, S // L),
        in_specs=[spec] * 6 + [_full((L, L))],
        out_specs=spec,
        out_shape=jax.ShapeDtypeStruct((B, S, WIDTH), F32),
        scratch_shapes=[pltpu.VMEM((nb * N_HEADS, HEAD_DIM, HEAD_DIM), F32)],
        compiler_params=_cparams(("parallel", "arbitrary")),
        name="wkv7",
    )(r, lw, k, v, kk, kka, tri)


def _merge_kernel(x_ref, yret_ref, o_ref, bonus_ref, g_ref, zg_ref, gng_ref, gnb_ref,
                  mavg_ref, wret_ref, wrwkv_ref, wo_ref, x1_ref):
    y = _head_norm(o_ref[...], mavg_ref[...], RWKV_GN_EPS)
    y_rwkv = (y * gng_ref[...] + gnb_ref[...] + bonus_ref[...].astype(F32)) * g_ref[...].astype(F32)
    br = _bdot(yret_ref[...], wret_ref[...])
    bw = _bdot(y_rwkv, wrwkv_ref[...])
    zg = zg_ref[...].astype(F32)
    d = br.shape[1]
    merged = _sigmoid(zg[:, :d]) * br + _sigmoid(zg[:, d:]) * bw
    x1_ref[...] = x_ref[...] + _bdot(merged, wo_ref[...])


def _merge(x2, y_ret, o, bonus, g, z_gate, gn_g, gn_b, mavg, w_ret_br, w_rwkv_br, w_o):
    T, D = x2.shape
    tm = ROW_TILE
    rows = lambda n: pl.BlockSpec((tm, n), lambda i: (i, 0))
    consts = (gn_g, gn_b, mavg, w_ret_br, w_rwkv_br, w_o)
    return pl.pallas_call(
        _merge_kernel,
        grid=(T // tm,),
        in_specs=[rows(D), rows(WIDTH), rows(WIDTH), rows(WIDTH), rows(WIDTH), rows(2 * D)]
        + [_full(c.shape) for c in consts],
        out_specs=rows(D),
        out_shape=jax.ShapeDtypeStruct((T, D), F32),
        compiler_params=_cparams(("parallel",)),
        name="merge",
    )(x2, y_ret, o, bonus, g, z_gate, *consts)


def _peer_scores_kernel(x_ref, g_ref, wq_ref, keys_ref, h_ref, s_ref):
    h = _rms(x_ref[...], g_ref[...])
    h_ref[...] = h.reshape(h_ref.shape)
    q = _bdot(h, wq_ref[...]).astype(BF16)
    for grp in range(PEER_GROUPS):
        s_ref[grp] = lax.dot_general(keys_ref[grp], q[:, grp * PEER_HALF:(grp + 1) * PEER_HALF],
                                     _NT, preferred_element_type=F32)


def _peer_scores(x1, g_ffn, w_pq, keys):
    T, D = x1.shape
    tm = ROW_TILE
    return pl.pallas_call(
        _peer_scores_kernel,
        grid=(T // tm,),
        in_specs=[pl.BlockSpec((tm, D), lambda i: (i, 0)), _full((1, D)), _full(w_pq.shape),
                  _full(keys.shape)],
        out_specs=[pl.BlockSpec((tm, D // LANES, LANES), lambda i: (i, 0, 0)),
                   pl.BlockSpec((PEER_GROUPS, PEER_N_KEYS, tm), lambda i: (0, 0, i))],
        out_shape=[jax.ShapeDtypeStruct((T, D // LANES, LANES), F32),
                   jax.ShapeDtypeStruct((PEER_GROUPS, PEER_N_KEYS, T), F32)],
        compiler_params=_cparams(("parallel",)),
        name="peer_scores",
    )(x1, g_ffn, w_pq, keys)


def _top_rows_steps(s, count, payload=None):
    rows = lax.broadcasted_iota(jnp.int32, s.shape, 0).astype(F32)
    vals, picks = [], []
    for _ in range(count):
        m = jnp.max(s, axis=0, keepdims=True)
        idx = jnp.min(jnp.where(s == m, rows, float(s.shape[0])), axis=0, keepdims=True)
        hit = rows == idx
        s = jnp.where(hit, -jnp.inf, s)
        vals.append(m)
        if payload is None:
            picks.append(idx)
        else:
            picks.append(jnp.max(jnp.where(hit, payload, -1), axis=0, keepdims=True))
        yield
    return jnp.concatenate(vals, axis=0), jnp.concatenate(picks, axis=0)


def _run(steps):
    try:
        while True:
            next(steps)
    except StopIteration as done:
        return done.value


def _pair_rows(a, b):
    K = PEER_TOPK
    out = []
    for i in range(K // 2):
        jn = K if i == 0 else K // 2
        out.append((a[i:i + 1, :], b[0:jn, :]))
    out.append((a[K // 2:K, :], b[0:1, :]))
    return out


def _peer_topk_kernel(s_ref, ids_ref, gates_ref):
    K = PEER_TOPK

    def head(h, carry):
        ids, gates = _run(_route_head_steps(lambda: s_ref[2 * h], lambda: s_ref[2 * h + 1]))
        off = pl.multiple_of(h * K, K)
        ids_ref[pl.ds(off, K), :] = ids
        gates_ref[pl.ds(off, K), :] = gates
        return carry

    lax.fori_loop(0, PEER_HEADS, head, 0)


def _route_head_steps(load_scores0, load_scores1):
    K = PEER_TOPK
    s0, i0 = yield from _top_rows_steps(load_scores0(), K)
    s1, i1 = yield from _top_rows_steps(load_scores1(), K)
    e0 = i0.astype(jnp.int32) * PEER_N_KEYS
    e1 = i1.astype(jnp.int32)
    cand = jnp.concatenate([x + y for x, y in _pair_rows(s0, s1)], axis=0)
    cand_id = jnp.concatenate([x + y for x, y in _pair_rows(e0, e1)], axis=0)
    best, ids = yield from _top_rows_steps(cand, K, payload=cand_id)
    e = jnp.exp(best - best[0:1, :])
    return ids, e / jnp.sum(e, axis=0, keepdims=True)


def _peer_topk(scores_t, n_tokens):
    T = n_tokens
    tk = PEER_ROUTE_TOK
    return pl.pallas_call(
        _peer_topk_kernel,
        grid=(T // tk,),
        in_specs=[pl.BlockSpec((PEER_GROUPS, PEER_N_KEYS, tk), lambda i: (0, 0, i))],
        out_specs=[pl.BlockSpec((PEER_SEL, tk), lambda i: (0, i))] * 2,
        out_shape=[jax.ShapeDtypeStruct((PEER_SEL, T), jnp.int32),
                   jax.ShapeDtypeStruct((PEER_SEL, T), F32)],
        compiler_params=_cparams(("parallel",)),
        name="peer_topk",
    )(scores_t)


def _gelu_tanh(x):
    return 0.5 * x * (1.0 + jnp.tanh(math.sqrt(2.0 / math.pi) * (x + 0.044715 * x * x * x)))


def _peer_experts_kernel(off_ref, ids_cur, ids_next, gexp_ref, h_ref, uv_hbm, s_ref, out_ref, nids_ref,
                         ngates_ref, buf, sem, rids_scr, rgates_scr):
    i = pl.program_id(0)
    n = pl.num_programs(0)
    sub = h_ref.shape[1]
    group_rows = PEER_TOK * PEER_SEL

    def issue_token(ids_ref, tok, group, t, k0=0, count=PEER_SEL):
        for k in range(k0, k0 + count):
            e = ids_ref[0, tok, k]
            pltpu.make_async_copy(uv_hbm.at[pl.ds(e, 1)],
                                  buf.at[group, pl.ds(t * PEER_SEL + k, 1)],
                                  sem.at[group]).start(priority=k % 2)

    def wait_group(group):
        pltpu.make_async_copy(uv_hbm.at[pl.ds(0, group_rows)], buf.at[group], sem.at[group]).wait()

    @pl.when(i == 0)
    def _():
        for group in range(PEER_AHEAD):
            def body(t, carry, group=group):
                issue_token(ids_cur, group * PEER_TOK + t, group, t)
                return carry
            lax.fori_loop(0, PEER_TOK, body, 0)

    ch = PEER_CHUNK
    n_ch = PEER_SEL // ch
    ccols = ch * sub
    lane = lax.broadcasted_iota(jnp.int32, (sub, ccols), 1)
    diag = (lane % sub) == lax.broadcasted_iota(jnp.int32, (sub, ccols), 0)

    def group_sum(x):
        step = 1
        while step < sub:
            partner = jnp.where((lane % (2 * step)) < step, pltpu.roll(x, ccols - step, 1),
                                pltpu.roll(x, step, 1))
            x = x + partner
            step *= 2
        return x

    tokens = PEER_GROUPS_PER_STEP * PEER_TOK
    burst = PEER_SEL // (2 * n_ch)
    issued = [0]

    def routing_steps():
        routed = []
        for j in range(s_ref.shape[0] // 2):
            routed.append((yield from _route_head_steps(lambda j=j: s_ref[2 * j],
                                                        lambda j=j: s_ref[2 * j + 1])))
        return routed

    routing = routing_steps()
    routed = []

    def routing_tick():
        if not routed:
            try:
                next(routing)
            except StopIteration as done:
                routed.extend(done.value)

    def request_burst():
        for r in range(issued[0], issued[0] + burst):
            tok, k = divmod(r, PEER_SEL)
            ahead = tok // PEER_TOK + PEER_AHEAD
            ids_ref = ids_cur if ahead < PEER_GROUPS_PER_STEP else ids_next
            group = ahead % PEER_GROUPS_PER_STEP
            issue_token(ids_ref, group * PEER_TOK + tok % PEER_TOK, group, tok % PEER_TOK, k, 1)
        issued[0] += burst
        if (issued[0] // burst) % PEER_ROUTE_EVERY == 0:
            routing_tick()

    def first_layer(tok):
        group, t = divmod(tok, PEER_TOK)
        if t == 0:
            wait_group(group)
        hb = h_ref[tok].astype(BF16)
        ps = []
        for c in range(n_ch):
            request_burst()
            rows = pl.ds(t * PEER_SEL + c * ch, ch)
            uc = buf[group, rows, 0:sub, :].reshape(ccols, LANES).astype(BF16)
            ps.append(lax.dot_general(hb, uc, _NT, preferred_element_type=F32))
        return ps

    def weights(tok, ps):
        ws = []
        for c in range(n_ch):
            act = jnp.sum(group_sum(jnp.where(diag, ps[c], 0.0)), axis=0, keepdims=True)
            w = _gelu_tanh(act) * gexp_ref[tok:tok + 1, c * ccols:(c + 1) * ccols]
            ws.append(jnp.where(diag, jnp.broadcast_to(w, (sub, ccols)), 0.0).astype(BF16))
        return ws

    def second_layer(tok, ws):
        group, t = divmod(tok, PEER_TOK)
        o = None
        for c in range(n_ch):
            request_burst()
            rows = pl.ds(t * PEER_SEL + c * ch, ch)
            vc = buf[group, rows, sub:2 * sub, :].reshape(ccols, LANES).astype(BF16)
            part = jnp.dot(ws[c], vc, preferred_element_type=F32)
            o = part if o is None else o + part
        return o

    ps = {tok: first_layer(tok) for tok in range(PEER_SKEW)}
    outs = []
    for tok in range(tokens):
        if tok + PEER_SKEW < tokens:
            ps[tok + PEER_SKEW] = first_layer(tok + PEER_SKEW)
        outs.append(second_layer(tok, weights(tok, ps.pop(tok))))
    assert issued[0] == tokens * PEER_SEL
    while not routed:
        routing_tick()
    for tok, o in enumerate(outs):
        out_ref[tok] = o
    rsteps = PEER_SEL // (len(routed) * PEER_TOPK)
    part = i % rsteps
    for j, (ids, gates) in enumerate(routed):
        row0 = pl.multiple_of((part * len(routed) + j) * PEER_TOPK, PEER_TOPK)
        rids_scr[pl.ds(row0, PEER_TOPK), :] = ids
        rgates_scr[pl.ds(row0, PEER_TOPK), :] = gates

    @pl.when(part == rsteps - 1)
    def _():
        nids_ref[...] = rids_scr[...].T
        ngates_ref[...] = rgates_scr[...].T

    @pl.when(i == n - 1)
    def _():
        for group in range(PEER_AHEAD):
            wait_group(group)


def _peer_experts(ids, gates, h_tiles, uv, scores_t, rng, next_rng, n_ranges):
    T, sub, _ = h_tiles.shape
    R = T // n_ranges
    tb = PEER_GROUPS_PER_STEP * PEER_TOK
    nb = R // tb
    rsteps = PEER_ROUTE_TOK // tb
    rgroups = PEER_GROUPS // rsteps
    ids3 = ids.reshape(nb, tb, PEER_SEL)
    gexp = jnp.repeat(gates, sub, axis=1)
    offsets = jnp.array([rng * nb, next_rng * (R // PEER_ROUTE_TOK)], jnp.int32)
    smem = lambda imap: pl.BlockSpec((1, tb, PEER_SEL), imap, memory_space=pltpu.SMEM)
    routed = pl.BlockSpec((PEER_ROUTE_TOK, PEER_SEL), lambda i, off: (i // rsteps, 0))
    grid_spec = pltpu.PrefetchScalarGridSpec(
        num_scalar_prefetch=1,
        grid=(nb,),
        in_specs=[smem(lambda i, off: (i, 0, 0)),
                  smem(lambda i, off: (jnp.minimum(i + 1, nb - 1), 0, 0)),
                  pl.BlockSpec((tb, PEER_SEL * sub), lambda i, off: (i, 0)),
                  pl.BlockSpec((tb, sub, LANES), lambda i, off: (off[0] + i, 0, 0)),
                  pl.BlockSpec(memory_space=pl.ANY),
                  pl.BlockSpec((rgroups, PEER_N_KEYS, PEER_ROUTE_TOK),
                               lambda i, off: (i % rsteps, 0, off[1] + i // rsteps))],
        out_specs=[pl.BlockSpec((tb, sub, LANES), lambda i, off: (i, 0, 0)), routed, routed],
        scratch_shapes=[pltpu.VMEM((PEER_GROUPS_PER_STEP, PEER_TOK * PEER_SEL, 2 * sub, LANES), F32),
                        pltpu.SemaphoreType.DMA((PEER_GROUPS_PER_STEP,)),
                        pltpu.VMEM((PEER_SEL, PEER_ROUTE_TOK), jnp.int32),
                        pltpu.VMEM((PEER_SEL, PEER_ROUTE_TOK), F32)])
    return pl.pallas_call(
        _peer_experts_kernel,
        grid_spec=grid_spec,
        out_shape=[jax.ShapeDtypeStruct((R, sub, LANES), F32),
                   jax.ShapeDtypeStruct((R, PEER_SEL), jnp.int32),
                   jax.ShapeDtypeStruct((R, PEER_SEL), F32)],
        compiler_params=_cparams(("arbitrary",)),
        name="peer_experts",
    )(offsets, ids3, ids3, gexp, h_tiles, uv, scores_t)


def _ple_final_kernel(x_ref, d_ref, p_ref, gple_ref, wg_ref, wu_ref, gfin_ref, out_ref):
    x = x_ref[...] + d_ref[...].reshape(x_ref.shape)
    gate = _sigmoid(_bdot(_rms(x, gple_ref[...]), wg_ref[...]))
    x = x + gate * _bdot(p_ref[...], wu_ref[...])
    out_ref[...] = _rms(x, gfin_ref[...])


def _ple_final(x2, delta_tiles, p2, g_ple, w_gate, w_up, g_final):
    T, D = x2.shape
    tm = ROW_TILE
    consts = (g_ple, w_gate, w_up, g_final)
    return pl.pallas_call(
        _ple_final_kernel,
        grid=(T // tm,),
        in_specs=[pl.BlockSpec((tm, D), lambda i: (i, 0)),
                  pl.BlockSpec((tm,) + delta_tiles.shape[1:], lambda i: (i, 0, 0)),
                  pl.BlockSpec((tm, p2.shape[1]), lambda i: (i, 0)),
                  _full(g_ple.shape), _full(w_gate.shape), _full(w_up.shape), _full(g_final.shape)],
        out_specs=pl.BlockSpec((tm, D), lambda i: (i, 0)),
        out_shape=jax.ShapeDtypeStruct((T, D), F32),
        compiler_params=_cparams(("parallel",)),
        name="ple_final",
    )(x2, delta_tiles, p2, *consts)


def kernel(x, p, g_mix, w_in, ret_gn_g, rwkv_mu, rwkv_w0, rwkv_w_up, rwkv_a0, rwkv_a_up, rwkv_g_up, rwkv_k_k, rwkv_k_a, rwkv_r_k, rwkv_gn_g, rwkv_gn_b, w_ret_br, w_rwkv_br, w_o, g_ffn, w_pq, peer_sub_keys, peer_u, peer_v, g_ple, w_ple_gate, w_ple_up, g_final):
    B, S, D = x.shape
    T = B * S
    assert w_in.shape[0] == 1, "single-layer block: the final RMSNorm is fused into its last step"
    i = 0
    row = lambda t: t.reshape(1, -1)
    head_of = jnp.arange(WIDTH) // HEAD_DIM
    same_head = head_of[:, None] == head_of[None, :]
    mones = same_head.astype(BF16)
    mavg = (same_head.astype(F32) / HEAD_DIM).astype(BF16)
    ret_cols = 4 * WIDTH
    x2 = x.reshape(T, D)
    wi = w_in[i].astype(BF16)
    z_ret, z_rwkv, z_gate = _in_proj(
        x2, row(g_mix[i]), wi[:, :ret_cols], wi[:, ret_cols:ret_cols + RWKV_COLS],
        wi[:, ret_cols + RWKV_COLS:])
    y_ret = _retention(z_ret.reshape(B, S, ret_cols), row(ret_gn_g[i]), mavg)
    r, lw, k, v, kk, kka, g, bonus = _rwkv_prep(
        z_rwkv.reshape(B, S, RWKV_COLS), rwkv_mu[i], rwkv_w0[i], rwkv_w_up[i], rwkv_a0[i],
        rwkv_a_up[i], rwkv_g_up[i], rwkv_k_k[i], rwkv_k_a[i], rwkv_r_k[i], mones)
    o = _wkv7(r, lw, k, v, kk, kka)
    flat = lambda t: t.reshape(T, WIDTH)
    x2 = _merge(x2, flat(y_ret), flat(o), flat(bonus), flat(g), z_gate,
                row(rwkv_gn_g[i]), row(rwkv_gn_b[i]), mavg, w_ret_br[i].astype(BF16),
                w_rwkv_br[i].astype(BF16), w_o[i].astype(BF16))
    keys = peer_sub_keys[i].reshape(PEER_GROUPS, PEER_N_KEYS, PEER_HALF).astype(BF16)
    h_tiles, scores_t = _peer_scores(x2, row(g_ffn[i]), w_pq[i].astype(BF16), keys)
    n_exp = peer_u.shape[1]
    uv = jnp.concatenate([peer_u[i].reshape(n_exp, D // LANES, LANES),
                          peer_v[i].reshape(n_exp, D // LANES, LANES)], axis=1)
    R = T // PEER_RANGES
    ids_t, gates_t = _peer_topk(scores_t, R)
    ids, gates = ids_t.T, gates_t.T
    deltas = []
    for c in range(PEER_RANGES):
        delta, ids, gates = _peer_experts(ids, gates, h_tiles, uv, scores_t, c,
                                          min(c + 1, PEER_RANGES - 1), PEER_RANGES)
        deltas.append(delta)
    delta_tiles = jnp.concatenate(deltas, axis=0)
    out = _ple_final(x2, delta_tiles, p[i].reshape(T, -1), row(g_ple[i]),
                     w_ple_gate[i].astype(BF16), w_ple_up[i].astype(BF16), row(g_final))
    return out.reshape(B, S, D)
```

```python
import math

import jax
import jax.numpy as jnp
from jax import lax
from jax.experimental import pallas as pl
from jax.experimental.pallas import tpu as pltpu

F32 = jnp.float32
BF16 = jnp.bfloat16

LANES = 128
SUBLANES = 8

RMS_EPS = 1e-6
HEAD_DIM = 64
N_HEADS = 8
WIDTH = N_HEADS * HEAD_DIM
RET_CHUNK = 128
RET_BATCH = 2
RET_GN_EPS = 1e-5
ROPE_BASE = 10000.0
RWKV_GN_EPS = 64e-5
L2_EPS = 1e-12
DECAY_LORA = 64
AAA_LORA = 64
GATE_LORA = 128
RWKV_COLS = 3 * WIDTH + DECAY_LORA + AAA_LORA + GATE_LORA
WKV_CHUNK = 64
WKV_BATCH = 4

PEER_HEADS = 8
PEER_N_KEYS = 128
PEER_HALF = 128
PEER_TOPK = 16
PEER_GROUPS = 2 * PEER_HEADS
PEER_SEL = PEER_HEADS * PEER_TOPK
PEER_ROUTE_TOK = 128
PEER_ROUTE_EVERY = 2
PEER_RANGES = 8
PEER_TOK = 8
PEER_GROUPS_PER_STEP = 4
PEER_AHEAD = 2
PEER_SKEW = 2
PEER_CHUNK = 32

ROW_TILE = 256
VMEM_LIMIT = 48 * 1024 * 1024

_NT = (((1,), (1,)), ((), ()))
_TN = (((0,), (0,)), ((), ()))


def _bdot(a, b):
    return jnp.dot(a.astype(BF16), b.astype(BF16), preferred_element_type=F32)


def _bdot_nt(a, b):
    return lax.dot_general(a.astype(BF16), b.astype(BF16), _NT, preferred_element_type=F32)


def _bdot_tn(a, b):
    return lax.dot_general(a.astype(BF16), b.astype(BF16), _TN, preferred_element_type=F32)


def _split2(a):
    hi = a.astype(BF16)
    lo = (a - hi.astype(F32)).astype(BF16)
    return hi, lo


def _seg_dot(a, m):
    hi, lo = _split2(a)
    return (jnp.dot(hi, m, preferred_element_type=F32)
            + jnp.dot(lo, m, preferred_element_type=F32))


def _rms(x, g):
    return x * lax.rsqrt(jnp.mean(x * x, axis=-1, keepdims=True) + RMS_EPS) * g


def _sigmoid(x):
    return 1.0 / (1.0 + jnp.exp(-x))


def _head_norm(o, mavg, eps):
    mu = _seg_dot(o, mavg)
    oc = o - mu
    var = _seg_dot(oc * oc, mavg)
    return oc * lax.rsqrt(var + eps)


def _cparams(sem, vmem=VMEM_LIMIT):
    return pltpu.CompilerParams(dimension_semantics=sem, vmem_limit_bytes=vmem)


def _full(shape):
    nd = len(shape)
    return pl.BlockSpec(shape, lambda *_: (0,) * nd)


def _in_proj_kernel(x_ref, g_ref, w1_ref, w2_ref, w3_ref, o1_ref, o2_ref, o3_ref):
    h = _rms(x_ref[...], g_ref[...]).astype(BF16)
    o1_ref[...] = jnp.dot(h, w1_ref[...], preferred_element_type=F32).astype(o1_ref.dtype)
    o2_ref[...] = jnp.dot(h, w2_ref[...], preferred_element_type=F32).astype(o2_ref.dtype)
    o3_ref[...] = jnp.dot(h, w3_ref[...], preferred_element_type=F32).astype(o3_ref.dtype)


def _in_proj(x2, g, w_ret, w_rwkv, w_gate):
    T, D = x2.shape
    tm = ROW_TILE
    ws = (w_ret, w_rwkv, w_gate)
    return pl.pallas_call(
        _in_proj_kernel,
        grid=(T // tm,),
        in_specs=[pl.BlockSpec((tm, D), lambda i: (i, 0)), _full((1, D))]
        + [_full(w.shape) for w in ws],
        out_specs=[pl.BlockSpec((tm, w.shape[1]), lambda i: (i, 0)) for w in ws],
        out_shape=[jax.ShapeDtypeStruct((T, w.shape[1]), dt) for w, dt in zip(ws, (BF16, F32, BF16))],
        compiler_params=_cparams(("parallel",)),
        name="in_proj",
    )(x2, g, *ws)


def _retention_kernel(z_ref, cos_ref, sin_ref, xi_ref, zeta_ref, decay_ref, cd_ref,
                      gn_ref, mavg_ref, y_ref, state_ref):
    @pl.when(pl.program_id(1) == 0)
    def _():
        state_ref[...] = jnp.zeros_like(state_ref)

    cos = cos_ref[...]
    sin = sin_ref[...]
    lane = lax.broadcasted_iota(jnp.int32, cos.shape, 1)
    first_half = (lane % HEAD_DIM) < (HEAD_DIM // 2)

    def rot(x):
        partner = jnp.where(first_half, pltpu.roll(x, WIDTH - HEAD_DIM // 2, 1),
                            pltpu.roll(x, HEAD_DIM // 2, 1))
        return x * cos + partner * sin

    qs, ks, vs, qxs, kzs, grs = [], [], [], [], [], []
    for b in range(RET_BATCH):
        z = z_ref[b].astype(F32)
        v = z[:, 2 * WIDTH:3 * WIDTH]
        grs.append(z[:, 3 * WIDTH:4 * WIDTH])
        qr = rot(z[:, 0:WIDTH])
        kr = rot(z[:, WIDTH:2 * WIDTH]) * (HEAD_DIM ** -0.5)
        qx = qr * xi_ref[...]
        kz = kr * zeta_ref[...]
        for h in range(N_HEADS):
            sl = slice(h * HEAD_DIM, (h + 1) * HEAD_DIM)
            qs.append(qr[:, sl].astype(BF16))
            ks.append(kr[:, sl].astype(BF16))
            vs.append(v[:, sl].astype(BF16))
            qxs.append(qx[:, sl])
            kzs.append(kz[:, sl])
    chains = range(RET_BATCH * N_HEADS)
    scores = [_bdot_nt(qs[c], ks[c]) * decay_ref[c % N_HEADS] for c in chains]
    states = [state_ref[c] for c in chains]
    cross = [_bdot(qxs[c], states[c]) for c in chains]
    outs = [_bdot(scores[c], vs[c]) + cross[c] for c in chains]
    for c in chains:
        state_ref[c] = states[c] * cd_ref[c % N_HEADS] + _bdot_tn(kzs[c], vs[c])
    for b in range(RET_BATCH):
        o = jnp.concatenate(outs[b * N_HEADS:(b + 1) * N_HEADS], axis=1)
        y = _head_norm(o, mavg_ref[...], RET_GN_EPS)
        y_ref[b] = (grs[b] * _sigmoid(grs[b]) * (y * gn_ref[...])).astype(BF16)


def _retention(z_ret, gn_g, mavg):
    B, S, _ = z_ret.shape
    C = RET_CHUNK
    half = HEAD_DIM // 2
    inv_freq = ROPE_BASE ** (-jnp.arange(half, dtype=F32) * 2.0 / HEAD_DIM)
    ang = jnp.arange(S, dtype=F32)[:, None] * inv_freq[None, :]
    cos_h = jnp.concatenate([jnp.cos(ang), jnp.cos(ang)], axis=1)
    sin_h = jnp.concatenate([-jnp.sin(ang), jnp.sin(ang)], axis=1)
    cos = jnp.tile(cos_h, (1, N_HEADS))
    sin = jnp.tile(sin_h, (1, N_HEADS))
    log_gamma = jnp.log1p(-(2.0 ** (-5.0 - jnp.arange(N_HEADS, dtype=F32))))
    idx = jnp.arange(C, dtype=F32)
    diff = idx[:, None] - idx[None, :]
    causal = diff >= 0
    decay = jnp.where(causal[None], jnp.exp(log_gamma[:, None, None] * jnp.where(causal, diff, 0.0)[None]), 0.0)
    zeta = jnp.exp(log_gamma[:, None] * (C - 1.0 - idx)[None, :])
    xi = jnp.exp(log_gamma[:, None] * (idx + 1.0)[None, :])
    widen = lambda t: jnp.repeat(t.T, HEAD_DIM, axis=1)
    cd = jnp.broadcast_to(jnp.exp(log_gamma * C)[:, None, None], (N_HEADS, HEAD_DIM, HEAD_DIM))
    return pl.pallas_call(
        _retention_kernel,
        grid=(B // RET_BATCH, S // C),
        in_specs=[pl.BlockSpec((RET_BATCH, C, 4 * WIDTH), lambda b, c: (b, c, 0)),
                  pl.BlockSpec((C, WIDTH), lambda b, c: (c, 0)),
                  pl.BlockSpec((C, WIDTH), lambda b, c: (c, 0)),
                  _full((C, WIDTH)), _full((C, WIDTH)), _full((N_HEADS, C, C)),
                  _full((N_HEADS, HEAD_DIM, HEAD_DIM)), _full((1, WIDTH)), _full((WIDTH, WIDTH))],
        out_specs=pl.BlockSpec((RET_BATCH, C, WIDTH), lambda b, c: (b, c, 0)),
        out_shape=jax.ShapeDtypeStruct((B, S, WIDTH), BF16),
        scratch_shapes=[pltpu.VMEM((RET_BATCH * N_HEADS, HEAD_DIM, HEAD_DIM), F32)],
        compiler_params=_cparams(("parallel", "arbitrary")),
        name="retention",
    )(z_ret, cos, sin, widen(xi), widen(zeta), decay, cd, gn_g, mavg)


def _rwkv_prep_kernel(z_ref, mu_ref, w0_ref, wup_ref, a0_ref, aup_ref, gup_ref, kk_ref,
                      ka_ref, rk_ref, mones_ref,
                      r_out, lw_out, k_out, v_out, kk_out, kka_out, g_out, bonus_out,
                      carry_ref):
    @pl.when(pl.program_id(1) == 0)
    def _():
        carry_ref[...] = jnp.zeros_like(carry_ref)

    z = z_ref[0]
    n = z.shape[0]
    row = lax.broadcasted_iota(jnp.int32, z.shape, 0)
    prev = jnp.where(row == 0, carry_ref[0:1, :], pltpu.roll(z, 1, 0))
    carry_ref[0:1, :] = z[n - 1:n, :]
    zs = z + (prev - z) * mu_ref[...]
    r = zs[:, 0:WIDTH]
    kr = zs[:, WIDTH:2 * WIDTH]
    vr = zs[:, 2 * WIDTH:3 * WIDTH]
    o = 3 * WIDTH
    wl = zs[:, o:o + DECAY_LORA]
    al = zs[:, o + DECAY_LORA:o + DECAY_LORA + AAA_LORA]
    gl = zs[:, o + DECAY_LORA + AAA_LORA:]
    t = -(w0_ref[...] + _bdot(jnp.tanh(wl), wup_ref[...]))
    softplus = jnp.maximum(t, 0.0) + jnp.log1p(jnp.exp(-jnp.abs(t)))
    w_log = -softplus - 0.5
    a = _sigmoid(a0_ref[...] + _bdot(al, aup_ref[...]))
    g = _bdot(_sigmoid(gl), gup_ref[...])
    mones = mones_ref[...]
    kk = kr * kk_ref[...]
    norm = jnp.sqrt(_seg_dot(kk * kk, mones))
    kk = kk / jnp.maximum(norm, L2_EPS)
    k2 = kr * (1.0 + (a - 1.0) * ka_ref[...])
    r_out[0] = r.astype(BF16)
    lw_out[0] = -jnp.exp(w_log)
    k_out[0] = k2.astype(BF16)
    v_out[0] = vr.astype(BF16)
    kk_out[0] = kk.astype(BF16)
    kka_out[0] = (kk * a).astype(BF16)
    g_out[0] = g.astype(BF16)
    bonus_out[0] = (_seg_dot(r * k2 * rk_ref[...], mones) * vr).astype(BF16)


def _rwkv_prep(z_rwkv, mu, w0, w_up, a0, a_up, g_up, k_k, k_a, r_k, mones):
    B, S, _ = z_rwkv.shape
    ts = ROW_TILE
    row = lambda t: t.reshape(1, -1)
    args = (row(mu), row(w0), w_up.astype(BF16), row(a0), a_up.astype(BF16),
            g_up.astype(BF16), row(k_k), row(k_a), row(r_k), mones)
    out_spec = pl.BlockSpec((1, ts, WIDTH), lambda b, s: (b, s, 0))
    return pl.pallas_call(
        _rwkv_prep_kernel,
        grid=(B, S // ts),
        in_specs=[pl.BlockSpec((1, ts, RWKV_COLS), lambda b, s: (b, s, 0))]
        + [_full(a.shape) for a in args],
        out_specs=[out_spec] * 8,
        out_shape=[jax.ShapeDtypeStruct((B, S, WIDTH), F32 if n == 1 else BF16) for n in range(8)],
        scratch_shapes=[pltpu.VMEM((8, RWKV_COLS), F32)],
        compiler_params=_cparams(("parallel", "arbitrary")),
        name="rwkv_prep",
    )(z_rwkv, *args)


def _wkv7_kernel(r_ref, lw_ref, k_ref, v_ref, kk_ref, kka_ref, tri_ref, o_ref, state_ref):
    @pl.when(pl.program_id(1) == 0)
    def _():
        state_ref[...] = jnp.zeros_like(state_ref)

    L = WKV_CHUNK
    tri = tri_ref[...]
    ri = lax.broadcasted_iota(jnp.int32, (2 * L, L), 0)
    ci = lax.broadcasted_iota(jnp.int32, (2 * L, L), 1)
    mask = jnp.where(ri < L, ri, ri - L + 1) > ci
    eye = (lax.broadcasted_iota(jnp.int32, (L, L), 0)
           == lax.broadcasted_iota(jnp.int32, (L, L), 1)).astype(F32)

    ar, vf, bt, kt, bke, gl = [], [], [], [], [], []
    for b in range(WKV_BATCH):
        lw = lw_ref[b]
        hi = lw.astype(BF16)
        rem = lw - hi.astype(F32)
        mid = rem.astype(BF16)
        lo = (rem - mid.astype(F32)).astype(BF16)
        cum = (jnp.dot(tri, hi, preferred_element_type=F32)
               + jnp.dot(tri, mid, preferred_element_type=F32)
               + jnp.dot(tri, lo, preferred_element_type=F32))
        cum_last = cum[L - 1:L, :]
        inv_g = jnp.exp(-cum)
        to_end = jnp.exp(cum_last - cum)
        g_last = jnp.exp(cum_last)
        kk = kk_ref[b].astype(F32)
        kka = kka_ref[b].astype(F32)
        k = k_ref[b].astype(F32)
        v = v_ref[b].astype(F32)
        a_t = -kk * jnp.exp(cum - lw)
        b_t = kka * inv_g
        k_t = k * inv_g
        r_t = r_ref[b].astype(F32) * jnp.exp(cum)
        b_end = kka * to_end
        k_end = k * to_end
        for h in range(N_HEADS):
            sl = slice(h * HEAD_DIM, (h + 1) * HEAD_DIM)
            ar.append(jnp.concatenate([a_t[:, sl], r_t[:, sl]], axis=0).astype(BF16))
            vf.append(v[:, sl])
            bt.append(b_t[:, sl])
            kt.append(k_t[:, sl])
            bke.append(jnp.concatenate([b_end[:, sl], k_end[:, sl]], axis=0))
            gl.append(g_last[:, sl])
    chains = range(WKV_BATCH * N_HEADS)
    vs = [x.astype(BF16) for x in vf]
    abrb = [jnp.where(mask, _bdot_nt(ar[c], bt[c]), 0.0) for c in chains]
    akrk = [jnp.where(mask, _bdot_nt(ar[c], kt[c]), 0.0) for c in chains]
    p = [m[:L] for m in abrb]
    inv = [eye + m for m in p]
    for _ in range(int(math.log2(L)) - 1):
        p = [_bdot(m, m) for m in p]
        inv = [inv[c] + _bdot(inv[c], p[c]) for c in chains]
    states = [state_ref[c] for c in chains]
    xs = [_bdot_nt(ar[c], states[c]) for c in chains]
    kv = [_bdot(akrk[c], vs[c]) for c in chains]
    u = [_bdot(inv[c], xs[c][:L] + kv[c][:L]) for c in chains]
    y = [xs[c][L:] + kv[c][L:] + _bdot(abrb[c][L:], u[c]) for c in chains]
    for c in chains:
        uv = jnp.concatenate([u[c], vf[c]], axis=0)
        state_ref[c] = states[c] * gl[c] + _bdot_tn(uv, bke[c])
    for b in range(WKV_BATCH):
        o_ref[b] = jnp.concatenate(y[b * N_HEADS:(b + 1) * N_HEADS], axis=1)


def _wkv7(r, lw, k, v, kk, kka):
    B, S, _ = r.shape
    L = WKV_CHUNK
    nb = WKV_BATCH
    tri = jnp.tril(jnp.ones((L, L), F32)).astype(BF16)
    spec = pl.BlockSpec((nb, L, WIDTH), lambda b, c: (b, c, 0))
    return pl.pallas_call(
        _wkv7_kernel,
        grid=(B // nb, S // L),
        in_specs=[spec] * 6 + [_full((L, L))],
        out_specs=spec,
        out_shape=jax.ShapeDtypeStruct((B, S, WIDTH), F32),
        scratch_shapes=[pltpu.VMEM((nb * N_HEADS, HEAD_DIM, HEAD_DIM), F32)],
        compiler_params=_cparams(("parallel", "arbitrary")),
        name="wkv7",
    )(r, lw, k, v, kk, kka, tri)


def _merge_kernel(x_ref, yret_ref, o_ref, bonus_ref, g_ref, zg_ref, gng_ref, gnb_ref,
                  mavg_ref, wret_ref, wrwkv_ref, wo_ref, x1_ref):
    y = _head_norm(o_ref[...], mavg_ref[...], RWKV_GN_EPS)
    y_rwkv = (y * gng_ref[...] + gnb_ref[...] + bonus_ref[...].astype(F32)) * g_ref[...].astype(F32)
    br = _bdot(yret_ref[...], wret_ref[...])
    bw = _bdot(y_rwkv, wrwkv_ref[...])
    zg = zg_ref[...].astype(F32)
    d = br.shape[1]
    merged = _sigmoid(zg[:, :d]) * br + _sigmoid(zg[:, d:]) * bw
    x1_ref[...] = x_ref[...] + _bdot(merged, wo_ref[...])


def _merge(x2, y_ret, o, bonus, g, z_gate, gn_g, gn_b, mavg, w_ret_br, w_rwkv_br, w_o):
    T, D = x2.shape
    tm = ROW_TILE
    rows = lambda n: pl.BlockSpec((tm, n), lambda i: (i, 0))
    consts = (gn_g, gn_b, mavg, w_ret_br, w_rwkv_br, w_o)
    return pl.pallas_call(
        _merge_kernel,
        grid=(T // tm,),
        in_specs=[rows(D), rows(WIDTH), rows(WIDTH), rows(WIDTH), rows(WIDTH), rows(2 * D)]
        + [_full(c.shape) for c in consts],
        out_specs=rows(D),
        out_shape=jax.ShapeDtypeStruct((T, D), F32),
        compiler_params=_cparams(("parallel",)),
        name="merge",
    )(x2, y_ret, o, bonus, g, z_gate, *consts)


def _peer_scores_kernel(x_ref, g_ref, wq_ref, keys_ref, h_ref, s_ref):
    h = _rms(x_ref[...], g_ref[...])
    h_ref[...] = h.reshape(h_ref.shape)
    q = _bdot(h, wq_ref[...]).astype(BF16)
    for grp in range(PEER_GROUPS):
        s_ref[grp] = lax.dot_general(keys_ref[grp], q[:, grp * PEER_HALF:(grp + 1) * PEER_HALF],
                                     _NT, preferred_element_type=F32)


def _peer_scores(x1, g_ffn, w_pq, keys):
    T, D = x1.shape
    tm = ROW_TILE
    return pl.pallas_call(
        _peer_scores_kernel,
        grid=(T // tm,),
        in_specs=[pl.BlockSpec((tm, D), lambda i: (i, 0)), _full((1, D)), _full(w_pq.shape),
                  _full(keys.shape)],
        out_specs=[pl.BlockSpec((tm, D // LANES, LANES), lambda i: (i, 0, 0)),
                   pl.BlockSpec((PEER_GROUPS, PEER_N_KEYS, tm), lambda i: (0, 0, i))],
        out_shape=[jax.ShapeDtypeStruct((T, D // LANES, LANES), F32),
                   jax.ShapeDtypeStruct((PEER_GROUPS, PEER_N_KEYS, T), F32)],
        compiler_params=_cparams(("parallel",)),
        name="peer_scores",
    )(x1, g_ffn, w_pq, keys)


def _top_rows_steps(s, count, payload=None):
    rows = lax.broadcasted_iota(jnp.int32, s.shape, 0).astype(F32)
    vals, picks = [], []
    for _ in range(count):
        m = jnp.max(s, axis=0, keepdims=True)
        idx = jnp.min(jnp.where(s == m, rows, float(s.shape[0])), axis=0, keepdims=True)
        hit = rows == idx
        s = jnp.where(hit, -jnp.inf, s)
        vals.append(m)
        if payload is None:
            picks.append(idx)
        else:
            picks.append(jnp.max(jnp.where(hit, payload, -1), axis=0, keepdims=True))
        yield
    return jnp.concatenate(vals, axis=0), jnp.concatenate(picks, axis=0)


def _run(steps):
    try:
        while True:
            next(steps)
    except StopIteration as done:
        return done.value


def _pair_rows(a, b):
    K = PEER_TOPK
    out = []
    for i in range(K // 2):
        jn = K if i == 0 else K // 2
        out.append((a[i:i + 1, :], b[0:jn, :]))
    out.append((a[K // 2:K, :], b[0:1, :]))
    return out


def _peer_topk_kernel(s_ref, ids_ref, gates_ref):
    K = PEER_TOPK

    def head(h, carry):
        ids, gates = _run(_route_head_steps(lambda: s_ref[2 * h], lambda: s_ref[2 * h + 1]))
        off = pl.multiple_of(h * K, K)
        ids_ref[pl.ds(off, K), :] = ids
        gates_ref[pl.ds(off, K), :] = gates
        return carry

    lax.fori_loop(0, PEER_HEADS, head, 0)


def _route_head_steps(load_scores0, load_scores1):
    K = PEER_TOPK
    s0, i0 = yield from _top_rows_steps(load_scores0(), K)
    s1, i1 = yield from _top_rows_steps(load_scores1(), K)
    e0 = i0.astype(jnp.int32) * PEER_N_KEYS
    e1 = i1.astype(jnp.int32)
    cand = jnp.concatenate([x + y for x, y in _pair_rows(s0, s1)], axis=0)
    cand_id = jnp.concatenate([x + y for x, y in _pair_rows(e0, e1)], axis=0)
    best, ids = yield from _top_rows_steps(cand, K, payload=cand_id)
    e = jnp.exp(best - best[0:1, :])
    return ids, e / jnp.sum(e, axis=0, keepdims=True)


def _peer_topk(scores_t, n_tokens):
    T = n_tokens
    tk = PEER_ROUTE_TOK
    return pl.pallas_call(
        _peer_topk_kernel,
        grid=(T // tk,),
        in_specs=[pl.BlockSpec((PEER_GROUPS, PEER_N_KEYS, tk), lambda i: (0, 0, i))],
        out_specs=[pl.BlockSpec((PEER_SEL, tk), lambda i: (0, i))] * 2,
        out_shape=[jax.ShapeDtypeStruct((PEER_SEL, T), jnp.int32),
                   jax.ShapeDtypeStruct((PEER_SEL, T), F32)],
        compiler_params=_cparams(("parallel",)),
        name="peer_topk",
    )(scores_t)


def _gelu_tanh(x):
    return 0.5 * x * (1.0 + jnp.tanh(math.sqrt(2.0 / math.pi) * (x + 0.044715 * x * x * x)))


def _peer_experts_kernel(off_ref, ids_cur, ids_next, gexp_ref, h_ref, uv_hbm, s_ref, out_ref, nids_ref,
                         ngates_ref, buf, sem, rids_scr, rgates_scr):
    i = pl.program_id(0)
    n = pl.num_programs(0)
    sub = h_ref.shape[1]
    group_rows = PEER_TOK * PEER_SEL

    def issue_token(ids_ref, tok, group, t, k0=0, count=PEER_SEL):
        for k in range(k0, k0 + count):
            e = ids_ref[0, tok, k]
            pltpu.make_async_copy(uv_hbm.at[pl.ds(e, 1)],
                                  buf.at[group, pl.ds(t * PEER_SEL + k, 1)],
                                  sem.at[group]).start(priority=k % 2)

    def wait_group(group):
        pltpu.make_async_copy(uv_hbm.at[pl.ds(0, group_rows)], buf.at[group], sem.at[group]).wait()

    @pl.when(i == 0)
    def _():
        for group in range(PEER_AHEAD):
            def body(t, carry, group=group):
                issue_token(ids_cur, group * PEER_TOK + t, group, t)
                return carry
            lax.fori_loop(0, PEER_TOK, body, 0)

    ch = PEER_CHUNK
    n_ch = PEER_SEL // ch
    ccols = ch * sub
    lane = lax.broadcasted_iota(jnp.int32, (sub, ccols), 1)
    diag = (lane % sub) == lax.broadcasted_iota(jnp.int32, (sub, ccols), 0)

    def group_sum(x):
        step = 1
        while step < sub:
            partner = jnp.where((lane % (2 * step)) < step, pltpu.roll(x, ccols - step, 1),
                                pltpu.roll(x, step, 1))
            x = x + partner
            step *= 2
        return x

    tokens = PEER_GROUPS_PER_STEP * PEER_TOK
    burst = PEER_SEL // (2 * n_ch)
    issued = [0]

    def routing_steps():
        routed = []
        for j in range(s_ref.shape[0] // 2):
            routed.append((yield from _route_head_steps(lambda j=j: s_ref[2 * j],
                                                        lambda j=j: s_ref[2 * j + 1])))
        return routed

    routing = routing_steps()
    routed = []

    def routing_tick():
        if not routed:
            try:
                next(routing)
            except StopIteration as done:
                routed.extend(done.value)

    def request_burst():
        for r in range(issued[0], issued[0] + burst):
            tok, k = divmod(r, PEER_SEL)
            ahead = tok // PEER_TOK + PEER_AHEAD
            ids_ref = ids_cur if ahead < PEER_GROUPS_PER_STEP else ids_next
            group = ahead % PEER_GROUPS_PER_STEP
            issue_token(ids_ref, group * PEER_TOK + tok % PEER_TOK, group, tok % PEER_TOK, k, 1)
        issued[0] += burst
        if (issued[0] // burst) % PEER_ROUTE_EVERY == 0:
            routing_tick()

    def first_layer(tok):
        group, t = divmod(tok, PEER_TOK)
        if t == 0:
            wait_group(group)
        hb = h_ref[tok].astype(BF16)
        ps = []
        for c in range(n_ch):
            request_burst()
            rows = pl.ds(t * PEER_SEL + c * ch, ch)
            uc = buf[group, rows, 0:sub, :].reshape(ccols, LANES).astype(BF16)
            ps.append(lax.dot_general(hb, uc, _NT, preferred_element_type=F32))
        return ps

    def weights(tok, ps):
        ws = []
        for c in range(n_ch):
            act = jnp.sum(group_sum(jnp.where(diag, ps[c], 0.0)), axis=0, keepdims=True)
            w = _gelu_tanh(act) * gexp_ref[tok:tok + 1, c * ccols:(c + 1) * ccols]
            ws.append(jnp.where(diag, jnp.broadcast_to(w, (sub, ccols)), 0.0).astype(BF16))
        return ws

    def second_layer(tok, ws):
        group, t = divmod(tok, PEER_TOK)
        o = None
        for c in range(n_ch):
            request_burst()
            rows = pl.ds(t * PEER_SEL + c * ch, ch)
            vc = buf[group, rows, sub:2 * sub, :].reshape(ccols, LANES).astype(BF16)
            part = jnp.dot(ws[c], vc, preferred_element_type=F32)
            o = part if o is None else o + part
        return o

    ps = {tok: first_layer(tok) for tok in range(PEER_SKEW)}
    outs = []
    for tok in range(tokens):
        if tok + PEER_SKEW < tokens:
            ps[tok + PEER_SKEW] = first_layer(tok + PEER_SKEW)
        outs.append(second_layer(tok, weights(tok, ps.pop(tok))))
    assert issued[0] == tokens * PEER_SEL
    while not routed:
        routing_tick()
    for tok, o in enumerate(outs):
        out_ref[tok] = o
    rsteps = PEER_SEL // (len(routed) * PEER_TOPK)
    part = i % rsteps
    for j, (ids, gates) in enumerate(routed):
        row0 = pl.multiple_of((part * len(routed) + j) * PEER_TOPK, PEER_TOPK)
        rids_scr[pl.ds(row0, PEER_TOPK), :] = ids
        rgates_scr[pl.ds(row0, PEER_TOPK), :] = gates

    @pl.when(part == rsteps - 1)
    def _():
        nids_ref[...] = rids_scr[...].T
        ngates_ref[...] = rgates_scr[...].T

    @pl.when(i == n - 1)
    def _():
        for group in range(PEER_AHEAD):
            wait_group(group)


def _peer_experts(ids, gates, h_tiles, uv, scores_t, rng, next_rng, n_ranges):
    T, sub, _ = h_tiles.shape
    R = T // n_ranges
    tb = PEER_GROUPS_PER_STEP * PEER_TOK
    nb = R // tb
    rsteps = PEER_ROUTE_TOK // tb
    rgroups = PEER_GROUPS // rsteps
    assert sub == SUBLANES, "a model vector must be exactly one (sublanes, lanes) tile"
    assert T == n_ranges * nb * tb and nb % rsteps == 0 and PEER_ROUTE_TOK == rsteps * tb
    assert PEER_GROUPS % rsteps == 0 and rgroups % 2 == 0
    assert PEER_AHEAD < PEER_GROUPS_PER_STEP and PEER_SKEW < tb and PEER_SEL % PEER_CHUNK == 0
    ids3 = ids.reshape(nb, tb, PEER_SEL)
    gexp = jnp.repeat(gates, sub, axis=1)
    offsets = jnp.array([rng * nb, next_rng * (R // PEER_ROUTE_TOK)], jnp.int32)
    smem = lambda imap: pl.BlockSpec((1, tb, PEER_SEL), imap, memory_space=pltpu.SMEM)
    routed = pl.BlockSpec((PEER_ROUTE_TOK, PEER_SEL), lambda i, off: (i // rsteps, 0))
    grid_spec = pltpu.PrefetchScalarGridSpec(
        num_scalar_prefetch=1,
        grid=(nb,),
        in_specs=[smem(lambda i, off: (i, 0, 0)),
                  smem(lambda i, off: (jnp.minimum(i + 1, nb - 1), 0, 0)),
                  pl.BlockSpec((tb, PEER_SEL * sub), lambda i, off: (i, 0)),
                  pl.BlockSpec((tb, sub, LANES), lambda i, off: (off[0] + i, 0, 0)),
                  pl.BlockSpec(memory_space=pl.ANY),
                  pl.BlockSpec((rgroups, PEER_N_KEYS, PEER_ROUTE_TOK),
                               lambda i, off: (i % rsteps, 0, off[1] + i // rsteps))],
        out_specs=[pl.BlockSpec((tb, sub, LANES), lambda i, off: (i, 0, 0)), routed, routed],
        scratch_shapes=[pltpu.VMEM((PEER_GROUPS_PER_STEP, PEER_TOK * PEER_SEL, 2 * sub, LANES), F32),
                        pltpu.SemaphoreType.DMA((PEER_GROUPS_PER_STEP,)),
                        pltpu.VMEM((PEER_SEL, PEER_ROUTE_TOK), jnp.int32),
                        pltpu.VMEM((PEER_SEL, PEER_ROUTE_TOK), F32)])
    return pl.pallas_call(
        _peer_experts_kernel,
        grid_spec=grid_spec,
        out_shape=[jax.ShapeDtypeStruct((R, sub, LANES), F32),
                   jax.ShapeDtypeStruct((R, PEER_SEL), jnp.int32),
                   jax.ShapeDtypeStruct((R, PEER_SEL), F32)],
        compiler_params=_cparams(("arbitrary",)),
        name="peer_experts",
    )(offsets, ids3, ids3, gexp, h_tiles, uv, scores_t)


def _ple_final_kernel(x_ref, d_ref, p_ref, gple_ref, wg_ref, wu_ref, gfin_ref, out_ref):
    x = x_ref[...] + d_ref[...].reshape(x_ref.shape)
    gate = _sigmoid(_bdot(_rms(x, gple_ref[...]), wg_ref[...]))
    x = x + gate * _bdot(p_ref[...], wu_ref[...])
    out_ref[...] = _rms(x, gfin_ref[...])


def _ple_final(x2, delta_tiles, p2, g_ple, w_gate, w_up, g_final):
    T, D = x2.shape
    tm = ROW_TILE
    consts = (g_ple, w_gate, w_up, g_final)
    return pl.pallas_call(
        _ple_final_kernel,
        grid=(T // tm,),
        in_specs=[pl.BlockSpec((tm, D), lambda i: (i, 0)),
                  pl.BlockSpec((tm,) + delta_tiles.shape[1:], lambda i: (i, 0, 0)),
                  pl.BlockSpec((tm, p2.shape[1]), lambda i: (i, 0)),
                  _full(g_ple.shape), _full(w_gate.shape), _full(w_up.shape), _full(g_final.shape)],
        out_specs=pl.BlockSpec((tm, D), lambda i: (i, 0)),
        out_shape=jax.ShapeDtypeStruct((T, D), F32),
        compiler_params=_cparams(("parallel",)),
        name="ple_final",
    )(x2, delta_tiles, p2, *consts)


def kernel(x, p, g_mix, w_in, ret_gn_g, rwkv_mu, rwkv_w0, rwkv_w_up, rwkv_a0, rwkv_a_up, rwkv_g_up, rwkv_k_k, rwkv_k_a, rwkv_r_k, rwkv_gn_g, rwkv_gn_b, w_ret_br, w_rwkv_br, w_o, g_ffn, w_pq, peer_sub_keys, peer_u, peer_v, g_ple, w_ple_gate, w_ple_up, g_final):
    B, S, D = x.shape
    T = B * S
    assert w_in.shape[0] == 1, "single-layer block: the final RMSNorm is fused into its last step"
    assert T % ROW_TILE == 0 and S % ROW_TILE == 0 and S % RET_CHUNK == 0 and S % WKV_CHUNK == 0
    assert B % RET_BATCH == 0 and B % WKV_BATCH == 0 and D % LANES == 0
    assert T % (PEER_RANGES * PEER_ROUTE_TOK) == 0
    i = 0
    row = lambda t: t.reshape(1, -1)
    head_of = jnp.arange(WIDTH) // HEAD_DIM
    same_head = head_of[:, None] == head_of[None, :]
    mones = same_head.astype(BF16)
    mavg = (same_head.astype(F32) / HEAD_DIM).astype(BF16)
    ret_cols = 4 * WIDTH
    x2 = x.reshape(T, D)
    wi = w_in[i].astype(BF16)
    z_ret, z_rwkv, z_gate = _in_proj(
        x2, row(g_mix[i]), wi[:, :ret_cols], wi[:, ret_cols:ret_cols + RWKV_COLS],
        wi[:, ret_cols + RWKV_COLS:])
    y_ret = _retention(z_ret.reshape(B, S, ret_cols), row(ret_gn_g[i]), mavg)
    r, lw, k, v, kk, kka, g, bonus = _rwkv_prep(
        z_rwkv.reshape(B, S, RWKV_COLS), rwkv_mu[i], rwkv_w0[i], rwkv_w_up[i], rwkv_a0[i],
        rwkv_a_up[i], rwkv_g_up[i], rwkv_k_k[i], rwkv_k_a[i], rwkv_r_k[i], mones)
    o = _wkv7(r, lw, k, v, kk, kka)
    flat = lambda t: t.reshape(T, WIDTH)
    x2 = _merge(x2, flat(y_ret), flat(o), flat(bonus), flat(g), z_gate,
                row(rwkv_gn_g[i]), row(rwkv_gn_b[i]), mavg, w_ret_br[i].astype(BF16),
                w_rwkv_br[i].astype(BF16), w_o[i].astype(BF16))
    keys = peer_sub_keys[i].reshape(PEER_GROUPS, PEER_N_KEYS, PEER_HALF).astype(BF16)
    h_tiles, scores_t = _peer_scores(x2, row(g_ffn[i]), w_pq[i].astype(BF16), keys)
    n_exp = peer_u.shape[1]
    uv = jnp.concatenate([peer_u[i].reshape(n_exp, D // LANES, LANES),
                          peer_v[i].reshape(n_exp, D // LANES, LANES)], axis=1)
    R = T // PEER_RANGES
    ids_t, gates_t = _peer_topk(scores_t, R)
    ids, gates = ids_t.T, gates_t.T
    deltas = []
    for c in range(PEER_RANGES):
        delta, ids, gates = _peer_experts(ids, gates, h_tiles, uv, scores_t, c,
                                          min(c + 1, PEER_RANGES - 1), PEER_RANGES)
        deltas.append(delta)
    delta_tiles = jnp.concatenate(deltas, axis=0)
    out = _ple_final(x2, delta_tiles, p[i].reshape(T, -1), row(g_ple[i]),
                     w_ple_gate[i].astype(BF16), w_ple_up[i].astype(BF16), row(g_final))
    return out.reshape(B, S, D)
```

```python
import math

import jax
import jax.numpy as jnp
from jax import lax
from jax.experimental import pallas as pl
from jax.experimental.pallas import tpu as pltpu

F32 = jnp.float32
BF16 = jnp.bfloat16

LANES = 128
SUBLANES = 8

RMS_EPS = 1e-6
HEAD_DIM = 64
N_HEADS = 8
WIDTH = N_HEADS * HEAD_DIM
RET_CHUNK = 128
RET_BATCH = 2
RET_GN_EPS = 1e-5
ROPE_BASE = 10000.0
RWKV_GN_EPS = 64e-5
L2_EPS = 1e-12
DECAY_LORA = 64
AAA_LORA = 64
GATE_LORA = 128
RWKV_COLS = 3 * WIDTH + DECAY_LORA + AAA_LORA + GATE_LORA
WKV_CHUNK = 64
WKV_BATCH = 4

PEER_HEADS = 8
PEER_N_KEYS = 128
PEER_HALF = 128
PEER_TOPK = 16
PEER_GROUPS = 2 * PEER_HEADS
PEER_SEL = PEER_HEADS * PEER_TOPK
PEER_ROUTE_TOK = 128
PEER_ROUTE_EVERY = 2
PEER_RANGES = 8
PEER_TOK = 8
PEER_GROUPS_PER_STEP = 4
PEER_AHEAD = 3
PEER_SKEW = 2
PEER_CHUNK = 32

ROW_TILE = 256
VMEM_LIMIT = 48 * 1024 * 1024

_NT = (((1,), (1,)), ((), ()))
_TN = (((0,), (0,)), ((), ()))


def _bdot(a, b):
    return jnp.dot(a.astype(BF16), b.astype(BF16), preferred_element_type=F32)


def _bdot_nt(a, b):
    return lax.dot_general(a.astype(BF16), b.astype(BF16), _NT, preferred_element_type=F32)


def _bdot_tn(a, b):
    return lax.dot_general(a.astype(BF16), b.astype(BF16), _TN, preferred_element_type=F32)


def _split2(a):
    hi = a.astype(BF16)
    lo = (a - hi.astype(F32)).astype(BF16)
    return hi, lo


def _seg_dot(a, m):
    hi, lo = _split2(a)
    return (jnp.dot(hi, m, preferred_element_type=F32)
            + jnp.dot(lo, m, preferred_element_type=F32))


def _rms(x, g):
    return x * lax.rsqrt(jnp.mean(x * x, axis=-1, keepdims=True) + RMS_EPS) * g


def _sigmoid(x):
    return 1.0 / (1.0 + jnp.exp(-x))


def _head_norm(o, mavg, eps):
    mu = _seg_dot(o, mavg)
    oc = o - mu
    var = _seg_dot(oc * oc, mavg)
    return oc * lax.rsqrt(var + eps)


def _cparams(sem, vmem=VMEM_LIMIT):
    return pltpu.CompilerParams(dimension_semantics=sem, vmem_limit_bytes=vmem)


def _full(shape):
    nd = len(shape)
    return pl.BlockSpec(shape, lambda *_: (0,) * nd)


def _in_proj_kernel(x_ref, g_ref, w1_ref, w2_ref, w3_ref, o1_ref, o2_ref, o3_ref):
    h = _rms(x_ref[...], g_ref[...]).astype(BF16)
    o1_ref[...] = jnp.dot(h, w1_ref[...], preferred_element_type=F32).astype(o1_ref.dtype)
    o2_ref[...] = jnp.dot(h, w2_ref[...], preferred_element_type=F32).astype(o2_ref.dtype)
    o3_ref[...] = jnp.dot(h, w3_ref[...], preferred_element_type=F32).astype(o3_ref.dtype)


def _in_proj(x2, g, w_ret, w_rwkv, w_gate):
    T, D = x2.shape
    tm = ROW_TILE
    ws = (w_ret, w_rwkv, w_gate)
    return pl.pallas_call(
        _in_proj_kernel,
        grid=(T // tm,),
        in_specs=[pl.BlockSpec((tm, D), lambda i: (i, 0)), _full((1, D))]
        + [_full(w.shape) for w in ws],
        out_specs=[pl.BlockSpec((tm, w.shape[1]), lambda i: (i, 0)) for w in ws],
        out_shape=[jax.ShapeDtypeStruct((T, w.shape[1]), dt) for w, dt in zip(ws, (BF16, F32, BF16))],
        compiler_params=_cparams(("parallel",)),
        name="in_proj",
    )(x2, g, *ws)


def _retention_kernel(z_ref, cos_ref, sin_ref, xi_ref, zeta_ref, decay_ref, cd_ref,
                      gn_ref, mavg_ref, y_ref, state_ref):
    @pl.when(pl.program_id(1) == 0)
    def _():
        state_ref[...] = jnp.zeros_like(state_ref)

    cos = cos_ref[...]
    sin = sin_ref[...]
    lane = lax.broadcasted_iota(jnp.int32, cos.shape, 1)
    first_half = (lane % HEAD_DIM) < (HEAD_DIM // 2)

    def rot(x):
        partner = jnp.where(first_half, pltpu.roll(x, WIDTH - HEAD_DIM // 2, 1),
                            pltpu.roll(x, HEAD_DIM // 2, 1))
        return x * cos + partner * sin

    qs, ks, vs, qxs, kzs, grs = [], [], [], [], [], []
    for b in range(RET_BATCH):
        z = z_ref[b].astype(F32)
        v = z[:, 2 * WIDTH:3 * WIDTH]
        grs.append(z[:, 3 * WIDTH:4 * WIDTH])
        qr = rot(z[:, 0:WIDTH])
        kr = rot(z[:, WIDTH:2 * WIDTH]) * (HEAD_DIM ** -0.5)
        qx = qr * xi_ref[...]
        kz = kr * zeta_ref[...]
        for h in range(N_HEADS):
            sl = slice(h * HEAD_DIM, (h + 1) * HEAD_DIM)
            qs.append(qr[:, sl].astype(BF16))
            ks.append(kr[:, sl].astype(BF16))
            vs.append(v[:, sl].astype(BF16))
            qxs.append(qx[:, sl])
            kzs.append(kz[:, sl])
    chains = range(RET_BATCH * N_HEADS)
    scores = [_bdot_nt(qs[c], ks[c]) * decay_ref[c % N_HEADS] for c in chains]
    states = [state_ref[c] for c in chains]
    cross = [_bdot(qxs[c], states[c]) for c in chains]
    outs = [_bdot(scores[c], vs[c]) + cross[c] for c in chains]
    for c in chains:
        state_ref[c] = states[c] * cd_ref[c % N_HEADS] + _bdot_tn(kzs[c], vs[c])
    for b in range(RET_BATCH):
        o = jnp.concatenate(outs[b * N_HEADS:(b + 1) * N_HEADS], axis=1)
        y = _head_norm(o, mavg_ref[...], RET_GN_EPS)
        y_ref[b] = (grs[b] * _sigmoid(grs[b]) * (y * gn_ref[...])).astype(BF16)


def _retention(z_ret, gn_g, mavg):
    B, S, _ = z_ret.shape
    C = RET_CHUNK
    half = HEAD_DIM // 2
    inv_freq = ROPE_BASE ** (-jnp.arange(half, dtype=F32) * 2.0 / HEAD_DIM)
    ang = jnp.arange(S, dtype=F32)[:, None] * inv_freq[None, :]
    cos_h = jnp.concatenate([jnp.cos(ang), jnp.cos(ang)], axis=1)
    sin_h = jnp.concatenate([-jnp.sin(ang), jnp.sin(ang)], axis=1)
    cos = jnp.tile(cos_h, (1, N_HEADS))
    sin = jnp.tile(sin_h, (1, N_HEADS))
    log_gamma = jnp.log1p(-(2.0 ** (-5.0 - jnp.arange(N_HEADS, dtype=F32))))
    idx = jnp.arange(C, dtype=F32)
    diff = idx[:, None] - idx[None, :]
    causal = diff >= 0
    decay = jnp.where(causal[None], jnp.exp(log_gamma[:, None, None] * jnp.where(causal, diff, 0.0)[None]), 0.0)
    zeta = jnp.exp(log_gamma[:, None] * (C - 1.0 - idx)[None, :])
    xi = jnp.exp(log_gamma[:, None] * (idx + 1.0)[None, :])
    widen = lambda t: jnp.repeat(t.T, HEAD_DIM, axis=1)
    cd = jnp.broadcast_to(jnp.exp(log_gamma * C)[:, None, None], (N_HEADS, HEAD_DIM, HEAD_DIM))
    return pl.pallas_call(
        _retention_kernel,
        grid=(B // RET_BATCH, S // C),
        in_specs=[pl.BlockSpec((RET_BATCH, C, 4 * WIDTH), lambda b, c: (b, c, 0)),
                  pl.BlockSpec((C, WIDTH), lambda b, c: (c, 0)),
                  pl.BlockSpec((C, WIDTH), lambda b, c: (c, 0)),
                  _full((C, WIDTH)), _full((C, WIDTH)), _full((N_HEADS, C, C)),
                  _full((N_HEADS, HEAD_DIM, HEAD_DIM)), _full((1, WIDTH)), _full((WIDTH, WIDTH))],
        out_specs=pl.BlockSpec((RET_BATCH, C, WIDTH), lambda b, c: (b, c, 0)),
        out_shape=jax.ShapeDtypeStruct((B, S, WIDTH), BF16),
        scratch_shapes=[pltpu.VMEM((RET_BATCH * N_HEADS, HEAD_DIM, HEAD_DIM), F32)],
        compiler_params=_cparams(("parallel", "arbitrary")),
        name="retention",
    )(z_ret, cos, sin, widen(xi), widen(zeta), decay, cd, gn_g, mavg)


def _rwkv_prep_kernel(z_ref, mu_ref, w0_ref, wup_ref, a0_ref, aup_ref, gup_ref, kk_ref,
                      ka_ref, rk_ref, mones_ref,
                      r_out, lw_out, k_out, v_out, kk_out, kka_out, g_out, bonus_out,
                      carry_ref):
    @pl.when(pl.program_id(1) == 0)
    def _():
        carry_ref[...] = jnp.zeros_like(carry_ref)

    z = z_ref[0]
    n = z.shape[0]
    row = lax.broadcasted_iota(jnp.int32, z.shape, 0)
    prev = jnp.where(row == 0, carry_ref[0:1, :], pltpu.roll(z, 1, 0))
    carry_ref[0:1, :] = z[n - 1:n, :]
    zs = z + (prev - z) * mu_ref[...]
    r = zs[:, 0:WIDTH]
    kr = zs[:, WIDTH:2 * WIDTH]
    vr = zs[:, 2 * WIDTH:3 * WIDTH]
    o = 3 * WIDTH
    wl = zs[:, o:o + DECAY_LORA]
    al = zs[:, o + DECAY_LORA:o + DECAY_LORA + AAA_LORA]
    gl = zs[:, o + DECAY_LORA + AAA_LORA:]
    t = -(w0_ref[...] + _bdot(jnp.tanh(wl), wup_ref[...]))
    softplus = jnp.maximum(t, 0.0) + jnp.log1p(jnp.exp(-jnp.abs(t)))
    w_log = -softplus - 0.5
    a = _sigmoid(a0_ref[...] + _bdot(al, aup_ref[...]))
    g = _bdot(_sigmoid(gl), gup_ref[...])
    mones = mones_ref[...]
    kk = kr * kk_ref[...]
    norm = jnp.sqrt(_seg_dot(kk * kk, mones))
    kk = kk / jnp.maximum(norm, L2_EPS)
    k2 = kr * (1.0 + (a - 1.0) * ka_ref[...])
    r_out[0] = r.astype(BF16)
    lw_out[0] = -jnp.exp(w_log)
    k_out[0] = k2.astype(BF16)
    v_out[0] = vr.astype(BF16)
    kk_out[0] = kk.astype(BF16)
    kka_out[0] = (kk * a).astype(BF16)
    g_out[0] = g.astype(BF16)
    bonus_out[0] = (_seg_dot(r * k2 * rk_ref[...], mones) * vr).astype(BF16)


def _rwkv_prep(z_rwkv, mu, w0, w_up, a0, a_up, g_up, k_k, k_a, r_k, mones):
    B, S, _ = z_rwkv.shape
    ts = ROW_TILE
    row = lambda t: t.reshape(1, -1)
    args = (row(mu), row(w0), w_up.astype(BF16), row(a0), a_up.astype(BF16),
            g_up.astype(BF16), row(k_k), row(k_a), row(r_k), mones)
    out_spec = pl.BlockSpec((1, ts, WIDTH), lambda b, s: (b, s, 0))
    return pl.pallas_call(
        _rwkv_prep_kernel,
        grid=(B, S // ts),
        in_specs=[pl.BlockSpec((1, ts, RWKV_COLS), lambda b, s: (b, s, 0))]
        + [_full(a.shape) for a in args],
        out_specs=[out_spec] * 8,
        out_shape=[jax.ShapeDtypeStruct((B, S, WIDTH), F32 if n == 1 else BF16) for n in range(8)],
        scratch_shapes=[pltpu.VMEM((8, RWKV_COLS), F32)],
        compiler_params=_cparams(("parallel", "arbitrary")),
        name="rwkv_prep",
    )(z_rwkv, *args)


def _wkv7_kernel(r_ref, lw_ref, k_ref, v_ref, kk_ref, kka_ref, tri_ref, o_ref, state_ref):
    @pl.when(pl.program_id(1) == 0)
    def _():
        state_ref[...] = jnp.zeros_like(state_ref)

    L = WKV_CHUNK
    tri = tri_ref[...]
    ri = lax.broadcasted_iota(jnp.int32, (2 * L, L), 0)
    ci = lax.broadcasted_iota(jnp.int32, (2 * L, L), 1)
    mask = jnp.where(ri < L, ri, ri - L + 1) > ci
    eye = (lax.broadcasted_iota(jnp.int32, (L, L), 0)
           == lax.broadcasted_iota(jnp.int32, (L, L), 1)).astype(F32)

    ar, vf, bt, kt, bke, gl = [], [], [], [], [], []
    for b in range(WKV_BATCH):
        lw = lw_ref[b]
        hi = lw.astype(BF16)
        rem = lw - hi.astype(F32)
        mid = rem.astype(BF16)
        lo = (rem - mid.astype(F32)).astype(BF16)
        cum = (jnp.dot(tri, hi, preferred_element_type=F32)
               + jnp.dot(tri, mid, preferred_element_type=F32)
               + jnp.dot(tri, lo, preferred_element_type=F32))
        cum_last = cum[L - 1:L, :]
        inv_g = jnp.exp(-cum)
        to_end = jnp.exp(cum_last - cum)
        g_last = jnp.exp(cum_last)
        kk = kk_ref[b].astype(F32)
        kka = kka_ref[b].astype(F32)
        k = k_ref[b].astype(F32)
        v = v_ref[b].astype(F32)
        a_t = -kk * jnp.exp(cum - lw)
        b_t = kka * inv_g
        k_t = k * inv_g
        r_t = r_ref[b].astype(F32) * jnp.exp(cum)
        b_end = kka * to_end
        k_end = k * to_end
        for h in range(N_HEADS):
            sl = slice(h * HEAD_DIM, (h + 1) * HEAD_DIM)
            ar.append(jnp.concatenate([a_t[:, sl], r_t[:, sl]], axis=0).astype(BF16))
            vf.append(v[:, sl])
            bt.append(b_t[:, sl])
            kt.append(k_t[:, sl])
            bke.append(jnp.concatenate([b_end[:, sl], k_end[:, sl]], axis=0))
            gl.append(g_last[:, sl])
    chains = range(WKV_BATCH * N_HEADS)
    vs = [x.astype(BF16) for x in vf]
    abrb = [jnp.where(mask, _bdot_nt(ar[c], bt[c]), 0.0) for c in chains]
    akrk = [jnp.where(mask, _bdot_nt(ar[c], kt[c]), 0.0) for c in chains]
    p = [m[:L] for m in abrb]
    inv = [eye + m for m in p]
    for _ in range(int(math.log2(L)) - 1):
        p = [_bdot(m, m) for m in p]
        inv = [inv[c] + _bdot(inv[c], p[c]) for c in chains]
    states = [state_ref[c] for c in chains]
    xs = [_bdot_nt(ar[c], states[c]) for c in chains]
    kv = [_bdot(akrk[c], vs[c]) for c in chains]
    u = [_bdot(inv[c], xs[c][:L] + kv[c][:L]) for c in chains]
    y = [xs[c][L:] + kv[c][L:] + _bdot(abrb[c][L:], u[c]) for c in chains]
    for c in chains:
        uv = jnp.concatenate([u[c], vf[c]], axis=0)
        state_ref[c] = states[c] * gl[c] + _bdot_tn(uv, bke[c])
    for b in range(WKV_BATCH):
        o_ref[b] = jnp.concatenate(y[b * N_HEADS:(b + 1) * N_HEADS], axis=1)


def _wkv7(r, lw, k, v, kk, kka):
    B, S, _ = r.shape
    L = WKV_CHUNK
    nb = WKV_BATCH
    tri = jnp.tril(jnp.ones((L, L), F32)).astype(BF16)
    spec = pl.BlockSpec((nb, L, WIDTH), lambda b, c: (b, c, 0))
    return pl.pallas_call(
        _wkv7_kernel,
        grid=(B // nb, S // L),
        in_specs=[spec] * 6 + [_full((L, L))],
        out_specs=spec,
        out_shape=jax.ShapeDtypeStruct((B, S, WIDTH), F32),
        scratch_shapes=[pltpu.VMEM((nb * N_HEADS, HEAD_DIM, HEAD_DIM), F32)],
        compiler_params=_cparams(("parallel", "arbitrary")),
        name="wkv7",
    )(r, lw, k, v, kk, kka, tri)


def _merge_kernel(x_ref, yret_ref, o_ref, bonus_ref, g_ref, zg_ref, gng_ref, gnb_ref,
                  mavg_ref, wret_ref, wrwkv_ref, wo_ref, x1_ref):
    y = _head_norm(o_ref[...], mavg_ref[...], RWKV_GN_EPS)
    y_rwkv = (y * gng_ref[...] + gnb_ref[...] + bonus_ref[...].astype(F32)) * g_ref[...].astype(F32)
    br = _bdot(yret_ref[...], wret_ref[...])
    bw = _bdot(y_rwkv, wrwkv_ref[...])
    zg = zg_ref[...].astype(F32)
    d = br.shape[1]
    merged = _sigmoid(zg[:, :d]) * br + _sigmoid(zg[:, d:]) * bw
    x1_ref[...] = x_ref[...] + _bdot(merged, wo_ref[...])


def _merge(x2, y_ret, o, bonus, g, z_gate, gn_g, gn_b, mavg, w_ret_br, w_rwkv_br, w_o):
    T, D = x2.shape
    tm = ROW_TILE
    rows = lambda n: pl.BlockSpec((tm, n), lambda i: (i, 0))
    consts = (gn_g, gn_b, mavg, w_ret_br, w_rwkv_br, w_o)
    return pl.pallas_call(
        _merge_kernel,
        grid=(T // tm,),
        in_specs=[rows(D), rows(WIDTH), rows(WIDTH), rows(WIDTH), rows(WIDTH), rows(2 * D)]
        + [_full(c.shape) for c in consts],
        out_specs=rows(D),
        out_shape=jax.ShapeDtypeStruct((T, D), F32),
        compiler_params=_cparams(("parallel",)),
        name="merge",
    )(x2, y_ret, o, bonus, g, z_gate, *consts)


def _peer_scores_kernel(x_ref, g_ref, wq_ref, keys_ref, h_ref, s_ref):
    h = _rms(x_ref[...], g_ref[...])
    h_ref[...] = h.reshape(h_ref.shape)
    q = _bdot(h, wq_ref[...]).astype(BF16)
    for grp in range(PEER_GROUPS):
        s_ref[grp] = lax.dot_general(keys_ref[grp], q[:, grp * PEER_HALF:(grp + 1) * PEER_HALF],
                                     _NT, preferred_element_type=F32)


def _peer_scores(x1, g_ffn, w_pq, keys):
    T, D = x1.shape
    tm = ROW_TILE
    return pl.pallas_call(
        _peer_scores_kernel,
        grid=(T // tm,),
        in_specs=[pl.BlockSpec((tm, D), lambda i: (i, 0)), _full((1, D)), _full(w_pq.shape),
                  _full(keys.shape)],
        out_specs=[pl.BlockSpec((tm, D // LANES, LANES), lambda i: (i, 0, 0)),
                   pl.BlockSpec((PEER_GROUPS, PEER_N_KEYS, tm), lambda i: (0, 0, i))],
        out_shape=[jax.ShapeDtypeStruct((T, D // LANES, LANES), F32),
                   jax.ShapeDtypeStruct((PEER_GROUPS, PEER_N_KEYS, T), F32)],
        compiler_params=_cparams(("parallel",)),
        name="peer_scores",
    )(x1, g_ffn, w_pq, keys)


def _top_rows_steps(s, count, payload=None):
    rows = lax.broadcasted_iota(jnp.int32, s.shape, 0).astype(F32)
    vals, picks = [], []
    for _ in range(count):
        m = jnp.max(s, axis=0, keepdims=True)
        idx = jnp.min(jnp.where(s == m, rows, float(s.shape[0])), axis=0, keepdims=True)
        hit = rows == idx
        s = jnp.where(hit, -jnp.inf, s)
        vals.append(m)
        if payload is None:
            picks.append(idx)
        else:
            picks.append(jnp.max(jnp.where(hit, payload, -1), axis=0, keepdims=True))
        yield
    return jnp.concatenate(vals, axis=0), jnp.concatenate(picks, axis=0)


def _run(steps):
    try:
        while True:
            next(steps)
    except StopIteration as done:
        return done.value


def _pair_rows(a, b):
    K = PEER_TOPK
    out = []
    for i in range(K // 2):
        jn = K if i == 0 else K // 2
        out.append((a[i:i + 1, :], b[0:jn, :]))
    out.append((a[K // 2:K, :], b[0:1, :]))
    return out


def _peer_topk_kernel(s_ref, ids_ref, gates_ref):
    K = PEER_TOPK

    def head(h, carry):
        ids, gates = _run(_route_head_steps(lambda: s_ref[2 * h], lambda: s_ref[2 * h + 1]))
        off = pl.multiple_of(h * K, K)
        ids_ref[pl.ds(off, K), :] = ids
        gates_ref[pl.ds(off, K), :] = gates
        return carry

    lax.fori_loop(0, PEER_HEADS, head, 0)


def _route_head_steps(load_scores0, load_scores1):
    K = PEER_TOPK
    s0, i0 = yield from _top_rows_steps(load_scores0(), K)
    s1, i1 = yield from _top_rows_steps(load_scores1(), K)
    e0 = i0.astype(jnp.int32) * PEER_N_KEYS
    e1 = i1.astype(jnp.int32)
    cand = jnp.concatenate([x + y for x, y in _pair_rows(s0, s1)], axis=0)
    cand_id = jnp.concatenate([x + y for x, y in _pair_rows(e0, e1)], axis=0)
    best, ids = yield from _top_rows_steps(cand, K, payload=cand_id)
    e = jnp.exp(best - best[0:1, :])
    return ids, e / jnp.sum(e, axis=0, keepdims=True)


def _peer_topk(scores_t, n_tokens):
    T = n_tokens
    tk = PEER_ROUTE_TOK
    return pl.pallas_call(
        _peer_topk_kernel,
        grid=(T // tk,),
        in_specs=[pl.BlockSpec((PEER_GROUPS, PEER_N_KEYS, tk), lambda i: (0, 0, i))],
        out_specs=[pl.BlockSpec((PEER_SEL, tk), lambda i: (0, i))] * 2,
        out_shape=[jax.ShapeDtypeStruct((PEER_SEL, T), jnp.int32),
                   jax.ShapeDtypeStruct((PEER_SEL, T), F32)],
        compiler_params=_cparams(("parallel",)),
        name="peer_topk",
    )(scores_t)


def _gelu_tanh(x):
    return 0.5 * x * (1.0 + jnp.tanh(math.sqrt(2.0 / math.pi) * (x + 0.044715 * x * x * x)))


def _peer_experts_kernel(off_ref, ids_cur, ids_next, gexp_ref, h_ref, uv_hbm, s_ref, out_ref, nids_ref,
                         ngates_ref, buf, sem, rids_scr, rgates_scr):
    i = pl.program_id(0)
    n = pl.num_programs(0)
    sub = h_ref.shape[1]
    group_rows = PEER_TOK * PEER_SEL

    def issue_token(ids_ref, tok, group, t, k0=0, count=PEER_SEL):
        for k in range(k0, k0 + count):
            e = ids_ref[0, tok, k]
            pltpu.make_async_copy(uv_hbm.at[pl.ds(e, 1)],
                                  buf.at[group, pl.ds(t * PEER_SEL + k, 1)],
                                  sem.at[group]).start(priority=k % 2)

    def wait_group(group):
        pltpu.make_async_copy(uv_hbm.at[pl.ds(0, group_rows)], buf.at[group], sem.at[group]).wait()

    @pl.when(i == 0)
    def _():
        for group in range(PEER_AHEAD):
            def body(t, carry, group=group):
                issue_token(ids_cur, group * PEER_TOK + t, group, t)
                return carry
            lax.fori_loop(0, PEER_TOK, body, 0)

    ch = PEER_CHUNK
    n_ch = PEER_SEL // ch
    ccols = ch * sub
    lane = lax.broadcasted_iota(jnp.int32, (sub, ccols), 1)
    diag = (lane % sub) == lax.broadcasted_iota(jnp.int32, (sub, ccols), 0)

    def group_sum(x):
        step = 1
        while step < sub:
            partner = jnp.where((lane % (2 * step)) < step, pltpu.roll(x, ccols - step, 1),
                                pltpu.roll(x, step, 1))
            x = x + partner
            step *= 2
        return x

    tokens = PEER_GROUPS_PER_STEP * PEER_TOK
    burst = PEER_SEL // (2 * n_ch)
    issued = [0]

    def routing_steps():
        routed = []
        for j in range(s_ref.shape[0] // 2):
            routed.append((yield from _route_head_steps(lambda j=j: s_ref[2 * j],
                                                        lambda j=j: s_ref[2 * j + 1])))
        return routed

    routing = routing_steps()
    routed = []

    def routing_tick():
        if not routed:
            try:
                next(routing)
            except StopIteration as done:
                routed.extend(done.value)

    def request_burst():
        for r in range(issued[0], issued[0] + burst):
            tok, k = divmod(r, PEER_SEL)
            ahead = tok // PEER_TOK + PEER_AHEAD
            ids_ref = ids_cur if ahead < PEER_GROUPS_PER_STEP else ids_next
            group = ahead % PEER_GROUPS_PER_STEP
            issue_token(ids_ref, group * PEER_TOK + tok % PEER_TOK, group, tok % PEER_TOK, k, 1)
        issued[0] += burst
        if (issued[0] // burst) % PEER_ROUTE_EVERY == 0:
            routing_tick()

    def first_layer(tok):
        group, t = divmod(tok, PEER_TOK)
        if t == 0:
            wait_group(group)
        hb = h_ref[tok].astype(BF16)
        ps = []
        for c in range(n_ch):
            request_burst()
            rows = pl.ds(t * PEER_SEL + c * ch, ch)
            uc = buf[group, rows, 0:sub, :].reshape(ccols, LANES).astype(BF16)
            ps.append(lax.dot_general(hb, uc, _NT, preferred_element_type=F32))
        return ps

    def weights(tok, ps):
        ws = []
        for c in range(n_ch):
            act = jnp.sum(group_sum(jnp.where(diag, ps[c], 0.0)), axis=0, keepdims=True)
            w = _gelu_tanh(act) * gexp_ref[tok:tok + 1, c * ccols:(c + 1) * ccols]
            ws.append(jnp.where(diag, jnp.broadcast_to(w, (sub, ccols)), 0.0).astype(BF16))
        return ws

    def second_layer(tok, ws):
        group, t = divmod(tok, PEER_TOK)
        o = None
        for c in range(n_ch):
            request_burst()
            rows = pl.ds(t * PEER_SEL + c * ch, ch)
            vc = buf[group, rows, sub:2 * sub, :].reshape(ccols, LANES).astype(BF16)
            part = jnp.dot(ws[c], vc, preferred_element_type=F32)
            o = part if o is None else o + part
        return o

    ps = {tok: first_layer(tok) for tok in range(PEER_SKEW)}
    outs = []
    for tok in range(tokens):
        if tok + PEER_SKEW < tokens:
            ps[tok + PEER_SKEW] = first_layer(tok + PEER_SKEW)
        outs.append(second_layer(tok, weights(tok, ps.pop(tok))))
    assert issued[0] == tokens * PEER_SEL
    while not routed:
        routing_tick()
    for tok, o in enumerate(outs):
        out_ref[tok] = o
    rsteps = PEER_SEL // (len(routed) * PEER_TOPK)
    part = i % rsteps
    for j, (ids, gates) in enumerate(routed):
        row0 = pl.multiple_of((part * len(routed) + j) * PEER_TOPK, PEER_TOPK)
        rids_scr[pl.ds(row0, PEER_TOPK), :] = ids
        rgates_scr[pl.ds(row0, PEER_TOPK), :] = gates

    @pl.when(part == rsteps - 1)
    def _():
        nids_ref[...] = rids_scr[...].T
        ngates_ref[...] = rgates_scr[...].T

    @pl.when(i == n - 1)
    def _():
        for group in range(PEER_AHEAD):
            wait_group(group)


def _peer_experts(ids, gates, h_tiles, uv, scores_t, rng, next_rng, n_ranges):
    T, sub, _ = h_tiles.shape
    R = T // n_ranges
    tb = PEER_GROUPS_PER_STEP * PEER_TOK
    nb = R // tb
    rsteps = PEER_ROUTE_TOK // tb
    rgroups = PEER_GROUPS // rsteps
    assert sub == SUBLANES, "a model vector must be exactly one (sublanes, lanes) tile"
    assert T == n_ranges * nb * tb and nb % rsteps == 0 and PEER_ROUTE_TOK == rsteps * tb
    assert PEER_GROUPS % rsteps == 0 and rgroups % 2 == 0
    assert PEER_AHEAD < PEER_GROUPS_PER_STEP and PEER_SKEW < tb and PEER_SEL % PEER_CHUNK == 0
    ids3 = ids.reshape(nb, tb, PEER_SEL)
    gexp = jnp.repeat(gates, sub, axis=1)
    offsets = jnp.array([rng * nb, next_rng * (R // PEER_ROUTE_TOK)], jnp.int32)
    smem = lambda imap: pl.BlockSpec((1, tb, PEER_SEL), imap, memory_space=pltpu.SMEM)
    routed = pl.BlockSpec((PEER_ROUTE_TOK, PEER_SEL), lambda i, off: (i // rsteps, 0))
    grid_spec = pltpu.PrefetchScalarGridSpec(
        num_scalar_prefetch=1,
        grid=(nb,),
        in_specs=[smem(lambda i, off: (i, 0, 0)),
                  smem(lambda i, off: (jnp.minimum(i + 1, nb - 1), 0, 0)),
                  pl.BlockSpec((tb, PEER_SEL * sub), lambda i, off: (i, 0)),
                  pl.BlockSpec((tb, sub, LANES), lambda i, off: (off[0] + i, 0, 0)),
                  pl.BlockSpec(memory_space=pl.ANY),
                  pl.BlockSpec((rgroups, PEER_N_KEYS, PEER_ROUTE_TOK),
                               lambda i, off: (i % rsteps, 0, off[1] + i // rsteps))],
        out_specs=[pl.BlockSpec((tb, sub, LANES), lambda i, off: (i, 0, 0)), routed, routed],
        scratch_shapes=[pltpu.VMEM((PEER_GROUPS_PER_STEP, PEER_TOK * PEER_SEL, 2 * sub, LANES), F32),
                        pltpu.SemaphoreType.DMA((PEER_GROUPS_PER_STEP,)),
                        pltpu.VMEM((PEER_SEL, PEER_ROUTE_TOK), jnp.int32),
                        pltpu.VMEM((PEER_SEL, PEER_ROUTE_TOK), F32)])
    return pl.pallas_call(
        _peer_experts_kernel,
        grid_spec=grid_spec,
        out_shape=[jax.ShapeDtypeStruct((R, sub, LANES), F32),
                   jax.ShapeDtypeStruct((R, PEER_SEL), jnp.int32),
                   jax.ShapeDtypeStruct((R, PEER_SEL), F32)],
        compiler_params=_cparams(("arbitrary",)),
        name="peer_experts",
    )(offsets, ids3, ids3, gexp, h_tiles, uv, scores_t)


def _ple_final_kernel(x_ref, d_ref, p_ref, gple_ref, wg_ref, wu_ref, gfin_ref, out_ref):
    x = x_ref[...] + d_ref[...].reshape(x_ref.shape)
    gate = _sigmoid(_bdot(_rms(x, gple_ref[...]), wg_ref[...]))
    x = x + gate * _bdot(p_ref[...], wu_ref[...])
    out_ref[...] = _rms(x, gfin_ref[...])


def _ple_final(x2, delta_tiles, p2, g_ple, w_gate, w_up, g_final):
    T, D = x2.shape
    tm = ROW_TILE
    consts = (g_ple, w_gate, w_up, g_final)
    return pl.pallas_call(
        _ple_final_kernel,
        grid=(T // tm,),
        in_specs=[pl.BlockSpec((tm, D), lambda i: (i, 0)),
                  pl.BlockSpec((tm,) + delta_tiles.shape[1:], lambda i: (i, 0, 0)),
                  pl.BlockSpec((tm, p2.shape[1]), lambda i: (i, 0)),
                  _full(g_ple.shape), _full(w_gate.shape), _full(w_up.shape), _full(g_final.shape)],
        out_specs=pl.BlockSpec((tm, D), lambda i: (i, 0)),
        out_shape=jax.ShapeDtypeStruct((T, D), F32),
        compiler_params=_cparams(("parallel",)),
        name="ple_final",
    )(x2, delta_tiles, p2, *consts)


def kernel(x, p, g_mix, w_in, ret_gn_g, rwkv_mu, rwkv_w0, rwkv_w_up, rwkv_a0, rwkv_a_up, rwkv_g_up, rwkv_k_k, rwkv_k_a, rwkv_r_k, rwkv_gn_g, rwkv_gn_b, w_ret_br, w_rwkv_br, w_o, g_ffn, w_pq, peer_sub_keys, peer_u, peer_v, g_ple, w_ple_gate, w_ple_up, g_final):
    B, S, D = x.shape
    T = B * S
    assert w_in.shape[0] == 1, "single-layer block: the final RMSNorm is fused into its last step"
    assert T % ROW_TILE == 0 and S % ROW_TILE == 0 and S % RET_CHUNK == 0 and S % WKV_CHUNK == 0
    assert B % RET_BATCH == 0 and B % WKV_BATCH == 0 and D % LANES == 0
    assert T % (PEER_RANGES * PEER_ROUTE_TOK) == 0
    i = 0
    row = lambda t: t.reshape(1, -1)
    head_of = jnp.arange(WIDTH) // HEAD_DIM
    same_head = head_of[:, None] == head_of[None, :]
    mones = same_head.astype(BF16)
    mavg = (same_head.astype(F32) / HEAD_DIM).astype(BF16)
    ret_cols = 4 * WIDTH
    x2 = x.reshape(T, D)
    wi = w_in[i].astype(BF16)
    z_ret, z_rwkv, z_gate = _in_proj(
        x2, row(g_mix[i]), wi[:, :ret_cols], wi[:, ret_cols:ret_cols + RWKV_COLS],
        wi[:, ret_cols + RWKV_COLS:])
    y_ret = _retention(z_ret.reshape(B, S, ret_cols), row(ret_gn_g[i]), mavg)
    r, lw, k, v, kk, kka, g, bonus = _rwkv_prep(
        z_rwkv.reshape(B, S, RWKV_COLS), rwkv_mu[i], rwkv_w0[i], rwkv_w_up[i], rwkv_a0[i],
        rwkv_a_up[i], rwkv_g_up[i], rwkv_k_k[i], rwkv_k_a[i], rwkv_r_k[i], mones)
    o = _wkv7(r, lw, k, v, kk, kka)
    flat = lambda t: t.reshape(T, WIDTH)
    x2 = _merge(x2, flat(y_ret), flat(o), flat(bonus), flat(g), z_gate,
                row(rwkv_gn_g[i]), row(rwkv_gn_b[i]), mavg, w_ret_br[i].astype(BF16),
                w_rwkv_br[i].astype(BF16), w_o[i].astype(BF16))
    keys = peer_sub_keys[i].reshape(PEER_GROUPS, PEER_N_KEYS, PEER_HALF).astype(BF16)
    h_tiles, scores_t = _peer_scores(x2, row(g_ffn[i]), w_pq[i].astype(BF16), keys)
    n_exp = peer_u.shape[1]
    uv = jnp.concatenate([peer_u[i].reshape(n_exp, D // LANES, LANES),
                          peer_v[i].reshape(n_exp, D // LANES, LANES)], axis=1)
    R = T // PEER_RANGES
    ids_t, gates_t = _peer_topk(scores_t, R)
    ids, gates = ids_t.T, gates_t.T
    deltas = []
    for c in range(PEER_RANGES):
        delta, ids, gates = _peer_experts(ids, gates, h_tiles, uv, scores_t, c,
                                          min(c + 1, PEER_RANGES - 1), PEER_RANGES)
        deltas.append(delta)
    delta_tiles = jnp.concatenate(deltas, axis=0)
    out = _ple_final(x2, delta_tiles, p[i].reshape(T, -1), row(g_ple[i]),
                     w_ple_gate[i].astype(BF16), w_ple_up[i].astype(BF16), row(g_final))
    return out.reshape(B, S, D)
```

```python
import math

import jax
import jax.numpy as jnp
from jax import lax
from jax.experimental import pallas as pl
from jax.experimental.pallas import tpu as pltpu

F32 = jnp.float32
BF16 = jnp.bfloat16

LANES = 128
SUBLANES = 8

RMS_EPS = 1e-6
HEAD_DIM = 64
N_HEADS = 8
WIDTH = N_HEADS * HEAD_DIM
RET_CHUNK = 128
RET_BATCH = 2
RET_GN_EPS = 1e-5
ROPE_BASE = 10000.0
RWKV_GN_EPS = 64e-5
L2_EPS = 1e-12
DECAY_LORA = 64
AAA_LORA = 64
GATE_LORA = 128
RWKV_COLS = 3 * WIDTH + DECAY_LORA + AAA_LORA + GATE_LORA
WKV_CHUNK = 64
WKV_BATCH = 4

PEER_HEADS = 8
PEER_N_KEYS = 128
PEER_HALF = 128
PEER_TOPK = 16
PEER_GROUPS = 2 * PEER_HEADS
PEER_SEL = PEER_HEADS * PEER_TOPK
PEER_ROUTE_TOK = 128
PEER_ROUTE_EVERY = 2
PEER_RANGES = 8
PEER_TOK = 8
PEER_GROUPS_PER_STEP = 4
PEER_AHEAD = 2
PEER_SKEW = 2
PEER_CHUNK = 32

ROW_TILE = 512
VMEM_LIMIT = 48 * 1024 * 1024

_NT = (((1,), (1,)), ((), ()))
_TN = (((0,), (0,)), ((), ()))


def _bdot(a, b):
    return jnp.dot(a.astype(BF16), b.astype(BF16), preferred_element_type=F32)


def _bdot_nt(a, b):
    return lax.dot_general(a.astype(BF16), b.astype(BF16), _NT, preferred_element_type=F32)


def _bdot_tn(a, b):
    return lax.dot_general(a.astype(BF16), b.astype(BF16), _TN, preferred_element_type=F32)


def _split2(a):
    hi = a.astype(BF16)
    lo = (a - hi.astype(F32)).astype(BF16)
    return hi, lo


def _seg_dot(a, m):
    hi, lo = _split2(a)
    return (jnp.dot(hi, m, preferred_element_type=F32)
            + jnp.dot(lo, m, preferred_element_type=F32))


def _rms(x, g):
    return x * lax.rsqrt(jnp.mean(x * x, axis=-1, keepdims=True) + RMS_EPS) * g


def _sigmoid(x):
    return 1.0 / (1.0 + jnp.exp(-x))


def _head_norm(o, mavg, eps):
    mu = _seg_dot(o, mavg)
    oc = o - mu
    var = _seg_dot(oc * oc, mavg)
    return oc * lax.rsqrt(var + eps)


def _cparams(sem, vmem=VMEM_LIMIT):
    return pltpu.CompilerParams(dimension_semantics=sem, vmem_limit_bytes=vmem)


def _full(shape):
    nd = len(shape)
    return pl.BlockSpec(shape, lambda *_: (0,) * nd)


def _in_proj_kernel(x_ref, g_ref, w1_ref, w2_ref, w3_ref, o1_ref, o2_ref, o3_ref):
    h = _rms(x_ref[...], g_ref[...]).astype(BF16)
    o1_ref[...] = jnp.dot(h, w1_ref[...], preferred_element_type=F32).astype(o1_ref.dtype)
    o2_ref[...] = jnp.dot(h, w2_ref[...], preferred_element_type=F32).astype(o2_ref.dtype)
    o3_ref[...] = jnp.dot(h, w3_ref[...], preferred_element_type=F32).astype(o3_ref.dtype)


def _in_proj(x2, g, w_ret, w_rwkv, w_gate):
    T, D = x2.shape
    tm = ROW_TILE
    ws = (w_ret, w_rwkv, w_gate)
    return pl.pallas_call(
        _in_proj_kernel,
        grid=(T // tm,),
        in_specs=[pl.BlockSpec((tm, D), lambda i: (i, 0)), _full((1, D))]
        + [_full(w.shape) for w in ws],
        out_specs=[pl.BlockSpec((tm, w.shape[1]), lambda i: (i, 0)) for w in ws],
        out_shape=[jax.ShapeDtypeStruct((T, w.shape[1]), dt) for w, dt in zip(ws, (BF16, F32, BF16))],
        compiler_params=_cparams(("parallel",)),
        name="in_proj",
    )(x2, g, *ws)


def _retention_kernel(z_ref, cos_ref, sin_ref, xi_ref, zeta_ref, decay_ref, cd_ref,
                      gn_ref, mavg_ref, y_ref, state_ref):
    @pl.when(pl.program_id(1) == 0)
    def _():
        state_ref[...] = jnp.zeros_like(state_ref)

    cos = cos_ref[...]
    sin = sin_ref[...]
    lane = lax.broadcasted_iota(jnp.int32, cos.shape, 1)
    first_half = (lane % HEAD_DIM) < (HEAD_DIM // 2)

    def rot(x):
        partner = jnp.where(first_half, pltpu.roll(x, WIDTH - HEAD_DIM // 2, 1),
                            pltpu.roll(x, HEAD_DIM // 2, 1))
        return x * cos + partner * sin

    qs, ks, vs, qxs, kzs, grs = [], [], [], [], [], []
    for b in range(RET_BATCH):
        z = z_ref[b].astype(F32)
        v = z[:, 2 * WIDTH:3 * WIDTH]
        grs.append(z[:, 3 * WIDTH:4 * WIDTH])
        qr = rot(z[:, 0:WIDTH])
        kr = rot(z[:, WIDTH:2 * WIDTH]) * (HEAD_DIM ** -0.5)
        qx = qr * xi_ref[...]
        kz = kr * zeta_ref[...]
        for h in range(N_HEADS):
            sl = slice(h * HEAD_DIM, (h + 1) * HEAD_DIM)
            qs.append(qr[:, sl].astype(BF16))
            ks.append(kr[:, sl].astype(BF16))
            vs.append(v[:, sl].astype(BF16))
            qxs.append(qx[:, sl])
            kzs.append(kz[:, sl])
    chains = range(RET_BATCH * N_HEADS)
    scores = [_bdot_nt(qs[c], ks[c]) * decay_ref[c % N_HEADS] for c in chains]
    states = [state_ref[c] for c in chains]
    cross = [_bdot(qxs[c], states[c]) for c in chains]
    outs = [_bdot(scores[c], vs[c]) + cross[c] for c in chains]
    for c in chains:
        state_ref[c] = states[c] * cd_ref[c % N_HEADS] + _bdot_tn(kzs[c], vs[c])
    for b in range(RET_BATCH):
        o = jnp.concatenate(outs[b * N_HEADS:(b + 1) * N_HEADS], axis=1)
        y = _head_norm(o, mavg_ref[...], RET_GN_EPS)
        y_ref[b] = (grs[b] * _sigmoid(grs[b]) * (y * gn_ref[...])).astype(BF16)


def _retention(z_ret, gn_g, mavg):
    B, S, _ = z_ret.shape
    C = RET_CHUNK
    half = HEAD_DIM // 2
    inv_freq = ROPE_BASE ** (-jnp.arange(half, dtype=F32) * 2.0 / HEAD_DIM)
    ang = jnp.arange(S, dtype=F32)[:, None] * inv_freq[None, :]
    cos_h = jnp.concatenate([jnp.cos(ang), jnp.cos(ang)], axis=1)
    sin_h = jnp.concatenate([-jnp.sin(ang), jnp.sin(ang)], axis=1)
    cos = jnp.tile(cos_h, (1, N_HEADS))
    sin = jnp.tile(sin_h, (1, N_HEADS))
    log_gamma = jnp.log1p(-(2.0 ** (-5.0 - jnp.arange(N_HEADS, dtype=F32))))
    idx = jnp.arange(C, dtype=F32)
    diff = idx[:, None] - idx[None, :]
    causal = diff >= 0
    decay = jnp.where(causal[None], jnp.exp(log_gamma[:, None, None] * jnp.where(causal, diff, 0.0)[None]), 0.0)
    zeta = jnp.exp(log_gamma[:, None] * (C - 1.0 - idx)[None, :])
    xi = jnp.exp(log_gamma[:, None] * (idx + 1.0)[None, :])
    widen = lambda t: jnp.repeat(t.T, HEAD_DIM, axis=1)
    cd = jnp.broadcast_to(jnp.exp(log_gamma * C)[:, None, None], (N_HEADS, HEAD_DIM, HEAD_DIM))
    return pl.pallas_call(
        _retention_kernel,
        grid=(B // RET_BATCH, S // C),
        in_specs=[pl.BlockSpec((RET_BATCH, C, 4 * WIDTH), lambda b, c: (b, c, 0)),
                  pl.BlockSpec((C, WIDTH), lambda b, c: (c, 0)),
                  pl.BlockSpec((C, WIDTH), lambda b, c: (c, 0)),
                  _full((C, WIDTH)), _full((C, WIDTH)), _full((N_HEADS, C, C)),
                  _full((N_HEADS, HEAD_DIM, HEAD_DIM)), _full((1, WIDTH)), _full((WIDTH, WIDTH))],
        out_specs=pl.BlockSpec((RET_BATCH, C, WIDTH), lambda b, c: (b, c, 0)),
        out_shape=jax.ShapeDtypeStruct((B, S, WIDTH), BF16),
        scratch_shapes=[pltpu.VMEM((RET_BATCH * N_HEADS, HEAD_DIM, HEAD_DIM), F32)],
        compiler_params=_cparams(("parallel", "arbitrary")),
        name="retention",
    )(z_ret, cos, sin, widen(xi), widen(zeta), decay, cd, gn_g, mavg)


def _rwkv_prep_kernel(z_ref, mu_ref, w0_ref, wup_ref, a0_ref, aup_ref, gup_ref, kk_ref,
                      ka_ref, rk_ref, mones_ref,
                      r_out, lw_out, k_out, v_out, kk_out, kka_out, g_out, bonus_out,
                      carry_ref):
    @pl.when(pl.program_id(1) == 0)
    def _():
        carry_ref[...] = jnp.zeros_like(carry_ref)

    z = z_ref[0]
    n = z.shape[0]
    row = lax.broadcasted_iota(jnp.int32, z.shape, 0)
    prev = jnp.where(row == 0, carry_ref[0:1, :], pltpu.roll(z, 1, 0))
    carry_ref[0:1, :] = z[n - 1:n, :]
    zs = z + (prev - z) * mu_ref[...]
    r = zs[:, 0:WIDTH]
    kr = zs[:, WIDTH:2 * WIDTH]
    vr = zs[:, 2 * WIDTH:3 * WIDTH]
    o = 3 * WIDTH
    wl = zs[:, o:o + DECAY_LORA]
    al = zs[:, o + DECAY_LORA:o + DECAY_LORA + AAA_LORA]
    gl = zs[:, o + DECAY_LORA + AAA_LORA:]
    t = -(w0_ref[...] + _bdot(jnp.tanh(wl), wup_ref[...]))
    softplus = jnp.maximum(t, 0.0) + jnp.log1p(jnp.exp(-jnp.abs(t)))
    w_log = -softplus - 0.5
    a = _sigmoid(a0_ref[...] + _bdot(al, aup_ref[...]))
    g = _bdot(_sigmoid(gl), gup_ref[...])
    mones = mones_ref[...]
    kk = kr * kk_ref[...]
    norm = jnp.sqrt(_seg_dot(kk * kk, mones))
    kk = kk / jnp.maximum(norm, L2_EPS)
    k2 = kr * (1.0 + (a - 1.0) * ka_ref[...])
    r_out[0] = r.astype(BF16)
    lw_out[0] = -jnp.exp(w_log)
    k_out[0] = k2.astype(BF16)
    v_out[0] = vr.astype(BF16)
    kk_out[0] = kk.astype(BF16)
    kka_out[0] = (kk * a).astype(BF16)
    g_out[0] = g.astype(BF16)
    bonus_out[0] = (_seg_dot(r * k2 * rk_ref[...], mones) * vr).astype(BF16)


def _rwkv_prep(z_rwkv, mu, w0, w_up, a0, a_up, g_up, k_k, k_a, r_k, mones):
    B, S, _ = z_rwkv.shape
    ts = ROW_TILE
    row = lambda t: t.reshape(1, -1)
    args = (row(mu), row(w0), w_up.astype(BF16), row(a0), a_up.astype(BF16),
            g_up.astype(BF16), row(k_k), row(k_a), row(r_k), mones)
    out_spec = pl.BlockSpec((1, ts, WIDTH), lambda b, s: (b, s, 0))
    return pl.pallas_call(
        _rwkv_prep_kernel,
        grid=(B, S // ts),
        in_specs=[pl.BlockSpec((1, ts, RWKV_COLS), lambda b, s: (b, s, 0))]
        + [_full(a.shape) for a in args],
        out_specs=[out_spec] * 8,
        out_shape=[jax.ShapeDtypeStruct((B, S, WIDTH), F32 if n == 1 else BF16) for n in range(8)],
        scratch_shapes=[pltpu.VMEM((8, RWKV_COLS), F32)],
        compiler_params=_cparams(("parallel", "arbitrary")),
        name="rwkv_prep",
    )(z_rwkv, *args)


def _wkv7_kernel(r_ref, lw_ref, k_ref, v_ref, kk_ref, kka_ref, tri_ref, o_ref, state_ref):
    @pl.when(pl.program_id(1) == 0)
    def _():
        state_ref[...] = jnp.zeros_like(state_ref)

    L = WKV_CHUNK
    tri = tri_ref[...]
    ri = lax.broadcasted_iota(jnp.int32, (2 * L, L), 0)
    ci = lax.broadcasted_iota(jnp.int32, (2 * L, L), 1)
    mask = jnp.where(ri < L, ri, ri - L + 1) > ci
    eye = (lax.broadcasted_iota(jnp.int32, (L, L), 0)
           == lax.broadcasted_iota(jnp.int32, (L, L), 1)).astype(F32)

    ar, vf, bt, kt, bke, gl = [], [], [], [], [], []
    for b in range(WKV_BATCH):
        lw = lw_ref[b]
        hi = lw.astype(BF16)
        rem = lw - hi.astype(F32)
        mid = rem.astype(BF16)
        lo = (rem - mid.astype(F32)).astype(BF16)
        cum = (jnp.dot(tri, hi, preferred_element_type=F32)
               + jnp.dot(tri, mid, preferred_element_type=F32)
               + jnp.dot(tri, lo, preferred_element_type=F32))
        cum_last = cum[L - 1:L, :]
        inv_g = jnp.exp(-cum)
        to_end = jnp.exp(cum_last - cum)
        g_last = jnp.exp(cum_last)
        kk = kk_ref[b].astype(F32)
        kka = kka_ref[b].astype(F32)
        k = k_ref[b].astype(F32)
        v = v_ref[b].astype(F32)
        a_t = -kk * jnp.exp(cum - lw)
        b_t = kka * inv_g
        k_t = k * inv_g
        r_t = r_ref[b].astype(F32) * jnp.exp(cum)
        b_end = kka * to_end
        k_end = k * to_end
        for h in range(N_HEADS):
            sl = slice(h * HEAD_DIM, (h + 1) * HEAD_DIM)
            ar.append(jnp.concatenate([a_t[:, sl], r_t[:, sl]], axis=0).astype(BF16))
            vf.append(v[:, sl])
            bt.append(b_t[:, sl])
            kt.append(k_t[:, sl])
            bke.append(jnp.concatenate([b_end[:, sl], k_end[:, sl]], axis=0))
            gl.append(g_last[:, sl])
    chains = range(WKV_BATCH * N_HEADS)
    vs = [x.astype(BF16) for x in vf]
    abrb = [jnp.where(mask, _bdot_nt(ar[c], bt[c]), 0.0) for c in chains]
    akrk = [jnp.where(mask, _bdot_nt(ar[c], kt[c]), 0.0) for c in chains]
    p = [m[:L] for m in abrb]
    inv = [eye + m for m in p]
    for _ in range(int(math.log2(L)) - 1):
        p = [_bdot(m, m) for m in p]
        inv = [inv[c] + _bdot(inv[c], p[c]) for c in chains]
    states = [state_ref[c] for c in chains]
    xs = [_bdot_nt(ar[c], states[c]) for c in chains]
    kv = [_bdot(akrk[c], vs[c]) for c in chains]
    u = [_bdot(inv[c], xs[c][:L] + kv[c][:L]) for c in chains]
    y = [xs[c][L:] + kv[c][L:] + _bdot(abrb[c][L:], u[c]) for c in chains]
    for c in chains:
        uv = jnp.concatenate([u[c], vf[c]], axis=0)
        state_ref[c] = states[c] * gl[c] + _bdot_tn(uv, bke[c])
    for b in range(WKV_BATCH):
        o_ref[b] = jnp.concatenate(y[b * N_HEADS:(b + 1) * N_HEADS], axis=1)


def _wkv7(r, lw, k, v, kk, kka):
    B, S, _ = r.shape
    L = WKV_CHUNK
    nb = WKV_BATCH
    tri = jnp.tril(jnp.ones((L, L), F32)).astype(BF16)
    spec = pl.BlockSpec((nb, L, WIDTH), lambda b, c: (b, c, 0))
    return pl.pallas_call(
        _wkv7_kernel,
        grid=(B // nb, S // L),
        in_specs=[spec] * 6 + [_full((L, L))],
        out_specs=spec,
        out_shape=jax.ShapeDtypeStruct((B, S, WIDTH), F32),
        scratch_shapes=[pltpu.VMEM((nb * N_HEADS, HEAD_DIM, HEAD_DIM), F32)],
        compiler_params=_cparams(("parallel", "arbitrary")),
        name="wkv7",
    )(r, lw, k, v, kk, kka, tri)


def _merge_kernel(x_ref, yret_ref, o_ref, bonus_ref, g_ref, zg_ref, gng_ref, gnb_ref,
                  mavg_ref, wret_ref, wrwkv_ref, wo_ref, x1_ref):
    y = _head_norm(o_ref[...], mavg_ref[...], RWKV_GN_EPS)
    y_rwkv = (y * gng_ref[...] + gnb_ref[...] + bonus_ref[...].astype(F32)) * g_ref[...].astype(F32)
    br = _bdot(yret_ref[...], wret_ref[...])
    bw = _bdot(y_rwkv, wrwkv_ref[...])
    zg = zg_ref[...].astype(F32)
    d = br.shape[1]
    merged = _sigmoid(zg[:, :d]) * br + _sigmoid(zg[:, d:]) * bw
    x1_ref[...] = x_ref[...] + _bdot(merged, wo_ref[...])


def _merge(x2, y_ret, o, bonus, g, z_gate, gn_g, gn_b, mavg, w_ret_br, w_rwkv_br, w_o):
    T, D = x2.shape
    tm = ROW_TILE
    rows = lambda n: pl.BlockSpec((tm, n), lambda i: (i, 0))
    consts = (gn_g, gn_b, mavg, w_ret_br, w_rwkv_br, w_o)
    return pl.pallas_call(
        _merge_kernel,
        grid=(T // tm,),
        in_specs=[rows(D), rows(WIDTH), rows(WIDTH), rows(WIDTH), rows(WIDTH), rows(2 * D)]
        + [_full(c.shape) for c in consts],
        out_specs=rows(D),
        out_shape=jax.ShapeDtypeStruct((T, D), F32),
        compiler_params=_cparams(("parallel",)),
        name="merge",
    )(x2, y_ret, o, bonus, g, z_gate, *consts)


def _peer_scores_kernel(x_ref, g_ref, wq_ref, keys_ref, h_ref, s_ref):
    h = _rms(x_ref[...], g_ref[...])
    h_ref[...] = h.reshape(h_ref.shape)
    q = _bdot(h, wq_ref[...]).astype(BF16)
    for grp in range(PEER_GROUPS):
        s_ref[grp] = lax.dot_general(keys_ref[grp], q[:, grp * PEER_HALF:(grp + 1) * PEER_HALF],
                                     _NT, preferred_element_type=F32)


def _peer_scores(x1, g_ffn, w_pq, keys):
    T, D = x1.shape
    tm = ROW_TILE
    return pl.pallas_call(
        _peer_scores_kernel,
        grid=(T // tm,),
        in_specs=[pl.BlockSpec((tm, D), lambda i: (i, 0)), _full((1, D)), _full(w_pq.shape),
                  _full(keys.shape)],
        out_specs=[pl.BlockSpec((tm, D // LANES, LANES), lambda i: (i, 0, 0)),
                   pl.BlockSpec((PEER_GROUPS, PEER_N_KEYS, tm), lambda i: (0, 0, i))],
        out_shape=[jax.ShapeDtypeStruct((T, D // LANES, LANES), F32),
                   jax.ShapeDtypeStruct((PEER_GROUPS, PEER_N_KEYS, T), F32)],
        compiler_params=_cparams(("parallel",)),
        name="peer_scores",
    )(x1, g_ffn, w_pq, keys)


def _top_rows_steps(s, count, payload=None):
    rows = lax.broadcasted_iota(jnp.int32, s.shape, 0).astype(F32)
    vals, picks = [], []
    for _ in range(count):
        m = jnp.max(s, axis=0, keepdims=True)
        idx = jnp.min(jnp.where(s == m, rows, float(s.shape[0])), axis=0, keepdims=True)
        hit = rows == idx
        s = jnp.where(hit, -jnp.inf, s)
        vals.append(m)
        if payload is None:
            picks.append(idx)
        else:
            picks.append(jnp.max(jnp.where(hit, payload, -1), axis=0, keepdims=True))
        yield
    return jnp.concatenate(vals, axis=0), jnp.concatenate(picks, axis=0)


def _run(steps):
    try:
        while True:
            next(steps)
    except StopIteration as done:
        return done.value


def _pair_rows(a, b):
    K = PEER_TOPK
    out = []
    for i in range(K // 2):
        jn = K if i == 0 else K // 2
        out.append((a[i:i + 1, :], b[0:jn, :]))
    out.append((a[K // 2:K, :], b[0:1, :]))
    return out


def _peer_topk_kernel(s_ref, ids_ref, gates_ref):
    K = PEER_TOPK

    def head(h, carry):
        ids, gates = _run(_route_head_steps(lambda: s_ref[2 * h], lambda: s_ref[2 * h + 1]))
        off = pl.multiple_of(h * K, K)
        ids_ref[pl.ds(off, K), :] = ids
        gates_ref[pl.ds(off, K), :] = gates
        return carry

    lax.fori_loop(0, PEER_HEADS, head, 0)


def _route_head_steps(load_scores0, load_scores1):
    K = PEER_TOPK
    s0, i0 = yield from _top_rows_steps(load_scores0(), K)
    s1, i1 = yield from _top_rows_steps(load_scores1(), K)
    e0 = i0.astype(jnp.int32) * PEER_N_KEYS
    e1 = i1.astype(jnp.int32)
    cand = jnp.concatenate([x + y for x, y in _pair_rows(s0, s1)], axis=0)
    cand_id = jnp.concatenate([x + y for x, y in _pair_rows(e0, e1)], axis=0)
    best, ids = yield from _top_rows_steps(cand, K, payload=cand_id)
    e = jnp.exp(best - best[0:1, :])
    return ids, e / jnp.sum(e, axis=0, keepdims=True)


def _peer_topk(scores_t, n_tokens):
    T = n_tokens
    tk = PEER_ROUTE_TOK
    return pl.pallas_call(
        _peer_topk_kernel,
        grid=(T // tk,),
        in_specs=[pl.BlockSpec((PEER_GROUPS, PEER_N_KEYS, tk), lambda i: (0, 0, i))],
        out_specs=[pl.BlockSpec((PEER_SEL, tk), lambda i: (0, i))] * 2,
        out_shape=[jax.ShapeDtypeStruct((PEER_SEL, T), jnp.int32),
                   jax.ShapeDtypeStruct((PEER_SEL, T), F32)],
        compiler_params=_cparams(("parallel",)),
        name="peer_topk",
    )(scores_t)


def _gelu_tanh(x):
    return 0.5 * x * (1.0 + jnp.tanh(math.sqrt(2.0 / math.pi) * (x + 0.044715 * x * x * x)))


def _peer_experts_kernel(off_ref, ids_cur, ids_next, gexp_ref, h_ref, uv_hbm, s_ref, out_ref, nids_ref,
                         ngates_ref, buf, sem, rids_scr, rgates_scr):
    i = pl.program_id(0)
    n = pl.num_programs(0)
    sub = h_ref.shape[1]
    group_rows = PEER_TOK * PEER_SEL

    def issue_token(ids_ref, tok, group, t, k0=0, count=PEER_SEL):
        for k in range(k0, k0 + count):
            e = ids_ref[0, tok, k]
            pltpu.make_async_copy(uv_hbm.at[pl.ds(e, 1)],
                                  buf.at[group, pl.ds(t * PEER_SEL + k, 1)],
                                  sem.at[group]).start(priority=k % 2)

    def wait_group(group):
        pltpu.make_async_copy(uv_hbm.at[pl.ds(0, group_rows)], buf.at[group], sem.at[group]).wait()

    @pl.when(i == 0)
    def _():
        for group in range(PEER_AHEAD):
            def body(t, carry, group=group):
                issue_token(ids_cur, group * PEER_TOK + t, group, t)
                return carry
            lax.fori_loop(0, PEER_TOK, body, 0)

    ch = PEER_CHUNK
    n_ch = PEER_SEL // ch
    ccols = ch * sub
    lane = lax.broadcasted_iota(jnp.int32, (sub, ccols), 1)
    diag = (lane % sub) == lax.broadcasted_iota(jnp.int32, (sub, ccols), 0)

    def group_sum(x):
        step = 1
        while step < sub:
            partner = jnp.where((lane % (2 * step)) < step, pltpu.roll(x, ccols - step, 1),
                                pltpu.roll(x, step, 1))
            x = x + partner
            step *= 2
        return x

    tokens = PEER_GROUPS_PER_STEP * PEER_TOK
    burst = PEER_SEL // (2 * n_ch)
    issued = [0]

    def routing_steps():
        routed = []
        for j in range(s_ref.shape[0] // 2):
            routed.append((yield from _route_head_steps(lambda j=j: s_ref[2 * j],
                                                        lambda j=j: s_ref[2 * j + 1])))
        return routed

    routing = routing_steps()
    routed = []

    def routing_tick():
        if not routed:
            try:
                next(routing)
            except StopIteration as done:
                routed.extend(done.value)

    def request_burst():
        for r in range(issued[0], issued[0] + burst):
            tok, k = divmod(r, PEER_SEL)
            ahead = tok // PEER_TOK + PEER_AHEAD
            ids_ref = ids_cur if ahead < PEER_GROUPS_PER_STEP else ids_next
            group = ahead % PEER_GROUPS_PER_STEP
            issue_token(ids_ref, group * PEER_TOK + tok % PEER_TOK, group, tok % PEER_TOK, k, 1)
        issued[0] += burst
        if (issued[0] // burst) % PEER_ROUTE_EVERY == 0:
            routing_tick()

    def first_layer(tok):
        group, t = divmod(tok, PEER_TOK)
        if t == 0:
            wait_group(group)
        hb = h_ref[tok].astype(BF16)
        ps = []
        for c in range(n_ch):
            request_burst()
            rows = pl.ds(t * PEER_SEL + c * ch, ch)
            uc = buf[group, rows, 0:sub, :].reshape(ccols, LANES).astype(BF16)
            ps.append(lax.dot_general(hb, uc, _NT, preferred_element_type=F32))
        return ps

    def weights(tok, ps):
        ws = []
        for c in range(n_ch):
            act = jnp.sum(group_sum(jnp.where(diag, ps[c], 0.0)), axis=0, keepdims=True)
            w = _gelu_tanh(act) * gexp_ref[tok:tok + 1, c * ccols:(c + 1) * ccols]
            ws.append(jnp.where(diag, jnp.broadcast_to(w, (sub, ccols)), 0.0).astype(BF16))
        return ws

    def second_layer(tok, ws):
        group, t = divmod(tok, PEER_TOK)
        o = None
        for c in range(n_ch):
            request_burst()
            rows = pl.ds(t * PEER_SEL + c * ch, ch)
            vc = buf[group, rows, sub:2 * sub, :].reshape(ccols, LANES).astype(BF16)
            part = jnp.dot(ws[c], vc, preferred_element_type=F32)
            o = part if o is None else o + part
        return o

    ps = {tok: first_layer(tok) for tok in range(PEER_SKEW)}
    outs = []
    for tok in range(tokens):
        if tok + PEER_SKEW < tokens:
            ps[tok + PEER_SKEW] = first_layer(tok + PEER_SKEW)
        outs.append(second_layer(tok, weights(tok, ps.pop(tok))))
    assert issued[0] == tokens * PEER_SEL
    while not routed:
        routing_tick()
    for tok, o in enumerate(outs):
        out_ref[tok] = o
    rsteps = PEER_SEL // (len(routed) * PEER_TOPK)
    part = i % rsteps
    for j, (ids, gates) in enumerate(routed):
        row0 = pl.multiple_of((part * len(routed) + j) * PEER_TOPK, PEER_TOPK)
        rids_scr[pl.ds(row0, PEER_TOPK), :] = ids
        rgates_scr[pl.ds(row0, PEER_TOPK), :] = gates

    @pl.when(part == rsteps - 1)
    def _():
        nids_ref[...] = rids_scr[...].T
        ngates_ref[...] = rgates_scr[...].T

    @pl.when(i == n - 1)
    def _():
        for group in range(PEER_AHEAD):
            wait_group(group)


def _peer_experts(ids, gates, h_tiles, uv, scores_t, rng, next_rng, n_ranges):
    T, sub, _ = h_tiles.shape
    R = T // n_ranges
    tb = PEER_GROUPS_PER_STEP * PEER_TOK
    nb = R // tb
    rsteps = PEER_ROUTE_TOK // tb
    rgroups = PEER_GROUPS // rsteps
    assert sub == SUBLANES, "a model vector must be exactly one (sublanes, lanes) tile"
    assert T == n_ranges * nb * tb and nb % rsteps == 0 and PEER_ROUTE_TOK == rsteps * tb
    assert PEER_GROUPS % rsteps == 0 and rgroups % 2 == 0
    assert PEER_AHEAD < PEER_GROUPS_PER_STEP and PEER_SKEW < tb and PEER_SEL % PEER_CHUNK == 0
    ids3 = ids.reshape(nb, tb, PEER_SEL)
    gexp = jnp.repeat(gates, sub, axis=1)
    offsets = jnp.array([rng * nb, next_rng * (R // PEER_ROUTE_TOK)], jnp.int32)
    smem = lambda imap: pl.BlockSpec((1, tb, PEER_SEL), imap, memory_space=pltpu.SMEM)
    routed = pl.BlockSpec((PEER_ROUTE_TOK, PEER_SEL), lambda i, off: (i // rsteps, 0))
    grid_spec = pltpu.PrefetchScalarGridSpec(
        num_scalar_prefetch=1,
        grid=(nb,),
        in_specs=[smem(lambda i, off: (i, 0, 0)),
                  smem(lambda i, off: (jnp.minimum(i + 1, nb - 1), 0, 0)),
                  pl.BlockSpec((tb, PEER_SEL * sub), lambda i, off: (i, 0)),
                  pl.BlockSpec((tb, sub, LANES), lambda i, off: (off[0] + i, 0, 0)),
                  pl.BlockSpec(memory_space=pl.ANY),
                  pl.BlockSpec((rgroups, PEER_N_KEYS, PEER_ROUTE_TOK),
                               lambda i, off: (i % rsteps, 0, off[1] + i // rsteps))],
        out_specs=[pl.BlockSpec((tb, sub, LANES), lambda i, off: (i, 0, 0)), routed, routed],
        scratch_shapes=[pltpu.VMEM((PEER_GROUPS_PER_STEP, PEER_TOK * PEER_SEL, 2 * sub, LANES), F32),
                        pltpu.SemaphoreType.DMA((PEER_GROUPS_PER_STEP,)),
                        pltpu.VMEM((PEER_SEL, PEER_ROUTE_TOK), jnp.int32),
                        pltpu.VMEM((PEER_SEL, PEER_ROUTE_TOK), F32)])
    return pl.pallas_call(
        _peer_experts_kernel,
        grid_spec=grid_spec,
        out_shape=[jax.ShapeDtypeStruct((R, sub, LANES), F32),
                   jax.ShapeDtypeStruct((R, PEER_SEL), jnp.int32),
                   jax.ShapeDtypeStruct((R, PEER_SEL), F32)],
        compiler_params=_cparams(("arbitrary",)),
        name="peer_experts",
    )(offsets, ids3, ids3, gexp, h_tiles, uv, scores_t)


def _ple_final_kernel(x_ref, d_ref, p_ref, gple_ref, wg_ref, wu_ref, gfin_ref, out_ref):
    x = x_ref[...] + d_ref[...].reshape(x_ref.shape)
    gate = _sigmoid(_bdot(_rms(x, gple_ref[...]), wg_ref[...]))
    x = x + gate * _bdot(p_ref[...], wu_ref[...])
    out_ref[...] = _rms(x, gfin_ref[...])


def _ple_final(x2, delta_tiles, p2, g_ple, w_gate, w_up, g_final):
    T, D = x2.shape
    tm = ROW_TILE
    consts = (g_ple, w_gate, w_up, g_final)
    return pl.pallas_call(
        _ple_final_kernel,
        grid=(T // tm,),
        in_specs=[pl.BlockSpec((tm, D), lambda i: (i, 0)),
                  pl.BlockSpec((tm,) + delta_tiles.shape[1:], lambda i: (i, 0, 0)),
                  pl.BlockSpec((tm, p2.shape[1]), lambda i: (i, 0)),
                  _full(g_ple.shape), _full(w_gate.shape), _full(w_up.shape), _full(g_final.shape)],
        out_specs=pl.BlockSpec((tm, D), lambda i: (i, 0)),
        out_shape=jax.ShapeDtypeStruct((T, D), F32),
        compiler_params=_cparams(("parallel",)),
        name="ple_final",
    )(x2, delta_tiles, p2, *consts)


def kernel(x, p, g_mix, w_in, ret_gn_g, rwkv_mu, rwkv_w0, rwkv_w_up, rwkv_a0, rwkv_a_up, rwkv_g_up, rwkv_k_k, rwkv_k_a, rwkv_r_k, rwkv_gn_g, rwkv_gn_b, w_ret_br, w_rwkv_br, w_o, g_ffn, w_pq, peer_sub_keys, peer_u, peer_v, g_ple, w_ple_gate, w_ple_up, g_final):
    B, S, D = x.shape
    T = B * S
    assert w_in.shape[0] == 1, "single-layer block: the final RMSNorm is fused into its last step"
    assert T % ROW_TILE == 0 and S % ROW_TILE == 0 and S % RET_CHUNK == 0 and S % WKV_CHUNK == 0
    assert B % RET_BATCH == 0 and B % WKV_BATCH == 0 and D % LANES == 0
    assert T % (PEER_RANGES * PEER_ROUTE_TOK) == 0
    i = 0
    row = lambda t: t.reshape(1, -1)
    head_of = jnp.arange(WIDTH) // HEAD_DIM
    same_head = head_of[:, None] == head_of[None, :]
    mones = same_head.astype(BF16)
    mavg = (same_head.astype(F32) / HEAD_DIM).astype(BF16)
    ret_cols = 4 * WIDTH
    x2 = x.reshape(T, D)
    wi = w_in[i].astype(BF16)
    z_ret, z_rwkv, z_gate = _in_proj(
        x2, row(g_mix[i]), wi[:, :ret_cols], wi[:, ret_cols:ret_cols + RWKV_COLS],
        wi[:, ret_cols + RWKV_COLS:])
    y_ret = _retention(z_ret.reshape(B, S, ret_cols), row(ret_gn_g[i]), mavg)
    r, lw, k, v, kk, kka, g, bonus = _rwkv_prep(
        z_rwkv.reshape(B, S, RWKV_COLS), rwkv_mu[i], rwkv_w0[i], rwkv_w_up[i], rwkv_a0[i],
        rwkv_a_up[i], rwkv_g_up[i], rwkv_k_k[i], rwkv_k_a[i], rwkv_r_k[i], mones)
    o = _wkv7(r, lw, k, v, kk, kka)
    flat = lambda t: t.reshape(T, WIDTH)
    x2 = _merge(x2, flat(y_ret), flat(o), flat(bonus), flat(g), z_gate,
                row(rwkv_gn_g[i]), row(rwkv_gn_b[i]), mavg, w_ret_br[i].astype(BF16),
                w_rwkv_br[i].astype(BF16), w_o[i].astype(BF16))
    keys = peer_sub_keys[i].reshape(PEER_GROUPS, PEER_N_KEYS, PEER_HALF).astype(BF16)
    h_tiles, scores_t = _peer_scores(x2, row(g_ffn[i]), w_pq[i].astype(BF16), keys)
    n_exp = peer_u.shape[1]
    uv = jnp.concatenate([peer_u[i].reshape(n_exp, D // LANES, LANES),
                          peer_v[i].reshape(n_exp, D // LANES, LANES)], axis=1)
    R = T // PEER_RANGES
    ids_t, gates_t = _peer_topk(scores_t, R)
    ids, gates = ids_t.T, gates_t.T
    deltas = []
    for c in range(PEER_RANGES):
        delta, ids, gates = _peer_experts(ids, gates, h_tiles, uv, scores_t, c,
                                          min(c + 1, PEER_RANGES - 1), PEER_RANGES)
        deltas.append(delta)
    delta_tiles = jnp.concatenate(deltas, axis=0)
    out = _ple_final(x2, delta_tiles, p[i].reshape(T, -1), row(g_ple[i]),
                     w_ple_gate[i].astype(BF16), w_ple_up[i].astype(BF16), row(g_final))
    return out.reshape(B, S, D)
```

```python
import math

import jax
import jax.numpy as jnp
from jax import lax
from jax.experimental import pallas as pl
from jax.experimental.pallas import tpu as pltpu

F32 = jnp.float32
BF16 = jnp.bfloat16

LANES = 128
SUBLANES = 8

RMS_EPS = 1e-6
HEAD_DIM = 64
N_HEADS = 8
WIDTH = N_HEADS * HEAD_DIM
RET_CHUNK = 128
RET_BATCH = 4
RET_GN_EPS = 1e-5
ROPE_BASE = 10000.0
RWKV_GN_EPS = 64e-5
L2_EPS = 1e-12
DECAY_LORA = 64
AAA_LORA = 64
GATE_LORA = 128
RWKV_COLS = 3 * WIDTH + DECAY_LORA + AAA_LORA + GATE_LORA
WKV_CHUNK = 64
WKV_BATCH = 8

PEER_HEADS = 8
PEER_N_KEYS = 128
PEER_HALF = 128
PEER_TOPK = 16
PEER_GROUPS = 2 * PEER_HEADS
PEER_SEL = PEER_HEADS * PEER_TOPK
PEER_ROUTE_TOK = 128
PEER_ROUTE_EVERY = 2
PEER_RANGES = 8
PEER_TOK = 8
PEER_GROUPS_PER_STEP = 4
PEER_AHEAD = 2
PEER_SKEW = 2
PEER_CHUNK = 32

ROW_TILE = 512
VMEM_LIMIT = 48 * 1024 * 1024

_NT = (((1,), (1,)), ((), ()))
_TN = (((0,), (0,)), ((), ()))


def _bdot(a, b):
    return jnp.dot(a.astype(BF16), b.astype(BF16), preferred_element_type=F32)


def _bdot_nt(a, b):
    return lax.dot_general(a.astype(BF16), b.astype(BF16), _NT, preferred_element_type=F32)


def _bdot_tn(a, b):
    return lax.dot_general(a.astype(BF16), b.astype(BF16), _TN, preferred_element_type=F32)


def _split2(a):
    hi = a.astype(BF16)
    lo = (a - hi.astype(F32)).astype(BF16)
    return hi, lo


def _seg_dot(a, m):
    hi, lo = _split2(a)
    return (jnp.dot(hi, m, preferred_element_type=F32)
            + jnp.dot(lo, m, preferred_element_type=F32))


def _rms(x, g):
    return x * lax.rsqrt(jnp.mean(x * x, axis=-1, keepdims=True) + RMS_EPS) * g


def _sigmoid(x):
    return 1.0 / (1.0 + jnp.exp(-x))


def _head_norm(o, mavg, eps):
    mu = _seg_dot(o, mavg)
    oc = o - mu
    var = _seg_dot(oc * oc, mavg)
    return oc * lax.rsqrt(var + eps)


def _cparams(sem, vmem=VMEM_LIMIT):
    return pltpu.CompilerParams(dimension_semantics=sem, vmem_limit_bytes=vmem)


def _full(shape):
    nd = len(shape)
    return pl.BlockSpec(shape, lambda *_: (0,) * nd)


def _in_proj_kernel(x_ref, g_ref, w1_ref, w2_ref, w3_ref, o1_ref, o2_ref, o3_ref):
    h = _rms(x_ref[...], g_ref[...]).astype(BF16)
    o1_ref[...] = jnp.dot(h, w1_ref[...], preferred_element_type=F32).astype(o1_ref.dtype)
    o2_ref[...] = jnp.dot(h, w2_ref[...], preferred_element_type=F32).astype(o2_ref.dtype)
    o3_ref[...] = jnp.dot(h, w3_ref[...], preferred_element_type=F32).astype(o3_ref.dtype)


def _in_proj(x2, g, w_ret, w_rwkv, w_gate):
    T, D = x2.shape
    tm = ROW_TILE
    ws = (w_ret, w_rwkv, w_gate)
    return pl.pallas_call(
        _in_proj_kernel,
        grid=(T // tm,),
        in_specs=[pl.BlockSpec((tm, D), lambda i: (i, 0)), _full((1, D))]
        + [_full(w.shape) for w in ws],
        out_specs=[pl.BlockSpec((tm, w.shape[1]), lambda i: (i, 0)) for w in ws],
        out_shape=[jax.ShapeDtypeStruct((T, w.shape[1]), dt) for w, dt in zip(ws, (BF16, F32, BF16))],
        compiler_params=_cparams(("parallel",)),
        name="in_proj",
    )(x2, g, *ws)


def _retention_kernel(z_ref, cos_ref, sin_ref, xi_ref, zeta_ref, decay_ref, cd_ref,
                      gn_ref, mavg_ref, y_ref, state_ref):
    @pl.when(pl.program_id(1) == 0)
    def _():
        state_ref[...] = jnp.zeros_like(state_ref)

    cos = cos_ref[...]
    sin = sin_ref[...]
    lane = lax.broadcasted_iota(jnp.int32, cos.shape, 1)
    first_half = (lane % HEAD_DIM) < (HEAD_DIM // 2)

    def rot(x):
        partner = jnp.where(first_half, pltpu.roll(x, WIDTH - HEAD_DIM // 2, 1),
                            pltpu.roll(x, HEAD_DIM // 2, 1))
        return x * cos + partner * sin

    qs, ks, vs, qxs, kzs, grs = [], [], [], [], [], []
    for b in range(RET_BATCH):
        z = z_ref[b].astype(F32)
        v = z[:, 2 * WIDTH:3 * WIDTH]
        grs.append(z[:, 3 * WIDTH:4 * WIDTH])
        qr = rot(z[:, 0:WIDTH])
        kr = rot(z[:, WIDTH:2 * WIDTH]) * (HEAD_DIM ** -0.5)
        qx = qr * xi_ref[...]
        kz = kr * zeta_ref[...]
        for h in range(N_HEADS):
            sl = slice(h * HEAD_DIM, (h + 1) * HEAD_DIM)
            qs.append(qr[:, sl].astype(BF16))
            ks.append(kr[:, sl].astype(BF16))
            vs.append(v[:, sl].astype(BF16))
            qxs.append(qx[:, sl])
            kzs.append(kz[:, sl])
    chains = range(RET_BATCH * N_HEADS)
    scores = [_bdot_nt(qs[c], ks[c]) * decay_ref[c % N_HEADS] for c in chains]
    states = [state_ref[c] for c in chains]
    cross = [_bdot(qxs[c], states[c]) for c in chains]
    outs = [_bdot(scores[c], vs[c]) + cross[c] for c in chains]
    for c in chains:
        state_ref[c] = states[c] * cd_ref[c % N_HEADS] + _bdot_tn(kzs[c], vs[c])
    for b in range(RET_BATCH):
        o = jnp.concatenate(outs[b * N_HEADS:(b + 1) * N_HEADS], axis=1)
        y = _head_norm(o, mavg_ref[...], RET_GN_EPS)
        y_ref[b] = (grs[b] * _sigmoid(grs[b]) * (y * gn_ref[...])).astype(BF16)


def _retention(z_ret, gn_g, mavg):
    B, S, _ = z_ret.shape
    C = RET_CHUNK
    half = HEAD_DIM // 2
    inv_freq = ROPE_BASE ** (-jnp.arange(half, dtype=F32) * 2.0 / HEAD_DIM)
    ang = jnp.arange(S, dtype=F32)[:, None] * inv_freq[None, :]
    cos_h = jnp.concatenate([jnp.cos(ang), jnp.cos(ang)], axis=1)
    sin_h = jnp.concatenate([-jnp.sin(ang), jnp.sin(ang)], axis=1)
    cos = jnp.tile(cos_h, (1, N_HEADS))
    sin = jnp.tile(sin_h, (1, N_HEADS))
    log_gamma = jnp.log1p(-(2.0 ** (-5.0 - jnp.arange(N_HEADS, dtype=F32))))
    idx = jnp.arange(C, dtype=F32)
    diff = idx[:, None] - idx[None, :]
    causal = diff >= 0
    decay = jnp.where(causal[None], jnp.exp(log_gamma[:, None, None] * jnp.where(causal, diff, 0.0)[None]), 0.0)
    zeta = jnp.exp(log_gamma[:, None] * (C - 1.0 - idx)[None, :])
    xi = jnp.exp(log_gamma[:, None] * (idx + 1.0)[None, :])
    widen = lambda t: jnp.repeat(t.T, HEAD_DIM, axis=1)
    cd = jnp.broadcast_to(jnp.exp(log_gamma * C)[:, None, None], (N_HEADS, HEAD_DIM, HEAD_DIM))
    return pl.pallas_call(
        _retention_kernel,
        grid=(B // RET_BATCH, S // C),
        in_specs=[pl.BlockSpec((RET_BATCH, C, 4 * WIDTH), lambda b, c: (b, c, 0)),
                  pl.BlockSpec((C, WIDTH), lambda b, c: (c, 0)),
                  pl.BlockSpec((C, WIDTH), lambda b, c: (c, 0)),
                  _full((C, WIDTH)), _full((C, WIDTH)), _full((N_HEADS, C, C)),
                  _full((N_HEADS, HEAD_DIM, HEAD_DIM)), _full((1, WIDTH)), _full((WIDTH, WIDTH))],
        out_specs=pl.BlockSpec((RET_BATCH, C, WIDTH), lambda b, c: (b, c, 0)),
        out_shape=jax.ShapeDtypeStruct((B, S, WIDTH), BF16),
        scratch_shapes=[pltpu.VMEM((RET_BATCH * N_HEADS, HEAD_DIM, HEAD_DIM), F32)],
        compiler_params=_cparams(("parallel", "arbitrary")),
        name="retention",
    )(z_ret, cos, sin, widen(xi), widen(zeta), decay, cd, gn_g, mavg)


def _rwkv_prep_kernel(z_ref, mu_ref, w0_ref, wup_ref, a0_ref, aup_ref, gup_ref, kk_ref,
                      ka_ref, rk_ref, mones_ref,
                      r_out, lw_out, k_out, v_out, kk_out, kka_out, g_out, bonus_out,
                      carry_ref):
    @pl.when(pl.program_id(1) == 0)
    def _():
        carry_ref[...] = jnp.zeros_like(carry_ref)

    z = z_ref[0]
    n = z.shape[0]
    row = lax.broadcasted_iota(jnp.int32, z.shape, 0)
    prev = jnp.where(row == 0, carry_ref[0:1, :], pltpu.roll(z, 1, 0))
    carry_ref[0:1, :] = z[n - 1:n, :]
    zs = z + (prev - z) * mu_ref[...]
    r = zs[:, 0:WIDTH]
    kr = zs[:, WIDTH:2 * WIDTH]
    vr = zs[:, 2 * WIDTH:3 * WIDTH]
    o = 3 * WIDTH
    wl = zs[:, o:o + DECAY_LORA]
    al = zs[:, o + DECAY_LORA:o + DECAY_LORA + AAA_LORA]
    gl = zs[:, o + DECAY_LORA + AAA_LORA:]
    t = -(w0_ref[...] + _bdot(jnp.tanh(wl), wup_ref[...]))
    softplus = jnp.maximum(t, 0.0) + jnp.log1p(jnp.exp(-jnp.abs(t)))
    w_log = -softplus - 0.5
    a = _sigmoid(a0_ref[...] + _bdot(al, aup_ref[...]))
    g = _bdot(_sigmoid(gl), gup_ref[...])
    mones = mones_ref[...]
    kk = kr * kk_ref[...]
    norm = jnp.sqrt(_seg_dot(kk * kk, mones))
    kk = kk / jnp.maximum(norm, L2_EPS)
    k2 = kr * (1.0 + (a - 1.0) * ka_ref[...])
    r_out[0] = r.astype(BF16)
    lw_out[0] = -jnp.exp(w_log)
    k_out[0] = k2.astype(BF16)
    v_out[0] = vr.astype(BF16)
    kk_out[0] = kk.astype(BF16)
    kka_out[0] = (kk * a).astype(BF16)
    g_out[0] = g.astype(BF16)
    bonus_out[0] = (_seg_dot(r * k2 * rk_ref[...], mones) * vr).astype(BF16)


def _rwkv_prep(z_rwkv, mu, w0, w_up, a0, a_up, g_up, k_k, k_a, r_k, mones):
    B, S, _ = z_rwkv.shape
    ts = ROW_TILE
    row = lambda t: t.reshape(1, -1)
    args = (row(mu), row(w0), w_up.astype(BF16), row(a0), a_up.astype(BF16),
            g_up.astype(BF16), row(k_k), row(k_a), row(r_k), mones)
    out_spec = pl.BlockSpec((1, ts, WIDTH), lambda b, s: (b, s, 0))
    return pl.pallas_call(
        _rwkv_prep_kernel,
        grid=(B, S // ts),
        in_specs=[pl.BlockSpec((1, ts, RWKV_COLS), lambda b, s: (b, s, 0))]
        + [_full(a.shape) for a in args],
        out_specs=[out_spec] * 8,
        out_shape=[jax.ShapeDtypeStruct((B, S, WIDTH), F32 if n == 1 else BF16) for n in range(8)],
        scratch_shapes=[pltpu.VMEM((8, RWKV_COLS), F32)],
        compiler_params=_cparams(("parallel", "arbitrary")),
        name="rwkv_prep",
    )(z_rwkv, *args)


def _wkv7_kernel(r_ref, lw_ref, k_ref, v_ref, kk_ref, kka_ref, tri_ref, o_ref, state_ref):
    @pl.when(pl.program_id(1) == 0)
    def _():
        state_ref[...] = jnp.zeros_like(state_ref)

    L = WKV_CHUNK
    tri = tri_ref[...]
    ri = lax.broadcasted_iota(jnp.int32, (2 * L, L), 0)
    ci = lax.broadcasted_iota(jnp.int32, (2 * L, L), 1)
    mask = jnp.where(ri < L, ri, ri - L + 1) > ci
    eye = (lax.broadcasted_iota(jnp.int32, (L, L), 0)
           == lax.broadcasted_iota(jnp.int32, (L, L), 1)).astype(F32)

    ar, vf, bt, kt, bke, gl = [], [], [], [], [], []
    for b in range(WKV_BATCH):
        lw = lw_ref[b]
        hi = lw.astype(BF16)
        rem = lw - hi.astype(F32)
        mid = rem.astype(BF16)
        lo = (rem - mid.astype(F32)).astype(BF16)
        cum = (jnp.dot(tri, hi, preferred_element_type=F32)
               + jnp.dot(tri, mid, preferred_element_type=F32)
               + jnp.dot(tri, lo, preferred_element_type=F32))
        cum_last = cum[L - 1:L, :]
        inv_g = jnp.exp(-cum)
        to_end = jnp.exp(cum_last - cum)
        g_last = jnp.exp(cum_last)
        kk = kk_ref[b].astype(F32)
        kka = kka_ref[b].astype(F32)
        k = k_ref[b].astype(F32)
        v = v_ref[b].astype(F32)
        a_t = -kk * jnp.exp(cum - lw)
        b_t = kka * inv_g
        k_t = k * inv_g
        r_t = r_ref[b].astype(F32) * jnp.exp(cum)
        b_end = kka * to_end
        k_end = k * to_end
        for h in range(N_HEADS):
            sl = slice(h * HEAD_DIM, (h + 1) * HEAD_DIM)
            ar.append(jnp.concatenate([a_t[:, sl], r_t[:, sl]], axis=0).astype(BF16))
            vf.append(v[:, sl])
            bt.append(b_t[:, sl])
            kt.append(k_t[:, sl])
            bke.append(jnp.concatenate([b_end[:, sl], k_end[:, sl]], axis=0))
            gl.append(g_last[:, sl])
    chains = range(WKV_BATCH * N_HEADS)
    vs = [x.astype(BF16) for x in vf]
    abrb = [jnp.where(mask, _bdot_nt(ar[c], bt[c]), 0.0) for c in chains]
    akrk = [jnp.where(mask, _bdot_nt(ar[c], kt[c]), 0.0) for c in chains]
    p = [m[:L] for m in abrb]
    inv = [eye + m for m in p]
    for _ in range(int(math.log2(L)) - 1):
        p = [_bdot(m, m) for m in p]
        inv = [inv[c] + _bdot(inv[c], p[c]) for c in chains]
    states = [state_ref[c] for c in chains]
    xs = [_bdot_nt(ar[c], states[c]) for c in chains]
    kv = [_bdot(akrk[c], vs[c]) for c in chains]
    u = [_bdot(inv[c], xs[c][:L] + kv[c][:L]) for c in chains]
    y = [xs[c][L:] + kv[c][L:] + _bdot(abrb[c][L:], u[c]) for c in chains]
    for c in chains:
        uv = jnp.concatenate([u[c], vf[c]], axis=0)
        state_ref[c] = states[c] * gl[c] + _bdot_tn(uv, bke[c])
    for b in range(WKV_BATCH):
        o_ref[b] = jnp.concatenate(y[b * N_HEADS:(b + 1) * N_HEADS], axis=1)


def _wkv7(r, lw, k, v, kk, kka):
    B, S, _ = r.shape
    L = WKV_CHUNK
    nb = WKV_BATCH
    tri = jnp.tril(jnp.ones((L, L), F32)).astype(BF16)
    spec = pl.BlockSpec((nb, L, WIDTH), lambda b, c: (b, c, 0))
    return pl.pallas_call(
        _wkv7_kernel,
        grid=(B // nb, S // L),
        in_specs=[spec] * 6 + [_full((L, L))],
        out_specs=spec,
        out_shape=jax.ShapeDtypeStruct((B, S, WIDTH), F32),
        scratch_shapes=[pltpu.VMEM((nb * N_HEADS, HEAD_DIM, HEAD_DIM), F32)],
        compiler_params=_cparams(("parallel", "arbitrary")),
        name="wkv7",
    )(r, lw, k, v, kk, kka, tri)


def _merge_kernel(x_ref, yret_ref, o_ref, bonus_ref, g_ref, zg_ref, gng_ref, gnb_ref,
                  mavg_ref, wret_ref, wrwkv_ref, wo_ref, x1_ref):
    y = _head_norm(o_ref[...], mavg_ref[...], RWKV_GN_EPS)
    y_rwkv = (y * gng_ref[...] + gnb_ref[...] + bonus_ref[...].astype(F32)) * g_ref[...].astype(F32)
    br = _bdot(yret_ref[...], wret_ref[...])
    bw = _bdot(y_rwkv, wrwkv_ref[...])
    zg = zg_ref[...].astype(F32)
    d = br.shape[1]
    merged = _sigmoid(zg[:, :d]) * br + _sigmoid(zg[:, d:]) * bw
    x1_ref[...] = x_ref[...] + _bdot(merged, wo_ref[...])


def _merge(x2, y_ret, o, bonus, g, z_gate, gn_g, gn_b, mavg, w_ret_br, w_rwkv_br, w_o):
    T, D = x2.shape
    tm = ROW_TILE
    rows = lambda n: pl.BlockSpec((tm, n), lambda i: (i, 0))
    consts = (gn_g, gn_b, mavg, w_ret_br, w_rwkv_br, w_o)
    return pl.pallas_call(
        _merge_kernel,
        grid=(T // tm,),
        in_specs=[rows(D), rows(WIDTH), rows(WIDTH), rows(WIDTH), rows(WIDTH), rows(2 * D)]
        + [_full(c.shape) for c in consts],
        out_specs=rows(D),
        out_shape=jax.ShapeDtypeStruct((T, D), F32),
        compiler_params=_cparams(("parallel",)),
        name="merge",
    )(x2, y_ret, o, bonus, g, z_gate, *consts)


def _peer_scores_kernel(x_ref, g_ref, wq_ref, keys_ref, h_ref, s_ref):
    h = _rms(x_ref[...], g_ref[...])
    h_ref[...] = h.reshape(h_ref.shape)
    q = _bdot(h, wq_ref[...]).astype(BF16)
    for grp in range(PEER_GROUPS):
        s_ref[grp] = lax.dot_general(keys_ref[grp], q[:, grp * PEER_HALF:(grp + 1) * PEER_HALF],
                                     _NT, preferred_element_type=F32)


def _peer_scores(x1, g_ffn, w_pq, keys):
    T, D = x1.shape
    tm = ROW_TILE
    return pl.pallas_call(
        _peer_scores_kernel,
        grid=(T // tm,),
        in_specs=[pl.BlockSpec((tm, D), lambda i: (i, 0)), _full((1, D)), _full(w_pq.shape),
                  _full(keys.shape)],
        out_specs=[pl.BlockSpec((tm, D // LANES, LANES), lambda i: (i, 0, 0)),
                   pl.BlockSpec((PEER_GROUPS, PEER_N_KEYS, tm), lambda i: (0, 0, i))],
        out_shape=[jax.ShapeDtypeStruct((T, D // LANES, LANES), F32),
                   jax.ShapeDtypeStruct((PEER_GROUPS, PEER_N_KEYS, T), F32)],
        compiler_params=_cparams(("parallel",)),
        name="peer_scores",
    )(x1, g_ffn, w_pq, keys)


def _top_rows_steps(s, count, payload=None):
    rows = lax.broadcasted_iota(jnp.int32, s.shape, 0).astype(F32)
    vals, picks = [], []
    for _ in range(count):
        m = jnp.max(s, axis=0, keepdims=True)
        idx = jnp.min(jnp.where(s == m, rows, float(s.shape[0])), axis=0, keepdims=True)
        hit = rows == idx
        s = jnp.where(hit, -jnp.inf, s)
        vals.append(m)
        if payload is None:
            picks.append(idx)
        else:
            picks.append(jnp.max(jnp.where(hit, payload, -1), axis=0, keepdims=True))
        yield
    return jnp.concatenate(vals, axis=0), jnp.concatenate(picks, axis=0)


def _run(steps):
    try:
        while True:
            next(steps)
    except StopIteration as done:
        return done.value


def _pair_rows(a, b):
    K = PEER_TOPK
    out = []
    for i in range(K // 2):
        jn = K if i == 0 else K // 2
        out.append((a[i:i + 1, :], b[0:jn, :]))
    out.append((a[K // 2:K, :], b[0:1, :]))
    return out


def _peer_topk_kernel(s_ref, ids_ref, gates_ref):
    K = PEER_TOPK

    def head(h, carry):
        ids, gates = _run(_route_head_steps(lambda: s_ref[2 * h], lambda: s_ref[2 * h + 1]))
        off = pl.multiple_of(h * K, K)
        ids_ref[pl.ds(off, K), :] = ids
        gates_ref[pl.ds(off, K), :] = gates
        return carry

    lax.fori_loop(0, PEER_HEADS, head, 0)


def _route_head_steps(load_scores0, load_scores1):
    K = PEER_TOPK
    s0, i0 = yield from _top_rows_steps(load_scores0(), K)
    s1, i1 = yield from _top_rows_steps(load_scores1(), K)
    e0 = i0.astype(jnp.int32) * PEER_N_KEYS
    e1 = i1.astype(jnp.int32)
    cand = jnp.concatenate([x + y for x, y in _pair_rows(s0, s1)], axis=0)
    cand_id = jnp.concatenate([x + y for x, y in _pair_rows(e0, e1)], axis=0)
    best, ids = yield from _top_rows_steps(cand, K, payload=cand_id)
    e = jnp.exp(best - best[0:1, :])
    return ids, e / jnp.sum(e, axis=0, keepdims=True)


def _peer_topk(scores_t, n_tokens):
    T = n_tokens
    tk = PEER_ROUTE_TOK
    return pl.pallas_call(
        _peer_topk_kernel,
        grid=(T // tk,),
        in_specs=[pl.BlockSpec((PEER_GROUPS, PEER_N_KEYS, tk), lambda i: (0, 0, i))],
        out_specs=[pl.BlockSpec((PEER_SEL, tk), lambda i: (0, i))] * 2,
        out_shape=[jax.ShapeDtypeStruct((PEER_SEL, T), jnp.int32),
                   jax.ShapeDtypeStruct((PEER_SEL, T), F32)],
        compiler_params=_cparams(("parallel",)),
        name="peer_topk",
    )(scores_t)


def _gelu_tanh(x):
    return 0.5 * x * (1.0 + jnp.tanh(math.sqrt(2.0 / math.pi) * (x + 0.044715 * x * x * x)))


def _peer_experts_kernel(off_ref, ids_cur, ids_next, gexp_ref, h_ref, uv_hbm, s_ref, out_ref, nids_ref,
                         ngates_ref, buf, sem, rids_scr, rgates_scr):
    i = pl.program_id(0)
    n = pl.num_programs(0)
    sub = h_ref.shape[1]
    group_rows = PEER_TOK * PEER_SEL

    def issue_token(ids_ref, tok, group, t, k0=0, count=PEER_SEL):
        for k in range(k0, k0 + count):
            e = ids_ref[0, tok, k]
            pltpu.make_async_copy(uv_hbm.at[pl.ds(e, 1)],
                                  buf.at[group, pl.ds(t * PEER_SEL + k, 1)],
                                  sem.at[group]).start(priority=k % 2)

    def wait_group(group):
        pltpu.make_async_copy(uv_hbm.at[pl.ds(0, group_rows)], buf.at[group], sem.at[group]).wait()

    @pl.when(i == 0)
    def _():
        for group in range(PEER_AHEAD):
            def body(t, carry, group=group):
                issue_token(ids_cur, group * PEER_TOK + t, group, t)
                return carry
            lax.fori_loop(0, PEER_TOK, body, 0)

    ch = PEER_CHUNK
    n_ch = PEER_SEL // ch
    ccols = ch * sub
    lane = lax.broadcasted_iota(jnp.int32, (sub, ccols), 1)
    diag = (lane % sub) == lax.broadcasted_iota(jnp.int32, (sub, ccols), 0)

    def group_sum(x):
        step = 1
        while step < sub:
            partner = jnp.where((lane % (2 * step)) < step, pltpu.roll(x, ccols - step, 1),
                                pltpu.roll(x, step, 1))
            x = x + partner
            step *= 2
        return x

    tokens = PEER_GROUPS_PER_STEP * PEER_TOK
    burst = PEER_SEL // (2 * n_ch)
    issued = [0]

    def routing_steps():
        routed = []
        for j in range(s_ref.shape[0] // 2):
            routed.append((yield from _route_head_steps(lambda j=j: s_ref[2 * j],
                                                        lambda j=j: s_ref[2 * j + 1])))
        return routed

    routing = routing_steps()
    routed = []

    def routing_tick():
        if not routed:
            try:
                next(routing)
            except StopIteration as done:
                routed.extend(done.value)

    def request_burst():
        for r in range(issued[0], issued[0] + burst):
            tok, k = divmod(r, PEER_SEL)
            ahead = tok // PEER_TOK + PEER_AHEAD
            ids_ref = ids_cur if ahead < PEER_GROUPS_PER_STEP else ids_next
            group = ahead % PEER_GROUPS_PER_STEP
            issue_token(ids_ref, group * PEER_TOK + tok % PEER_TOK, group, tok % PEER_TOK, k, 1)
        issued[0] += burst
        if (issued[0] // burst) % PEER_ROUTE_EVERY == 0:
            routing_tick()

    def first_layer(tok):
        group, t = divmod(tok, PEER_TOK)
        if t == 0:
            wait_group(group)
        hb = h_ref[tok].astype(BF16)
        ps = []
        for c in range(n_ch):
            request_burst()
            rows = pl.ds(t * PEER_SEL + c * ch, ch)
            uc = buf[group, rows, 0:sub, :].reshape(ccols, LANES).astype(BF16)
            ps.append(lax.dot_general(hb, uc, _NT, preferred_element_type=F32))
        return ps

    def weights(tok, ps):
        ws = []
        for c in range(n_ch):
            act = jnp.sum(group_sum(jnp.where(diag, ps[c], 0.0)), axis=0, keepdims=True)
            w = _gelu_tanh(act) * gexp_ref[tok:tok + 1, c * ccols:(c + 1) * ccols]
            ws.append(jnp.where(diag, jnp.broadcast_to(w, (sub, ccols)), 0.0).astype(BF16))
        return ws

    def second_layer(tok, ws):
        group, t = divmod(tok, PEER_TOK)
        o = None
        for c in range(n_ch):
            request_burst()
            rows = pl.ds(t * PEER_SEL + c * ch, ch)
            vc = buf[group, rows, sub:2 * sub, :].reshape(ccols, LANES).astype(BF16)
            part = jnp.dot(ws[c], vc, preferred_element_type=F32)
            o = part if o is None else o + part
        return o

    ps = {tok: first_layer(tok) for tok in range(PEER_SKEW)}
    outs = []
    for tok in range(tokens):
        if tok + PEER_SKEW < tokens:
            ps[tok + PEER_SKEW] = first_layer(tok + PEER_SKEW)
        outs.append(second_layer(tok, weights(tok, ps.pop(tok))))
    assert issued[0] == tokens * PEER_SEL
    while not routed:
        routing_tick()
    for tok, o in enumerate(outs):
        out_ref[tok] = o
    rsteps = PEER_SEL // (len(routed) * PEER_TOPK)
    part = i % rsteps
    for j, (ids, gates) in enumerate(routed):
        row0 = pl.multiple_of((part * len(routed) + j) * PEER_TOPK, PEER_TOPK)
        rids_scr[pl.ds(row0, PEER_TOPK), :] = ids
        rgates_scr[pl.ds(row0, PEER_TOPK), :] = gates

    @pl.when(part == rsteps - 1)
    def _():
        nids_ref[...] = rids_scr[...].T
        ngates_ref[...] = rgates_scr[...].T

    @pl.when(i == n - 1)
    def _():
        for group in range(PEER_AHEAD):
            wait_group(group)


def _peer_experts(ids, gates, h_tiles, uv, scores_t, rng, next_rng, n_ranges):
    T, sub, _ = h_tiles.shape
    R = T // n_ranges
    tb = PEER_GROUPS_PER_STEP * PEER_TOK
    nb = R // tb
    rsteps = PEER_ROUTE_TOK // tb
    rgroups = PEER_GROUPS // rsteps
    assert sub == SUBLANES, "a model vector must be exactly one (sublanes, lanes) tile"
    assert T == n_ranges * nb * tb and nb % rsteps == 0 and PEER_ROUTE_TOK == rsteps * tb
    assert PEER_GROUPS % rsteps == 0 and rgroups % 2 == 0
    assert PEER_AHEAD < PEER_GROUPS_PER_STEP and PEER_SKEW < tb and PEER_SEL % PEER_CHUNK == 0
    ids3 = ids.reshape(nb, tb, PEER_SEL)
    gexp = jnp.repeat(gates, sub, axis=1)
    offsets = jnp.array([rng * nb, next_rng * (R // PEER_ROUTE_TOK)], jnp.int32)
    smem = lambda imap: pl.BlockSpec((1, tb, PEER_SEL), imap, memory_space=pltpu.SMEM)
    routed = pl.BlockSpec((PEER_ROUTE_TOK, PEER_SEL), lambda i, off: (i // rsteps, 0))
    grid_spec = pltpu.PrefetchScalarGridSpec(
        num_scalar_prefetch=1,
        grid=(nb,),
        in_specs=[smem(lambda i, off: (i, 0, 0)),
                  smem(lambda i, off: (jnp.minimum(i + 1, nb - 1), 0, 0)),
                  pl.BlockSpec((tb, PEER_SEL * sub), lambda i, off: (i, 0)),
                  pl.BlockSpec((tb, sub, LANES), lambda i, off: (off[0] + i, 0, 0)),
                  pl.BlockSpec(memory_space=pl.ANY),
                  pl.BlockSpec((rgroups, PEER_N_KEYS, PEER_ROUTE_TOK),
                               lambda i, off: (i % rsteps, 0, off[1] + i // rsteps))],
        out_specs=[pl.BlockSpec((tb, sub, LANES), lambda i, off: (i, 0, 0)), routed, routed],
        scratch_shapes=[pltpu.VMEM((PEER_GROUPS_PER_STEP, PEER_TOK * PEER_SEL, 2 * sub, LANES), F32),
                        pltpu.SemaphoreType.DMA((PEER_GROUPS_PER_STEP,)),
                        pltpu.VMEM((PEER_SEL, PEER_ROUTE_TOK), jnp.int32),
                        pltpu.VMEM((PEER_SEL, PEER_ROUTE_TOK), F32)])
    return pl.pallas_call(
        _peer_experts_kernel,
        grid_spec=grid_spec,
        out_shape=[jax.ShapeDtypeStruct((R, sub, LANES), F32),
                   jax.ShapeDtypeStruct((R, PEER_SEL), jnp.int32),
                   jax.ShapeDtypeStruct((R, PEER_SEL), F32)],
        compiler_params=_cparams(("arbitrary",)),
        name="peer_experts",
    )(offsets, ids3, ids3, gexp, h_tiles, uv, scores_t)


def _ple_final_kernel(x_ref, d_ref, p_ref, gple_ref, wg_ref, wu_ref, gfin_ref, out_ref):
    x = x_ref[...] + d_ref[...].reshape(x_ref.shape)
    gate = _sigmoid(_bdot(_rms(x, gple_ref[...]), wg_ref[...]))
    x = x + gate * _bdot(p_ref[...], wu_ref[...])
    out_ref[...] = _rms(x, gfin_ref[...])


def _ple_final(x2, delta_tiles, p2, g_ple, w_gate, w_up, g_final):
    T, D = x2.shape
    tm = ROW_TILE
    consts = (g_ple, w_gate, w_up, g_final)
    return pl.pallas_call(
        _ple_final_kernel,
        grid=(T // tm,),
        in_specs=[pl.BlockSpec((tm, D), lambda i: (i, 0)),
                  pl.BlockSpec((tm,) + delta_tiles.shape[1:], lambda i: (i, 0, 0)),
                  pl.BlockSpec((tm, p2.shape[1]), lambda i: (i, 0)),
                  _full(g_ple.shape), _full(w_gate.shape), _full(w_up.shape), _full(g_final.shape)],
        out_specs=pl.BlockSpec((tm, D), lambda i: (i, 0)),
        out_shape=jax.ShapeDtypeStruct((T, D), F32),
        compiler_params=_cparams(("parallel",)),
        name="ple_final",
    )(x2, delta_tiles, p2, *consts)


def kernel(x, p, g_mix, w_in, ret_gn_g, rwkv_mu, rwkv_w0, rwkv_w_up, rwkv_a0, rwkv_a_up, rwkv_g_up, rwkv_k_k, rwkv_k_a, rwkv_r_k, rwkv_gn_g, rwkv_gn_b, w_ret_br, w_rwkv_br, w_o, g_ffn, w_pq, peer_sub_keys, peer_u, peer_v, g_ple, w_ple_gate, w_ple_up, g_final):
    B, S, D = x.shape
    T = B * S
    assert w_in.shape[0] == 1, "single-layer block: the final RMSNorm is fused into its last step"
    assert T % ROW_TILE == 0 and S % ROW_TILE == 0 and S % RET_CHUNK == 0 and S % WKV_CHUNK == 0
    assert B % RET_BATCH == 0 and B % WKV_BATCH == 0 and D % LANES == 0
    assert T % (PEER_RANGES * PEER_ROUTE_TOK) == 0
    i = 0
    row = lambda t: t.reshape(1, -1)
    head_of = jnp.arange(WIDTH) // HEAD_DIM
    same_head = head_of[:, None] == head_of[None, :]
    mones = same_head.astype(BF16)
    mavg = (same_head.astype(F32) / HEAD_DIM).astype(BF16)
    ret_cols = 4 * WIDTH
    x2 = x.reshape(T, D)
    wi = w_in[i].astype(BF16)
    z_ret, z_rwkv, z_gate = _in_proj(
        x2, row(g_mix[i]), wi[:, :ret_cols], wi[:, ret_cols:ret_cols + RWKV_COLS],
        wi[:, ret_cols + RWKV_COLS:])
    y_ret = _retention(z_ret.reshape(B, S, ret_cols), row(ret_gn_g[i]), mavg)
    r, lw, k, v, kk, kka, g, bonus = _rwkv_prep(
        z_rwkv.reshape(B, S, RWKV_COLS), rwkv_mu[i], rwkv_w0[i], rwkv_w_up[i], rwkv_a0[i],
        rwkv_a_up[i], rwkv_g_up[i], rwkv_k_k[i], rwkv_k_a[i], rwkv_r_k[i], mones)
    o = _wkv7(r, lw, k, v, kk, kka)
    flat = lambda t: t.reshape(T, WIDTH)
    x2 = _merge(x2, flat(y_ret), flat(o), flat(bonus), flat(g), z_gate,
                row(rwkv_gn_g[i]), row(rwkv_gn_b[i]), mavg, w_ret_br[i].astype(BF16),
                w_rwkv_br[i].astype(BF16), w_o[i].astype(BF16))
    keys = peer_sub_keys[i].reshape(PEER_GROUPS, PEER_N_KEYS, PEER_HALF).astype(BF16)
    h_tiles, scores_t = _peer_scores(x2, row(g_ffn[i]), w_pq[i].astype(BF16), keys)
    n_exp = peer_u.shape[1]
    uv = jnp.concatenate([peer_u[i].reshape(n_exp, D // LANES, LANES),
                          peer_v[i].reshape(n_exp, D // LANES, LANES)], axis=1)
    R = T // PEER_RANGES
    ids_t, gates_t = _peer_topk(scores_t, R)
    ids, gates = ids_t.T, gates_t.T
    deltas = []
    for c in range(PEER_RANGES):
        delta, ids, gates = _peer_experts(ids, gates, h_tiles, uv, scores_t, c,
                                          min(c + 1, PEER_RANGES - 1), PEER_RANGES)
        deltas.append(delta)
    delta_tiles = jnp.concatenate(deltas, axis=0)
    out = _ple_final(x2, delta_tiles, p[i].reshape(T, -1), row(g_ple[i]),
                     w_ple_gate[i].astype(BF16), w_ple_up[i].astype(BF16), row(g_final))
    return out.reshape(B, S, D)
```
